```python
import math
import jax, jax.numpy as jnp
from jax import lax
import numpy as np

D_MODEL = 1024
BATCH = 8
SEQ = 2048
DEPTH = 2

GRID_W = 64
CTX_LEN = 256
S5_WIDTH = 512
S5_GROUP = 16
S5_GROUPS = S5_WIDTH // S5_GROUP
S5_STATE = 64
RET_HEADS = 4
RET_DK = 64
RET_DV = 128
RET_QK_WIDTH = RET_HEADS * RET_DK
RET_WIDTH = RET_HEADS * RET_DV
RET_CHUNK = 128
NA_HEADS = 8
NA_HEAD_DIM = 64
NA_WIDTH = NA_HEADS * NA_HEAD_DIM
NA_ROWS = 8
NA_COLS = 16
N_BRANCH = 3
FFN_HIDDEN = ((8 * D_MODEL + 3 * 256 - 1) // (3 * 256)) * 256
ROPE_BASE = 10000.0
RMS_EPS = 1e-6
GN_EPS = 1e-5
IN_SPLIT = (S5_WIDTH, RET_QK_WIDTH, RET_WIDTH, NA_WIDTH, NA_WIDTH,
            RET_QK_WIDTH, RET_WIDTH, NA_WIDTH, N_BRANCH * D_MODEL)
N_CTX_KV = 5
N_IN = sum(IN_SPLIT)

kernel_name = 'hybrid_s5_retention_natten_dit_block'


def rms_norm(x):
    xf = x.astype(jnp.float32)
    return (xf * lax.rsqrt(jnp.mean(xf * xf, axis=-1, keepdims=True) + RMS_EPS)).astype(x.dtype)


def split_cols(t, sizes):
    return jnp.split(t, np.cumsum(sizes)[:-1].tolist(), axis=-1)


def heads(t, n):
    return t.reshape(t.shape[:-1] + (n, t.shape[-1] // n))


def seq_order(t, reverse):
    return jnp.flip(t, axis=1) if reverse else t


def axial_rotary(t):
    L, d = t.shape[1], t.shape[-1]
    half, quarter = d // 2, d // 4
    pos = jnp.arange(L)
    inv_freq = ROPE_BASE ** (-jnp.arange(quarter, dtype=jnp.float32) / quarter)

    def rotate(part, coord):
        ang = coord.astype(jnp.float32)[:, None] * inv_freq[None, :]
        cos = jnp.cos(ang)[None, :, None, :].astype(t.dtype)
        sin = jnp.sin(ang)[None, :, None, :].astype(t.dtype)
        a, b = part[..., :quarter], part[..., quarter:]
        return jnp.concatenate([a * cos - b * sin, a * sin + b * cos], axis=-1)

    return jnp.concatenate([rotate(t[..., :half], pos // GRID_W),
                            rotate(t[..., half:], pos % GRID_W)], axis=-1)


def _ssm_combine(e1, e2):
    a1r, a1i, b1r, b1i = e1
    a2r, a2i, b2r, b2i = e2
    return (a1r * a2r - a1i * a2i, a1r * a2i + a1i * a2r,
            a2r * b1r - a2i * b1i + b2r, a2r * b1i + a2i * b1r + b2i)


def s5_discretize(lam_re, lam_im, log_dt, b_re, b_im):
    f32 = jnp.float32
    lam_re, lam_im = lam_re.astype(f32), lam_im.astype(f32)
    dt = jnp.exp(log_dt.astype(f32))[:, None]
    mag, ang = jnp.exp(lam_re * dt), lam_im * dt
    ab_re, ab_im = mag * jnp.cos(ang), mag * jnp.sin(ang)
    den = lam_re * lam_re + lam_im * lam_im
    f_re = ((ab_re - 1.0) * lam_re + ab_im * lam_im) / den
    f_im = (ab_im * lam_re - (ab_re - 1.0) * lam_im) / den
    b_re, b_im = b_re.astype(f32), b_im.astype(f32)
    bb_re = f_re[..., None] * b_re - f_im[..., None] * b_im
    bb_im = f_re[..., None] * b_im + f_im[..., None] * b_re
    return ab_re, ab_im, bb_re, bb_im


def s5_states(u, disc, x0):
    ab_re, ab_im, bb_re, bb_im = disc
    u = u.astype(jnp.float32)
    bu_re = jnp.einsum('blgh,gph->blgp', u, bb_re)
    bu_im = jnp.einsum('blgh,gph->blgp', u, bb_im)
    shape = (1, u.shape[1]) + ab_re.shape
    a_re, a_im, x_re, x_im = lax.associative_scan(
        _ssm_combine, (jnp.broadcast_to(ab_re, shape), jnp.broadcast_to(ab_im, shape), bu_re, bu_im), axis=1)
    if x0 is not None:
        x0_re, x0_im = x0[0][:, None], x0[1][:, None]
        x_re, x_im = x_re + a_re * x0_re - a_im * x0_im, x_im + a_re * x0_im + a_im * x0_re
    return x_re, x_im


def s5_readout(x_re, x_im, c_re, c_im):
    f32 = jnp.float32
    return (jnp.einsum('ghp,blgp->blgh', c_re.astype(f32), x_re)
            - jnp.einsum('ghp,blgp->blgh', c_im.astype(f32), x_im))


def s5_glu(y, w_glu, b_glu):
    g = jax.nn.gelu(y)
    return g * jax.nn.sigmoid(g @ w_glu + b_glu)


def s5_mixer(u_lat, u_ctx, lam_re, lam_im, log_dt, b_re, b_im, c_re, c_im, d_skip, w_glu, b_glu, need_ctx):
    bsz, L, _ = u_lat.shape
    n_ctx = u_ctx.shape[1]
    ul = u_lat.reshape(bsz, L, S5_GROUPS, S5_GROUP)
    uc = u_ctx.reshape(bsz, n_ctx, S5_GROUPS, S5_GROUP)
    d32 = d_skip.astype(jnp.float32)
    y_lat = d32 * u_lat.astype(jnp.float32)
    y_ctx = d32 * u_ctx.astype(jnp.float32) if need_ctx else None
    for dirn in range(2):
        rev = dirn == 1
        disc = s5_discretize(lam_re[dirn], lam_im[dirn], log_dt[dirn], b_re[dirn], b_im[dirn])
        xc_re, xc_im = s5_states(seq_order(uc, rev), disc, None)
        xl_re, xl_im = s5_states(seq_order(ul, rev), disc, (xc_re[:, -1], xc_im[:, -1]))
        y_lat = y_lat + seq_order(s5_readout(xl_re, xl_im, c_re[dirn], c_im[dirn]), rev).reshape(bsz, L, S5_WIDTH)
        if need_ctx:
            y_ctx = y_ctx + seq_order(s5_readout(xc_re, xc_im, c_re[dirn], c_im[dirn]), rev).reshape(bsz, n_ctx, S5_WIDTH)
    out_lat = s5_glu(y_lat.astype(u_lat.dtype), w_glu, b_glu)
    out_ctx = s5_glu(y_ctx.astype(u_ctx.dtype), w_glu, b_glu) if need_ctx else None
    return out_lat, out_ctx


def retention_chunkwise(q, k, v, log_gamma, s0, strict):
    bsz, L, H, dk = q.shape
    dv = v.shape[-1]
    cs = RET_CHUNK
    n = L // cs
    qc = q.reshape(bsz, n, cs, H, dk)
    kc = k.reshape(bsz, n, cs, H, dk)
    vc = v.reshape(bsz, n, cs, H, dv)
    idx = jnp.arange(cs, dtype=jnp.float32)
    diff = idx[:, None] - idx[None, :]
    mask = diff > 0 if strict else diff >= 0
    intra = jnp.where(mask[None], jnp.exp(jnp.where(mask, diff, 0.0)[None] * log_gamma[:, None, None]), 0.0)
    scores = jnp.einsum('bcihd,bcjhd->bchij', qc, kc) * intra
    o = jnp.einsum('bchij,bcjhe->bcihe', scores, vc)
    k_dec = kc * jnp.exp((cs - 1 - idx)[:, None] * log_gamma[None, :])[:, :, None]
    kv = jnp.einsum('bcjhd,bcjhe->bchde', k_dec, vc)
    chunk_decay = jnp.exp(cs * log_gamma)[:, None, None]

    def step(s, kv_c):
        return chunk_decay * s + kv_c, s

    s_final, s_in = lax.scan(step, s0, jnp.moveaxis(kv, 1, 0))
    q_dec = qc * jnp.exp((idx + 1.0)[:, None] * log_gamma[None, :])[:, :, None]
    o = o + jnp.einsum('bcihd,cbhde->bcihe', q_dec, s_in)
    return o.reshape(bsz, L, H, dv), s_final


def retention_final_state(k, v, log_gamma):
    L = k.shape[1]
    w = jnp.exp((L - 1 - jnp.arange(L, dtype=jnp.float32))[:, None] * log_gamma[None, :])
    return jnp.einsum('blhd,blhe->bhde', k * w[:, :, None], v)


def head_norm(o):
    mu = jnp.mean(o, axis=-1, keepdims=True)
    var = jnp.mean(jnp.square(o - mu), axis=-1, keepdims=True)
    return (o - mu) * lax.rsqrt(var + GN_EPS)


def retention_mixer(q, k, v, g, q_c, k_c, v_c, g_c, theta, need_ctx):
    f32 = jnp.float32
    scale = RET_DK ** -0.5
    q = axial_rotary(q).astype(f32)
    k = (axial_rotary(k) * scale).astype(f32)
    v = v.astype(f32)
    k_c = (k_c * scale).astype(f32)
    v_c = v_c.astype(f32)
    log_gamma = jax.nn.log_sigmoid(theta.astype(f32))
    bsz = q.shape[0]
    o_lat, o_ctx = [], []
    for dirn in range(2):
        rev = dirn == 1
        lg = log_gamma[dirn]
        if need_ctx:
            s0 = jnp.zeros((bsz, RET_HEADS, RET_DK, RET_DV), f32)
            oc, s_ctx = retention_chunkwise(seq_order(q_c.astype(f32), rev), seq_order(k_c, rev),
                                            seq_order(v_c, rev), lg, s0, rev)
            o_ctx.append(seq_order(oc, rev))
        else:
            s_ctx = retention_final_state(seq_order(k_c, rev), seq_order(v_c, rev), lg)
        ol, _ = retention_chunkwise(seq_order(q, rev), seq_order(k, rev), seq_order(v, rev), lg, s_ctx, rev)
        o_lat.append(seq_order(ol, rev))
    y_lat = jax.nn.silu(g) * head_norm(o_lat[0] + o_lat[1]).reshape(g.shape).astype(g.dtype)
    y_ctx = (jax.nn.silu(g_c) * head_norm(o_ctx[0] + o_ctx[1]).reshape(g_c.shape).astype(g_c.dtype)
             if need_ctx else None)
    return y_lat, y_ctx


def neighborhood_attention(q, k, v, k_c, v_c, rpb):
    f32 = jnp.float32
    bsz, L, H, d = q.shape
    rows = L // GRID_W
    wr = min(NA_ROWS, rows)
    n_win = wr * NA_COLS
    qg = (q * d ** -0.5).reshape(bsz, rows, GRID_W, H, d)
    kg = k.reshape(bsz, rows, GRID_W, H, d)
    vg = v.reshape(bsz, rows, GRID_W, H, d)
    row_start = jnp.clip(jnp.arange(rows) - wr // 2, 0, rows - wr)
    cols = jnp.arange(GRID_W)
    col_idx = jnp.clip(cols - NA_COLS // 2, 0, GRID_W - NA_COLS)[:, None] + jnp.arange(NA_COLS)[None, :]
    col_bias_idx = col_idx - cols[:, None] + NA_COLS - 1
    rpb32 = rpb.astype(f32)

    def row_block(args):
        q_r, r = args
        rs = row_start[r]
        k_win = lax.dynamic_slice_in_dim(kg, rs, wr, axis=1)[:, :, col_idx]
        v_win = lax.dynamic_slice_in_dim(vg, rs, wr, axis=1)[:, :, col_idx]
        row_bias_idx = rs + jnp.arange(wr) - r + NA_ROWS - 1
        bias = rpb32[:, row_bias_idx[:, None, None], col_bias_idx[None]]
        s_win = jnp.einsum('bqhd,brqchd->bhqrc', q_r, k_win).astype(f32) + jnp.transpose(bias, (0, 2, 1, 3))[None]
        s_ctx = jnp.einsum('bqhd,bkhd->bhqk', q_r, k_c).astype(f32)
        s = jnp.concatenate([s_win.reshape(bsz, H, GRID_W, n_win), s_ctx], axis=-1)
        p = jax.nn.softmax(s, axis=-1).astype(v.dtype)
        p_win = p[..., :n_win].reshape(bsz, H, GRID_W, wr, NA_COLS)
        return (jnp.einsum('bhqrc,brqchd->bqhd', p_win, v_win)
                + jnp.einsum('bhqk,bkhd->bqhd', p[..., n_win:], v_c))

    out = lax.map(row_block, (jnp.moveaxis(qg, 1, 0), jnp.arange(rows)))
    return jnp.moveaxis(out, 0, 1).reshape(bsz, L, H * d)


def context_attention(q, k, v):
    bsz, n, H, d = q.shape
    s = jnp.einsum('bqhd,bkhd->bhqk', q * d ** -0.5, k).astype(jnp.float32)
    p = jax.nn.softmax(s, axis=-1).astype(v.dtype)
    return jnp.einsum('bhqk,bkhd->bqhd', p, v).reshape(bsz, n, H * d)


def merge_branches(y_s5, y_ret, y_na, gates, w_bs5, w_bret, w_bna, w_out):
    g_s5, g_ret, g_na = jnp.split(jax.nn.sigmoid(gates), N_BRANCH, axis=-1)
    m = g_s5 * (y_s5 @ w_bs5) + g_ret * (y_ret @ w_bret) + g_na * (y_na @ w_bna)
    return m @ w_out


def swiglu(h, w_gate, w_up, w_down):
    return (jax.nn.silu(h @ w_gate) * (h @ w_up)) @ w_down


def setup_inputs(seed: int = 0) -> dict:
    key = jax.random.key(seed)
    ks = iter(jax.random.split(key, 32))
    f32 = jnp.float32

    def normal(shape, std):
        return jax.random.normal(next(ks), shape, f32) * std

    G, P, HG = S5_GROUPS, S5_STATE, S5_GROUP
    ret_init = jnp.log(2.0 ** (5.0 + jnp.arange(RET_HEADS, dtype=f32)) - 1.0)
    return {
        'x': normal((BATCH, SEQ, D_MODEL), 1.0),
        'c': normal((BATCH, D_MODEL), 1.0),
        'ctx': normal((BATCH, CTX_LEN, D_MODEL), 1.0),
        'c_ctx': normal((D_MODEL,), 1.0),
        'w_ada': normal((DEPTH, D_MODEL, 6 * D_MODEL), 0.5 * D_MODEL ** -0.5),
        'b_ada': normal((DEPTH, 6 * D_MODEL), 0.01),
        'w_in': normal((DEPTH, D_MODEL, N_IN), D_MODEL ** -0.5),
        's5_lam_re': -0.5 + normal((DEPTH, 2, G, P), 0.01),
        's5_lam_im': jnp.pi * jnp.arange(P, dtype=f32) + normal((DEPTH, 2, G, P), 0.01),
        's5_log_dt': jax.random.uniform(next(ks), (DEPTH, 2, G), f32, math.log(1e-3), math.log(1e-1)),
        's5_b_re': normal((DEPTH, 2, G, P, HG), (2.0 * HG) ** -0.5),
        's5_b_im': normal((DEPTH, 2, G, P, HG), (2.0 * HG) ** -0.5),
        's5_c_re': normal((DEPTH, 2, G, HG, P), (2.0 * P) ** -0.5),
        's5_c_im': normal((DEPTH, 2, G, HG, P), (2.0 * P) ** -0.5),
        's5_d': normal((DEPTH, S5_WIDTH), 1.0),
        's5_w_glu': normal((DEPTH, S5_WIDTH, S5_WIDTH), S5_WIDTH ** -0.5),
        's5_b_glu': normal((DEPTH, S5_WIDTH), 0.01),
        'ret_theta': ret_init + normal((DEPTH, 2, RET_HEADS), 0.01),
        'na_rpb': normal((DEPTH, NA_HEADS, 2 * NA_ROWS - 1, 2 * NA_COLS - 1), 0.02),
        'w_branch_s5': normal((DEPTH, S5_WIDTH, D_MODEL), S5_WIDTH ** -0.5),
        'w_branch_ret': normal((DEPTH, RET_WIDTH, D_MODEL), RET_WIDTH ** -0.5),
        'w_branch_na': normal((DEPTH, NA_WIDTH, D_MODEL), NA_WIDTH ** -0.5),
        'w_out': normal((DEPTH, D_MODEL, D_MODEL), D_MODEL ** -0.5),
        'w_ffn_gate': normal((DEPTH, D_MODEL, FFN_HIDDEN), D_MODEL ** -0.5),
        'w_ffn_up': normal((DEPTH, D_MODEL, FFN_HIDDEN), D_MODEL ** -0.5),
        'w_ffn_down': normal((DEPTH, FFN_HIDDEN, D_MODEL), FFN_HIDDEN ** -0.5),
        'final_norm': 1.0 + normal((D_MODEL,), 0.01),
    }


def reference(x, c, ctx, c_ctx, w_ada, b_ada, w_in, s5_lam_re, s5_lam_im, s5_log_dt, s5_b_re, s5_b_im,
              s5_c_re, s5_c_im, s5_d, s5_w_glu, s5_b_glu, ret_theta, na_rpb, w_branch_s5, w_branch_ret,
              w_branch_na, w_out, w_ffn_gate, w_ffn_up, w_ffn_down, final_norm):
    silu_c = jax.nn.silu(c)
    silu_cc = jax.nn.silu(c_ctx)
    for l in range(DEPTH):
        need_ctx = l < DEPTH - 1
        mod = (silu_c @ w_ada[l] + b_ada[l])[:, None, :]
        mod_c = silu_cc @ w_ada[l] + b_ada[l]
        sh1, sc1, g1, sh2, sc2, g2 = jnp.split(mod, 6, axis=-1)
        csh1, csc1, cg1, csh2, csc2, cg2 = jnp.split(mod_c, 6, axis=-1)

        h = rms_norm(x) * (1.0 + sc1) + sh1
        hc = rms_norm(ctx) * (1.0 + csc1) + csh1
        u, rk, rv, nk, nv, rq, rg, nq, gates = split_cols(h @ w_in[l], IN_SPLIT)
        n_pieces = len(IN_SPLIT) if need_ctx else N_CTX_KV
        ctx_cols = sum(IN_SPLIT[:n_pieces])
        cp = split_cols(hc @ w_in[l][:, :ctx_cols], IN_SPLIT[:n_pieces])
        cu, crk, crv, cnk, cnv = cp[:N_CTX_KV]
        crq, crg, cnq, cgates = cp[N_CTX_KV:] if need_ctx else (None, None, None, None)

        y_s5, y_s5_c = s5_mixer(u, cu, s5_lam_re[l], s5_lam_im[l], s5_log_dt[l], s5_b_re[l], s5_b_im[l],
                                s5_c_re[l], s5_c_im[l], s5_d[l], s5_w_glu[l], s5_b_glu[l], need_ctx)
        y_ret, y_ret_c = retention_mixer(
            heads(rq, RET_HEADS), heads(rk, RET_HEADS), heads(rv, RET_HEADS), rg,
            heads(crq, RET_HEADS) if need_ctx else None, heads(crk, RET_HEADS), heads(crv, RET_HEADS),
            crg, ret_theta[l], need_ctx)
        k_na_c, v_na_c = heads(cnk, NA_HEADS), heads(cnv, NA_HEADS)
        y_na = neighborhood_attention(heads(nq, NA_HEADS), heads(nk, NA_HEADS), heads(nv, NA_HEADS),
                                      k_na_c, v_na_c, na_rpb[l])
        x = x + g1 * merge_branches(y_s5, y_ret, y_na, gates, w_branch_s5[l], w_branch_ret[l],
                                    w_branch_na[l], w_out[l])
        if need_ctx:
            y_na_c = context_attention(heads(cnq, NA_HEADS), k_na_c, v_na_c)
            ctx = ctx + cg1 * merge_branches(y_s5_c, y_ret_c, y_na_c, cgates, w_branch_s5[l],
                                             w_branch_ret[l], w_branch_na[l], w_out[l])

        x = x + g2 * swiglu(rms_norm(x) * (1.0 + sc2) + sh2, w_ffn_gate[l], w_ffn_up[l], w_ffn_down[l])
        if need_ctx:
            ctx = ctx + cg2 * swiglu(rms_norm(ctx) * (1.0 + csc2) + csh2, w_ffn_gate[l], w_ffn_up[l],
                                     w_ffn_down[l])
    return rms_norm(x) * final_norm
```

```python
import functools
import math

import numpy as np
import jax
import jax.numpy as jnp
from jax import lax
from jax.experimental import pallas as pl
from jax.experimental.pallas import tpu as pltpu

F32 = jnp.float32
BF16 = jnp.bfloat16

D_MODEL = 1024
GRID_W = 64
S5_WIDTH = 512
S5_GROUP = 16
S5_GROUPS = S5_WIDTH // S5_GROUP
S5_STATE = 64
S5_CHUNK = 16
RET_HEADS = 4
RET_DK = 64
RET_DV = 128
RET_QK_WIDTH = RET_HEADS * RET_DK
RET_WIDTH = RET_HEADS * RET_DV
RET_T = 256
NA_HEADS = 8
NA_HEAD_DIM = 64
NA_WIDTH = NA_HEADS * NA_HEAD_DIM
NA_ROWS = 8
NA_COLS = 16
NA_QROWS = 4
NA_KROWS = 12
N_BRANCH = 3
ROPE_BASE = 10000.0
RMS_EPS = 1e-6
GN_EPS = 1e-5
NEG_INF = -1e30

_REF_SPLIT = (S5_WIDTH, RET_QK_WIDTH, RET_WIDTH, NA_WIDTH, NA_WIDTH,
              RET_QK_WIDTH, RET_WIDTH, NA_WIDTH, N_BRANCH * D_MODEL)
_REF_NAMES = ('u', 'rk', 'rv', 'nk', 'nv', 'rq', 'rg', 'nq', 'gates')
_REF_OFF = dict(zip(_REF_NAMES, np.concatenate([[0], np.cumsum(_REF_SPLIT)[:-1]]).tolist()))
_REF_W = dict(zip(_REF_NAMES, _REF_SPLIT))
_MY_ORDER = ('gates', 'u', 'rv', 'nk', 'nv', 'rg', 'nq', 'rk', 'rq')
_MY_OFF = {}
_o = 0
for _n in _MY_ORDER:
    _MY_OFF[_n] = _o
    _o += _REF_W[_n]
N_IN = _o
_COL_PERM = np.concatenate([np.arange(_REF_OFF[n], _REF_OFF[n] + _REF_W[n]) for n in _MY_ORDER])

VMEM_LIMIT = 56 * 1024 * 1024


def _cparams(sem):
    return pltpu.CompilerParams(dimension_semantics=sem, vmem_limit_bytes=VMEM_LIMIT)


def _sigmoid(x):
    return 1.0 / (1.0 + jnp.exp(-x))


def _rms(x):
    return x * lax.rsqrt(jnp.mean(x * x, axis=-1, keepdims=True) + RMS_EPS)


def _ada_kernel(c_ref, w_ref, b_ref, o_ref):
    c = c_ref[...]
    s = c * _sigmoid(c)
    o_ref[0] = jnp.dot(s, w_ref[0], preferred_element_type=F32,
                       precision=lax.Precision.HIGHEST) + b_ref[0]


def _ada(cvec, w_ada, b_ada):
    depth, d, n = w_ada.shape
    tn = 1536
    rows = cvec.shape[0]
    return pl.pallas_call(
        _ada_kernel,
        grid=(depth, n // tn),
        in_specs=[pl.BlockSpec((rows, d), lambda l, j: (0, 0)),
                  pl.BlockSpec((1, d, tn), lambda l, j: (l, 0, j)),
                  pl.BlockSpec((1, 1, tn), lambda l, j: (l, 0, j))],
        out_specs=pl.BlockSpec((1, rows, tn), lambda l, j: (l, 0, j)),
        out_shape=jax.ShapeDtypeStruct((depth, rows, n), F32),
        compiler_params=_cparams(("parallel", "parallel")),
        name="ada_mod",
    )(cvec, w_ada, b_ada.reshape(depth, 1, n))


def _inproj_kernel(x_ref, sh_ref, sc_ref, w_ref, o_ref, h_ref):
    @pl.when(pl.program_id(1) == 0)
    def _():
        h = _rms(x_ref[...]) * (1.0 + sc_ref[0]) + sh_ref[0]
        h_ref[...] = h.astype(BF16)

    o_ref[...] = jnp.dot(h_ref[...], w_ref[...], preferred_element_type=F32).astype(o_ref.dtype)


def _inproj(x2, mod, mod_row, w, tm, tn):
    m, d = x2.shape
    n = w.shape[1]
    return pl.pallas_call(
        _inproj_kernel,
        grid=(m // tm, n // tn),
        in_specs=[pl.BlockSpec((tm, d), lambda i, j: (i, 0)),
                  pl.BlockSpec((1, 1, d), lambda i, j: (mod_row(i) * 6 + 0, 0, 0)),
                  pl.BlockSpec((1, 1, d), lambda i, j: (mod_row(i) * 6 + 1, 0, 0)),
                  pl.BlockSpec((d, tn), lambda i, j: (0, j))],
        out_specs=pl.BlockSpec((tm, tn), lambda i, j: (i, j)),
        out_shape=jax.ShapeDtypeStruct((m, n), BF16),
        scratch_shapes=[pltpu.VMEM((tm, d), BF16)],
        compiler_params=_cparams(("parallel", "arbitrary")),
        name="in_proj",
    )(x2, mod, mod, w)


def _s5_weights(lam_re, lam_im, log_dt, b_re, b_im, c_re, c_im, d_skip):
    hp = lax.Precision.HIGHEST
    t_n, g_n, p_n, h_n = S5_CHUNK, S5_GROUPS, S5_STATE, S5_GROUP
    lam_re, lam_im = lam_re.astype(F32), lam_im.astype(F32)
    dt = jnp.exp(log_dt.astype(F32))[:, :, None]
    tau = jnp.arange(t_n + 1, dtype=F32)
    mag = jnp.exp((lam_re * dt)[..., None] * tau)
    ang = (lam_im * dt)[..., None] * tau
    pr, pi = mag * jnp.cos(ang), mag * jnp.sin(ang)
    ab_re, ab_im = pr[..., 1], pi[..., 1]
    den = lam_re * lam_re + lam_im * lam_im
    f_re = ((ab_re - 1.0) * lam_re + ab_im * lam_im) / den
    f_im = (ab_im * lam_re - (ab_re - 1.0) * lam_im) / den
    b_re, b_im = b_re.astype(F32), b_im.astype(F32)
    bb_re = f_re[..., None] * b_re - f_im[..., None] * b_im
    bb_im = f_re[..., None] * b_im + f_im[..., None] * b_re
    e_re = pr[..., None] * bb_re[:, :, :, None, :] - pi[..., None] * bb_im[:, :, :, None, :]
    e_im = pr[..., None] * bb_im[:, :, :, None, :] + pi[..., None] * bb_re[:, :, :, None, :]
    c_re, c_im = c_re.astype(F32), c_im.astype(F32)
    m_tau = (jnp.einsum('dgop,dgpth->dgtoh', c_re, e_re, precision=hp)
             - jnp.einsum('dgop,dgpth->dgtoh', c_im, e_im, precision=hp))
    t_in = np.arange(t_n)[:, None]
    t_out = np.arange(t_n)[None, :]
    lag = t_out - t_in
    kf = m_tau[0][:, np.clip(lag, 0, t_n)] * jnp.asarray(lag >= 0, F32)[None, :, :, None, None]
    kb = m_tau[1][:, np.clip(-lag, 0, t_n)] * jnp.asarray(lag <= 0, F32)[None, :, :, None, None]
    kin = (kf + kb).transpose(0, 1, 4, 2, 3)
    dmat = (jnp.asarray(np.eye(t_n), F32)[None, :, None, :, None]
            * (jnp.eye(h_n, dtype=F32)[None] * d_skip.astype(F32).reshape(g_n, 1, h_n))[:, None, :, None, :])
    kin = (kin + dmat).reshape(g_n, t_n * h_n, t_n * h_n)
    idx_f = np.arange(t_n)[::-1].copy()
    idx_b = np.arange(t_n)

    def inj(e, d, idx):
        return e[d][:, :, idx, :].transpose(0, 2, 3, 1)

    vin = jnp.concatenate([inj(e_re, 0, idx_f), inj(e_re, 1, idx_b),
                           inj(e_im, 0, idx_f), inj(e_im, 1, idx_b)], axis=-1)
    vin = vin.reshape(g_n, t_n * h_n, 4 * p_n)
    pw_f = np.arange(1, t_n + 1)
    pw_b = np.arange(t_n, 0, -1)

    def rd(d, pw):
        prd, pid = pr[d][:, :, pw], pi[d][:, :, pw]
        cr, ci = c_re[d], c_im[d]
        w_re = cr[:, :, :, None] * prd[:, None] - ci[:, :, :, None] * pid[:, None]
        w_im = -(cr[:, :, :, None] * pid[:, None] + ci[:, :, :, None] * prd[:, None])
        return w_re.transpose(0, 2, 3, 1), w_im.transpose(0, 2, 3, 1)

    wf_re, wf_im = rd(0, pw_f)
    wb_re, wb_im = rd(1, pw_b)
    win = jnp.concatenate([wf_re, wb_re, wf_im, wb_im], axis=1).reshape(g_n, 4 * p_n, t_n * h_n)
    a_t = jnp.concatenate([pr[0][..., t_n], pr[1][..., t_n], pi[0][..., t_n], pi[1][..., t_n]], axis=-1)
    a_t = jnp.broadcast_to(a_t[:, None, :], (g_n, 8, 4 * p_n))
    return kin.astype(BF16), vin.astype(BF16), win.astype(BF16), a_t


def _s5_kernel(u_ref, kin_ref, vin_ref, win_ref, a_ref, y_ref, s_scr, xa_scr, xb_scr, *, ng, ncc, ncl, bsz):
    rows = (ncc + ncl) * bsz
    half = 2 * S5_STATE
    for g in range(ng):
        s_scr[g] = jnp.dot(u_ref[g], vin_ref[g], preferred_element_type=F32)
    lane = lax.broadcasted_iota(jnp.int32, (bsz, 2 * half), 1)
    fwd_lane = (lane % half) < S5_STATE
    a_re = [a_ref[g, :, :half] for g in range(ng)]
    a_im = [a_ref[g, :, half:] for g in range(ng)]

    def step(fc, bc, xs):
        rf = pl.ds(pl.multiple_of(fc * bsz, bsz), bsz)
        rb = pl.ds(pl.multiple_of(bc * bsz, bsz), bsz)
        out = []
        for g in range(ng):
            x = xs[g]
            xa_scr[g, rf, :] = x
            xb_scr[g, rb, :] = x
            s = jnp.where(fwd_lane, s_scr[g, rf, :], s_scr[g, rb, :])
            xr, xi = x[:, :half], x[:, half:]
            nr = a_re[g] * xr - a_im[g] * xi + s[:, :half]
            ni = a_re[g] * xi + a_im[g] * xr + s[:, half:]
            out.append(jnp.concatenate([nr, ni], axis=1))
        return tuple(out)

    xs = tuple(jnp.zeros((bsz, 2 * half), F32) for _ in range(ng))
    xs = lax.fori_loop(0, ncc, lambda i, c: step(i, ncc - 1 - i, c), xs)
    xs = lax.fori_loop(0, ncl, lambda i, c: step(ncc + i, ncc + ncl - 1 - i, c), xs)
    lane_r = lax.broadcasted_iota(jnp.int32, (rows, 2 * half), 1)
    fwd_r = (lane_r % half) < S5_STATE
    for g in range(ng):
        x_in = jnp.where(fwd_r, xa_scr[g], xb_scr[g]).astype(BF16)
        y = (jnp.dot(u_ref[g], kin_ref[g], preferred_element_type=F32)
             + jnp.dot(x_in, win_ref[g], preferred_element_type=F32))
        y_ref[g] = y.astype(y_ref.dtype)


def _s5(u_t, kin, vin, win, a_t, bsz, ncc, ncl, ng=4):
    g_n, rows, w = u_t.shape
    wspec = pl.BlockSpec((ng, w, w), lambda i: (i, 0, 0))
    return pl.pallas_call(
        functools.partial(_s5_kernel, ng=ng, ncc=ncc, ncl=ncl, bsz=bsz),
        grid=(g_n // ng,),
        in_specs=[pl.BlockSpec((ng, rows, w), lambda i: (i, 0, 0)), wspec, wspec, wspec,
                  pl.BlockSpec((ng, 8, w), lambda i: (i, 0, 0))],
        out_specs=pl.BlockSpec((ng, rows, w), lambda i: (i, 0, 0)),
        out_shape=jax.ShapeDtypeStruct((g_n, rows, w), BF16),
        scratch_shapes=[pltpu.VMEM((ng, rows, w), F32)] * 3,
        compiler_params=_cparams(("parallel",)),
        name="s5_mixer",
    )(u_t, kin, vin, win, a_t)


def _to_chunked(u, bsz, n):
    t = S5_CHUNK
    return (u.reshape(bsz, n // t, t, S5_GROUPS, S5_GROUP).transpose(3, 1, 0, 2, 4)
            .reshape(S5_GROUPS, (n // t) * bsz, t * S5_GROUP))


def _from_chunked(y, bsz, n):
    t = S5_CHUNK
    return (y.reshape(S5_GROUPS, n // t, bsz, t, S5_GROUP).transpose(2, 1, 3, 0, 4)
            .reshape(bsz * n, S5_WIDTH))


def _rotary_tables(seq):
    quarter = RET_DK // 4
    pos = jnp.arange(seq)
    inv_freq = ROPE_BASE ** (-jnp.arange(quarter, dtype=F32) / quarter)
    ang_r = (pos // GRID_W).astype(F32)[:, None] * inv_freq[None, :]
    ang_c = (pos % GRID_W).astype(F32)[:, None] * inv_freq[None, :]
    cos = jnp.concatenate([jnp.cos(ang_r)] * 2 + [jnp.cos(ang_c)] * 2, axis=-1)
    sin = jnp.concatenate([jnp.sin(ang_r)] * 2 + [jnp.sin(ang_c)] * 2, axis=-1)
    cos = jnp.tile(cos, (1, RET_HEADS))
    sin = jnp.tile(sin, (1, RET_HEADS))
    p = np.zeros((RET_QK_WIDTH, RET_QK_WIDTH), np.float32)
    for d in range(RET_QK_WIDTH):
        if d % (2 * quarter) < quarter:
            p[d + quarter, d] = -1.0
        else:
            p[d - quarter, d] = 1.0
    return cos, sin, jnp.asarray(p, BF16)


def _ret_kernel(lg_ref, q_ref, k_ref, v_ref, g_ref, cq_ref, ck_ref, cv_ref, cg_ref,
                cos_ref, sin_ref, p_ref, o_ref, co_ref, krot_scr, sin_scr, dm_scr, *, need_ctx, ncl):
    t = RET_T
    qkw, vw = RET_QK_WIDTH, RET_WIDTH

    def per_head(shape, axis, width, d):
        head = lax.broadcasted_iota(jnp.int32, shape, axis) // width
        out = jnp.zeros(shape, F32)
        for h in range(RET_HEADS):
            out = jnp.where(head == h, lg_ref[d, h], out)
        return out

    row = lax.broadcasted_iota(jnp.int32, (t, qkw), 0).astype(F32)
    lgf = per_head((t, qkw), 1, RET_DK, 0)
    lgb = per_head((t, qkw), 1, RET_DK, 1)
    qdec_f = jnp.exp((row + 1.0) * lgf)
    qdec_b = jnp.exp((t - row) * lgb)
    kdec_f = jnp.exp((t - 1.0 - row) * lgf)
    kdec_b = jnp.exp(row * lgb)
    cdec_f = jnp.exp(float(t) * per_head((qkw, vw), 0, RET_DK, 0))
    cdec_b = jnp.exp(float(t) * per_head((qkw, vw), 0, RET_DK, 1))
    blk = (lax.broadcasted_iota(jnp.int32, (qkw, vw), 0) // RET_DK
           == lax.broadcasted_iota(jnp.int32, (qkw, vw), 1) // RET_DV)
    head_lane = lax.broadcasted_iota(jnp.int32, (t, qkw), 1) // RET_DK

    ii = lax.broadcasted_iota(jnp.int32, (t, t), 0)
    jj = lax.broadcasted_iota(jnp.int32, (t, t), 1)
    dif = (ii - jj).astype(F32)
    for h in range(RET_HEADS):
        df = jnp.where(dif >= 0, jnp.exp(jnp.where(dif >= 0, dif, 0.0) * lg_ref[0, h]), 0.0)
        db = jnp.where(dif < 0, jnp.exp(jnp.where(dif < 0, -dif, 0.0) * lg_ref[1, h]), 0.0)
        dm_scr[h] = df + db

    def rotary(x_bf, c):
        rows = pl.ds(c * t, t)
        swapped = jnp.dot(x_bf, p_ref[...], preferred_element_type=F32)
        return x_bf.astype(F32) * cos_ref[rows, :] + swapped * sin_ref[rows, :]

    krot_scr[pl.ds(0, t), :] = ck_ref[...]
    for c in range(ncl):
        krot_scr[pl.ds((c + 1) * t, t), :] = rotary(k_ref[pl.ds(c * t, t), :], c).astype(BF16)

    def v_chunk(c):
        return cv_ref[...] if c == 0 else v_ref[pl.ds((c - 1) * t, t), :]

    def kv(c, kdec):
        kd = (krot_scr[pl.ds(c * t, t), :].astype(F32) * kdec).astype(BF16)
        return lax.dot_general(kd, v_chunk(c), (((0,), (0,)), ((), ())), preferred_element_type=F32)

    s = jnp.zeros((qkw, vw), F32)
    for c in range(ncl + 1):
        sin_scr[c, pl.ds(0, qkw), :] = jnp.where(blk, s, 0.0).astype(BF16)
        if c < ncl:
            s = cdec_f * s + kv(c, kdec_f)
    sin_scr[0, pl.ds(qkw, qkw), :] = jnp.zeros((qkw, vw), BF16)
    s = kv(0, kdec_b)
    for c in range(ncl, 0, -1):
        sin_scr[c, pl.ds(qkw, qkw), :] = jnp.where(blk, s, 0.0).astype(BF16)
        if c > 1:
            s = cdec_b * s + kv(c, kdec_b)

    for c in range(0 if need_ctx else 1, ncl + 1):
        if c == 0:
            q = cq_ref[...].astype(F32)
            gate = cg_ref[...].astype(F32)
        else:
            q = rotary(q_ref[pl.ds((c - 1) * t, t), :], c - 1)
            gate = g_ref[pl.ds((c - 1) * t, t), :].astype(F32)
        q_bf = q.astype(BF16)
        cross = (jnp.dot((q * qdec_f).astype(BF16), sin_scr[c, pl.ds(0, qkw), :], preferred_element_type=F32)
                 + jnp.dot((q * qdec_b).astype(BF16), sin_scr[c, pl.ds(qkw, qkw), :], preferred_element_type=F32))
        k_c = krot_scr[pl.ds(c * t, t), :]
        v_c = v_chunk(c)
        outs = []
        for h in range(RET_HEADS):
            qm = jnp.where(head_lane == h, q_bf, jnp.zeros_like(q_bf))
            sc = lax.dot_general(qm, k_c, (((1,), (1,)), ((), ())), preferred_element_type=F32)
            sc = (sc * dm_scr[h]).astype(BF16)
            o = (jnp.dot(sc, v_c[:, h * RET_DV:(h + 1) * RET_DV], preferred_element_type=F32)
                 + cross[:, h * RET_DV:(h + 1) * RET_DV])
            mu = jnp.mean(o, axis=-1, keepdims=True)
            var = jnp.mean(jnp.square(o - mu), axis=-1, keepdims=True)
            outs.append((o - mu) * lax.rsqrt(var + GN_EPS))
        y = gate * _sigmoid(gate) * jnp.concatenate(outs, axis=1)
        if c == 0:
            co_ref[...] = y.astype(co_ref.dtype)
        else:
            o_ref[pl.ds((c - 1) * t, t), :] = y.astype(o_ref.dtype)
    if not need_ctx:
        co_ref[...] = jnp.zeros(co_ref.shape, co_ref.dtype)


def _retention(proj_lat, proj_ctx, lg, cos, sin, pmat, bsz, seq, n_ctx, need_ctx):
    t = RET_T
    ncl = seq // t
    assert n_ctx == t
    qb, kb = _MY_OFF['rq'] // RET_QK_WIDTH, _MY_OFF['rk'] // RET_QK_WIDTH
    vb, gb = _MY_OFF['rv'] // RET_WIDTH, _MY_OFF['rg'] // RET_WIDTH

    def col(n, w, j):
        return pl.BlockSpec((n, w), lambda b, j=j: (b, j))

    const = lambda shape: pl.BlockSpec(shape, lambda b: (0,) * len(shape))
    return pl.pallas_call(
        functools.partial(_ret_kernel, need_ctx=need_ctx, ncl=ncl),
        grid=(bsz,),
        in_specs=[pl.BlockSpec(memory_space=pltpu.SMEM),
                  col(seq, RET_QK_WIDTH, qb), col(seq, RET_QK_WIDTH, kb), col(seq, RET_WIDTH, vb), col(seq, RET_WIDTH, gb),
                  col(n_ctx, RET_QK_WIDTH, qb), col(n_ctx, RET_QK_WIDTH, kb), col(n_ctx, RET_WIDTH, vb), col(n_ctx, RET_WIDTH, gb),
                  const((seq, RET_QK_WIDTH)), const((seq, RET_QK_WIDTH)), const((RET_QK_WIDTH, RET_QK_WIDTH))],
        out_specs=[pl.BlockSpec((seq, RET_WIDTH), lambda b: (b, 0)),
                   pl.BlockSpec((n_ctx, RET_WIDTH), lambda b: (b, 0))],
        out_shape=[jax.ShapeDtypeStruct((bsz * seq, RET_WIDTH), BF16),
                   jax.ShapeDtypeStruct((bsz * n_ctx, RET_WIDTH), BF16)],
        scratch_shapes=[pltpu.VMEM((seq + n_ctx, RET_QK_WIDTH), BF16),
                        pltpu.VMEM((ncl + 1, 2 * RET_QK_WIDTH, RET_WIDTH), BF16),
                        pltpu.VMEM((RET_HEADS, t, t), F32)],
        compiler_params=_cparams(("parallel",)),
        name="retention",
    )(lg, proj_lat, proj_lat, proj_lat, proj_lat, proj_ctx, proj_ctx, proj_ctx, proj_ctx, cos, sin, pmat)


def _na_block_start(kblk, rows):
    return jnp.clip(kblk * NA_QROWS - NA_ROWS // 2, 0, rows - NA_KROWS)


def _na_bias(rpb, rows):
    nq, nk = NA_QROWS * GRID_W, NA_KROWS * GRID_W
    nblk = rows // NA_QROWS
    pats = []
    for kblk in (0, 1, nblk - 1):
        r0 = kblk * NA_QROWS
        ks = int(np.clip(r0 - NA_ROWS // 2, 0, rows - NA_KROWS))
        qi, ki = np.arange(nq)[:, None], np.arange(nk)[None, :]
        qr, qc = r0 + qi // GRID_W, qi % GRID_W
        kr, kc = ks + ki // GRID_W, ki % GRID_W
        rs = np.clip(qr - NA_ROWS // 2, 0, rows - NA_ROWS)
        cs = np.clip(qc - NA_COLS // 2, 0, GRID_W - NA_COLS)
        valid = (kr >= rs) & (kr < rs + NA_ROWS) & (kc >= cs) & (kc < cs + NA_COLS)
        ri = np.clip(kr - qr + NA_ROWS - 1, 0, 2 * NA_ROWS - 2)
        ci = np.clip(kc - qc + NA_COLS - 1, 0, 2 * NA_COLS - 2)
        pats.append(jnp.where(jnp.asarray(valid)[None], rpb.astype(F32)[:, ri, ci], NEG_INF))
    return jnp.stack(pats).astype(BF16)


def _attend(q_pair, k_list, v_list, bias_list):
    lane = lax.broadcasted_iota(jnp.int32, q_pair.shape, 1) // NA_HEAD_DIM
    outs = []
    for hh in range(2):
        qm = jnp.where(lane == hh, q_pair, jnp.zeros_like(q_pair))
        ss = []
        for k_i, b_i in zip(k_list, bias_list):
            s = lax.dot_general(qm, k_i, (((1,), (1,)), ((), ())), preferred_element_type=F32)
            if b_i is not None:
                s = s + b_i[hh].astype(F32)
            ss.append(s)
        m = ss[0].max(axis=-1, keepdims=True)
        for s in ss[1:]:
            m = jnp.maximum(m, s.max(axis=-1, keepdims=True))
        ps = [jnp.exp(s - m) for s in ss]
        den = ps[0].sum(axis=-1, keepdims=True)
        for p in ps[1:]:
            den = den + p.sum(axis=-1, keepdims=True)
        acc = jnp.dot(ps[0].astype(BF16), v_list[0], preferred_element_type=F32)
        for p, v_i in zip(ps[1:], v_list[1:]):
            acc = acc + jnp.dot(p.astype(BF16), v_i, preferred_element_type=F32)
        outs.append(acc / den)
    return jnp.where(lane == 0, outs[0], outs[1])


def _na_kernel(q_ref, k_ref, v_ref, ck_ref, cv_ref, bias_ref, o_ref, *, rows):
    nk = NA_KROWS * GRID_W
    ks = pl.multiple_of(_na_block_start(pl.program_id(1), rows) * GRID_W, NA_QROWS * GRID_W)
    for hp in range(NA_HEADS // 2):
        ln = pl.ds(hp * 128, 128)
        y = _attend(q_ref[:, ln],
                    [k_ref[pl.ds(ks, nk), ln], ck_ref[:, ln]],
                    [v_ref[pl.ds(ks, nk), ln], cv_ref[:, ln]],
                    [(bias_ref[0, 2 * hp], bias_ref[0, 2 * hp + 1]), None])
        o_ref[:, ln] = y.astype(o_ref.dtype)


def _na(proj_lat, proj_ctx, bias, bsz, seq, n_ctx):
    rows = seq // GRID_W
    nq, nk = NA_QROWS * GRID_W, NA_KROWS * GRID_W
    nblk = seq // nq
    qb, kb, vb = (_MY_OFF[n] // NA_WIDTH for n in ('nq', 'nk', 'nv'))

    def pat(k):
        return jnp.where(k == 0, 0, jnp.where(k == nblk - 1, 2, 1))

    return pl.pallas_call(
        functools.partial(_na_kernel, rows=rows),
        grid=(bsz, nblk),
        in_specs=[pl.BlockSpec((nq, NA_WIDTH), lambda b, k: (b * nblk + k, qb)),
                  pl.BlockSpec((seq, NA_WIDTH), lambda b, k: (b, kb)),
                  pl.BlockSpec((seq, NA_WIDTH), lambda b, k: (b, vb)),
                  pl.BlockSpec((n_ctx, NA_WIDTH), lambda b, k: (b, kb)),
                  pl.BlockSpec((n_ctx, NA_WIDTH), lambda b, k: (b, vb)),
                  pl.BlockSpec((1, NA_HEADS, nq, nk), lambda b, k: (pat(k), 0, 0, 0))],
        out_specs=pl.BlockSpec((nq, NA_WIDTH), lambda b, k: (b * nblk + k, 0)),
        out_shape=jax.ShapeDtypeStruct((bsz * seq, NA_WIDTH), BF16),
        compiler_params=_cparams(("parallel", "arbitrary")),
        name="neighborhood_attention",
    )(proj_lat, proj_lat, proj_lat, proj_ctx, proj_ctx, bias)


def _ctx_attn_kernel(q_ref, k_ref, v_ref, o_ref):
    for hp in range(NA_HEADS // 2):
        ln = pl.ds(hp * 128, 128)
        y = _attend(q_ref[:, ln], [k_ref[:, ln]], [v_ref[:, ln]], [None])
        o_ref[:, ln] = y.astype(o_ref.dtype)


def _ctx_attn(proj_ctx, bsz, n_ctx):
    qb, kb, vb = (_MY_OFF[n] // NA_WIDTH for n in ('nq', 'nk', 'nv'))
    spec = lambda j: pl.BlockSpec((n_ctx, NA_WIDTH), lambda b: (b, j))
    return pl.pallas_call(
        _ctx_attn_kernel,
        grid=(bsz,),
        in_specs=[spec(qb), spec(kb), spec(vb)],
        out_specs=pl.BlockSpec((n_ctx, NA_WIDTH), lambda b: (b, 0)),
        out_shape=jax.ShapeDtypeStruct((bsz * n_ctx, NA_WIDTH), BF16),
        compiler_params=_cparams(("parallel",)),
        name="context_attention",
    )(proj_ctx, proj_ctx, proj_ctx)


def _gelu_tanh(x):
    return 0.5 * x * (1.0 + jnp.tanh(math.sqrt(2.0 / math.pi) * (x + 0.044715 * (x * x * x))))


def _merge_kernel(x_ref, g0_ref, g1_ref, g2_ref, ys5_ref, yret_ref, yna_ref, wglu_ref, bglu_ref,
                  wbs5_ref, wbret_ref, wbna_ref, wout_ref, gate_ref, o_ref):
    ge = _gelu_tanh(ys5_ref[...].astype(F32))
    z = jnp.dot(ge.astype(BF16), wglu_ref[...], preferred_element_type=F32) + bglu_ref[...]
    s5 = (ge * _sigmoid(z)).astype(BF16)
    m = (_sigmoid(g0_ref[...].astype(F32)) * jnp.dot(s5, wbs5_ref[...], preferred_element_type=F32)
         + _sigmoid(g1_ref[...].astype(F32)) * jnp.dot(yret_ref[...], wbret_ref[...], preferred_element_type=F32)
         + _sigmoid(g2_ref[...].astype(F32)) * jnp.dot(yna_ref[...], wbna_ref[...], preferred_element_type=F32))
    o_ref[...] = x_ref[...] + gate_ref[0] * jnp.dot(m.astype(BF16), wout_ref[...], preferred_element_type=F32)


def _merge(x2, proj, ys5, yret, yna, mod, mod_row, wglu, bglu, wbs5, wbret, wbna, wout, tm):
    m, d = x2.shape
    const = lambda a: pl.BlockSpec(a.shape, lambda i: (0,) * a.ndim)
    rowblk = lambda w, j=0: pl.BlockSpec((tm, w), lambda i, j=j: (i, j))
    return pl.pallas_call(
        _merge_kernel,
        grid=(m // tm,),
        in_specs=[rowblk(d), rowblk(d, 0), rowblk(d, 1), rowblk(d, 2),
                  rowblk(S5_WIDTH), rowblk(RET_WIDTH), rowblk(NA_WIDTH),
                  const(wglu), const(bglu), const(wbs5), const(wbret), const(wbna), const(wout),
                  pl.BlockSpec((1, 1, d), lambda i: (mod_row(i) * 6 + 2, 0, 0))],
        out_specs=rowblk(d),
        out_shape=jax.ShapeDtypeStruct((m, d), F32),
        compiler_params=_cparams(("parallel",)),
        name="merge_residual",
    )(x2, proj, proj, proj, ys5, yret, yna, wglu, bglu, wbs5, wbret, wbna, wout, mod)


def _ffn_kernel(x_ref, sh_ref, sc_ref, gate_ref, wg_ref, wu_ref, wd_ref, fn_ref, o_ref, *, final, th):
    x = x_ref[...]
    h = (_rms(x) * (1.0 + sc_ref[0]) + sh_ref[0]).astype(BF16)
    hidden = wg_ref.shape[1]
    acc = jnp.zeros(x.shape, F32)
    for j in range(hidden // th):
        a = jnp.dot(h, wg_ref[:, j * th:(j + 1) * th], preferred_element_type=F32)
        b = jnp.dot(h, wu_ref[:, j * th:(j + 1) * th], preferred_element_type=F32)
        act = (a * _sigmoid(a) * b).astype(BF16)
        acc = acc + jnp.dot(act, wd_ref[j * th:(j + 1) * th, :], preferred_element_type=F32)
    y = x + gate_ref[0] * acc
    if final:
        y = _rms(y) * fn_ref[...]
    o_ref[...] = y


def _ffn(x2, mod, mod_row, wg, wu, wd, fn, tm, final):
    m, d = x2.shape
    const = lambda a: pl.BlockSpec(a.shape, lambda i: (0,) * a.ndim)
    modspec = lambda k: pl.BlockSpec((1, 1, d), lambda i, k=k: (mod_row(i) * 6 + k, 0, 0))
    return pl.pallas_call(
        functools.partial(_ffn_kernel, final=final, th=256),
        grid=(m // tm,),
        in_specs=[pl.BlockSpec((tm, d), lambda i: (i, 0)), modspec(3), modspec(4), modspec(5),
                  const(wg), const(wu), const(wd), const(fn)],
        out_specs=pl.BlockSpec((tm, d), lambda i: (i, 0)),
        out_shape=jax.ShapeDtypeStruct((m, d), F32),
        compiler_params=_cparams(("parallel",)),
        name="swiglu_residual",
    )(x2, mod, mod, mod, wg, wu, wd, fn)


def kernel(x, c, ctx, c_ctx, w_ada, b_ada, w_in, s5_lam_re, s5_lam_im, s5_log_dt, s5_b_re, s5_b_im,
           s5_c_re, s5_c_im, s5_d, s5_w_glu, s5_b_glu, ret_theta, na_rpb, w_branch_s5, w_branch_ret,
           w_branch_na, w_out, w_ffn_gate, w_ffn_up, w_ffn_down, final_norm):
    bsz, seq, d = x.shape
    n_ctx = ctx.shape[1]
    depth = w_ada.shape[0]
    rows = seq // GRID_W
    mod_rows = 16
    ctx_row = bsz
    assert bsz + 1 <= mod_rows

    cvec = jnp.zeros((mod_rows, d), F32).at[:bsz].set(c).at[ctx_row].set(c_ctx)
    mods = _ada(cvec, w_ada, b_ada).reshape(depth, mod_rows * 6, 1, d)

    col_scale = np.ones((N_IN,), np.float32)
    col_scale[_MY_OFF['nq']:_MY_OFF['nq'] + NA_WIDTH] = NA_HEAD_DIM ** -0.5
    col_scale[_MY_OFF['rk']:_MY_OFF['rk'] + RET_QK_WIDTH] = RET_DK ** -0.5
    w_in_k = (w_in[:, :, _COL_PERM] * col_scale).astype(BF16)

    cos, sin, pmat = _rotary_tables(seq)
    log_gamma = jax.nn.log_sigmoid(ret_theta.astype(F32))

    tm = 512
    lat_row = lambda i: i // (seq // tm)
    ctx_mod_row = lambda i: ctx_row
    ub = _MY_OFF['u']
    fn2 = final_norm.reshape(1, d).astype(F32)

    x2 = x.reshape(bsz * seq, d)
    c2 = ctx.reshape(bsz * n_ctx, d)
    for l in range(depth):
        need_ctx = l < depth - 1
        mod = mods[l]
        proj_lat = _inproj(x2, mod, lat_row, w_in_k[l], tm, 1664)
        proj_ctx = _inproj(c2, mod, ctx_mod_row, w_in_k[l], tm, 1664)

        kin, vin, win, a_t = _s5_weights(s5_lam_re[l], s5_lam_im[l], s5_log_dt[l], s5_b_re[l], s5_b_im[l],
                                         s5_c_re[l], s5_c_im[l], s5_d[l])
        u_t = jnp.concatenate([_to_chunked(proj_ctx[:, ub:ub + S5_WIDTH], bsz, n_ctx),
                               _to_chunked(proj_lat[:, ub:ub + S5_WIDTH], bsz, seq)], axis=1)
        ncc, ncl = n_ctx // S5_CHUNK, seq // S5_CHUNK
        y_t = _s5(u_t, kin, vin, win, a_t, bsz, ncc, ncl)
        ys5_lat = _from_chunked(y_t[:, ncc * bsz:], bsz, seq)

        yret_lat, yret_ctx = _retention(proj_lat, proj_ctx, log_gamma[l], cos, sin, pmat, bsz, seq, n_ctx, need_ctx)
        yna_lat = _na(proj_lat, proj_ctx, _na_bias(na_rpb[l], rows), bsz, seq, n_ctx)

        wts = (s5_w_glu[l].astype(BF16), s5_b_glu[l].reshape(1, -1).astype(F32), w_branch_s5[l].astype(BF16),
               w_branch_ret[l].astype(BF16), w_branch_na[l].astype(BF16), w_out[l].astype(BF16))
        ffn_w = (w_ffn_gate[l].astype(BF16), w_ffn_up[l].astype(BF16), w_ffn_down[l].astype(BF16), fn2)
        x2 = _merge(x2, proj_lat, ys5_lat, yret_lat, yna_lat, mod, lat_row, *wts, tm)
        x2 = _ffn(x2, mod, lat_row, *ffn_w, tm, final=not need_ctx)
        if need_ctx:
            ys5_ctx = _from_chunked(y_t[:, :ncc * bsz], bsz, n_ctx)
            yna_ctx = _ctx_attn(proj_ctx, bsz, n_ctx)
            c2 = _merge(c2, proj_ctx, ys5_ctx, yret_ctx, yna_ctx, mod, ctx_mod_row, *wts, tm)
            c2 = _ffn(c2, mod, ctx_mod_row, *ffn_w, tm, final=False)
    return x2.reshape(bsz, seq, d)
```

```python
import functools
import math

import numpy as np
import jax
import jax.numpy as jnp
from jax import lax
from jax.experimental import pallas as pl
from jax.experimental.pallas import tpu as pltpu

F32 = jnp.float32
BF16 = jnp.bfloat16
HIGHEST = lax.Precision.HIGHEST

D_MODEL = 1024
GRID_W = 64
S5_WIDTH = 512
S5_GROUP = 16
S5_GROUPS = S5_WIDTH // S5_GROUP
S5_STATE = 64
S5_CHUNK = 16
RET_HEADS = 4
RET_DK = 64
RET_DV = 128
RET_QK_WIDTH = RET_HEADS * RET_DK
RET_WIDTH = RET_HEADS * RET_DV
RET_T = 256
NA_HEADS = 8
NA_HEAD_DIM = 64
NA_WIDTH = NA_HEADS * NA_HEAD_DIM
NA_ROWS = 8
NA_COLS = 16
NA_QROWS = 4
NA_KROWS = 12
N_BRANCH = 3
ROPE_BASE = 10000.0
RMS_EPS = 1e-6
GN_EPS = 1e-5
NEG_INF = -1e30
LANES = 128
SUBLANES = 8
MOD_ROWS = 16

_REF_SPLIT = (S5_WIDTH, RET_QK_WIDTH, RET_WIDTH, NA_WIDTH, NA_WIDTH,
              RET_QK_WIDTH, RET_WIDTH, NA_WIDTH, N_BRANCH * D_MODEL)
_REF_NAMES = ('u', 'rk', 'rv', 'nk', 'nv', 'rq', 'rg', 'nq', 'gates')
_REF_OFF = dict(zip(_REF_NAMES, np.concatenate([[0], np.cumsum(_REF_SPLIT)[:-1]]).tolist()))
_REF_W = dict(zip(_REF_NAMES, _REF_SPLIT))
_MY_ORDER = ('gates', 'u', 'rv', 'nk', 'nv', 'rg', 'nq', 'rk', 'rq')
_MY_OFF = {}
_o = 0
for _n in _MY_ORDER:
    _MY_OFF[_n] = _o
    _o += _REF_W[_n]
N_IN = _o
_COL_PERM = np.concatenate([np.arange(_REF_OFF[n], _REF_OFF[n] + _REF_W[n]) for n in _MY_ORDER])

VMEM_LIMIT = 56 * 1024 * 1024


def _cparams(sem):
    return pltpu.CompilerParams(dimension_semantics=sem, vmem_limit_bytes=VMEM_LIMIT)


def _sigmoid(x):
    return 1.0 / (1.0 + jnp.exp(-x))


def _rms(x):
    return x * lax.rsqrt(jnp.mean(x * x, axis=-1, keepdims=True) + RMS_EPS)


def _layer_spec(arr, l):
    nd = arr.ndim
    return pl.BlockSpec((1,) + arr.shape[1:], lambda *_: (l,) + (0,) * (nd - 1))


def _ada_kernel(c_ref, w_ref, b_ref, o_ref):
    c = c_ref[...]
    s = c * _sigmoid(c)
    o_ref[0] = jnp.dot(s, w_ref[0], preferred_element_type=F32, precision=HIGHEST) + b_ref[0]


def _ada(cvec, w_ada, b_ada):
    depth, d, n = w_ada.shape
    tn = 1536
    rows = cvec.shape[0]
    return pl.pallas_call(
        _ada_kernel,
        grid=(depth, n // tn),
        in_specs=[pl.BlockSpec((rows, d), lambda l, j: (0, 0)),
                  pl.BlockSpec((1, d, tn), lambda l, j: (l, 0, j)),
                  pl.BlockSpec((1, 1, tn), lambda l, j: (l, 0, j))],
        out_specs=pl.BlockSpec((1, rows, tn), lambda l, j: (l, 0, j)),
        out_shape=jax.ShapeDtypeStruct((depth, rows, n), F32),
        compiler_params=_cparams(("parallel", "parallel")),
        name="ada_mod",
    )(cvec, w_ada, b_ada.reshape(depth, 1, n))


def _inproj_kernel(x_ref, sh_ref, sc_ref, w_ref, o_ref, h_ref):
    @pl.when(pl.program_id(1) == 0)
    def _():
        h = _rms(x_ref[...]) * (1.0 + sc_ref[0]) + sh_ref[0]
        h_ref[...] = h.astype(BF16)

    o_ref[...] = jnp.dot(h_ref[...], w_ref[0], preferred_element_type=F32).astype(o_ref.dtype)


def _inproj(x2, mods, mod_row, w, l, tm, tn):
    m, d = x2.shape
    n = w.shape[2]
    base = l * MOD_ROWS * 6
    return pl.pallas_call(
        _inproj_kernel,
        grid=(m // tm, n // tn),
        in_specs=[pl.BlockSpec((tm, d), lambda i, j: (i, 0)),
                  pl.BlockSpec((1, 1, d), lambda i, j: (base + mod_row(i) * 6 + 0, 0, 0)),
                  pl.BlockSpec((1, 1, d), lambda i, j: (base + mod_row(i) * 6 + 1, 0, 0)),
                  pl.BlockSpec((1, d, tn), lambda i, j: (l, 0, j))],
        out_specs=pl.BlockSpec((tm, tn), lambda i, j: (i, j)),
        out_shape=jax.ShapeDtypeStruct((m, n), BF16),
        scratch_shapes=[pltpu.VMEM((tm, d), BF16)],
        compiler_params=_cparams(("parallel", "arbitrary")),
        name="in_proj",
    )(x2, mods, mods, w)


def _s5_weights(lam_re, lam_im, log_dt, b_re, b_im, c_re, c_im, d_skip):
    t_n, g_n, p_n, h_n = S5_CHUNK, S5_GROUPS, S5_STATE, S5_GROUP
    depth = lam_re.shape[0]
    lam_re, lam_im = lam_re.astype(F32), lam_im.astype(F32)
    dt = jnp.exp(log_dt.astype(F32))[..., None]
    tau = jnp.arange(t_n + 1, dtype=F32)
    mag = jnp.exp((lam_re * dt)[..., None] * tau)
    ang = (lam_im * dt)[..., None] * tau
    pr, pi = mag * jnp.cos(ang), mag * jnp.sin(ang)
    ab_re, ab_im = pr[..., 1], pi[..., 1]
    den = lam_re * lam_re + lam_im * lam_im
    f_re = ((ab_re - 1.0) * lam_re + ab_im * lam_im) / den
    f_im = (ab_im * lam_re - (ab_re - 1.0) * lam_im) / den
    b_re, b_im = b_re.astype(F32), b_im.astype(F32)
    bb_re = f_re[..., None] * b_re - f_im[..., None] * b_im
    bb_im = f_re[..., None] * b_im + f_im[..., None] * b_re
    c_re, c_im = c_re.astype(F32), c_im.astype(F32)

    e_re = pr[..., None] * bb_re[..., None, :] - pi[..., None] * bb_im[..., None, :]
    e_im = pr[..., None] * bb_im[..., None, :] + pi[..., None] * bb_re[..., None, :]
    m_tau = (jnp.einsum('ldgop,ldgpxh->ldgxoh', c_re, e_re, precision=HIGHEST)
             - jnp.einsum('ldgop,ldgpxh->ldgxoh', c_im, e_im, precision=HIGHEST))
    lag = np.arange(t_n)[None, :] - np.arange(t_n)[:, None]
    x = np.arange(t_n + 1)[:, None, None]
    oh_f = jnp.asarray(lag[None] == x, F32)
    oh_b = jnp.asarray(-lag[None] == x, F32)
    kin = (jnp.einsum('xab,lgxoh->lgahbo', oh_f, m_tau[:, 0], precision=HIGHEST)
           + jnp.einsum('xab,lgxoh->lgahbo', oh_b, m_tau[:, 1], precision=HIGHEST))
    eye_t = jnp.eye(t_n, dtype=F32)[None, None, :, None, :, None]
    eye_h = jnp.eye(h_n, dtype=F32)[None, None, None, :, None, :]
    dsk = d_skip.astype(F32).reshape(depth, g_n, 1, h_n, 1, 1)
    kin = (kin + eye_t * eye_h * dsk).reshape(depth, g_n, t_n * h_n, t_n * h_n)

    def inj(e, d, rev):
        et = e[:, d, :, :, :t_n, :]
        if rev:
            et = jnp.flip(et, axis=3)
        return et.transpose(0, 1, 3, 4, 2)

    vin = jnp.concatenate([inj(e_re, 0, True), inj(e_re, 1, False),
                           inj(e_im, 0, True), inj(e_im, 1, False)], axis=-1)
    vin = vin.reshape(depth, g_n, t_n * h_n, 4 * p_n)

    def rd(d, rev):
        prd, pid = pr[:, d, :, :, 1:], pi[:, d, :, :, 1:]
        if rev:
            prd, pid = jnp.flip(prd, axis=-1), jnp.flip(pid, axis=-1)
        cr = c_re[:, d].transpose(0, 1, 3, 2)[:, :, :, None, :]
        ci = c_im[:, d].transpose(0, 1, 3, 2)[:, :, :, None, :]
        prd, pid = prd[..., None], pid[..., None]
        return cr * prd - ci * pid, -(cr * pid + ci * prd)

    wf_re, wf_im = rd(0, False)
    wb_re, wb_im = rd(1, True)
    win = jnp.concatenate([wf_re, wb_re, wf_im, wb_im], axis=2).reshape(depth, g_n, 4 * p_n, t_n * h_n)
    a_t = jnp.concatenate([pr[:, 0, :, :, t_n], pr[:, 1, :, :, t_n],
                           pi[:, 0, :, :, t_n], pi[:, 1, :, :, t_n]], axis=-1)
    a_t = jnp.broadcast_to(a_t[:, :, None, :], (depth, g_n, SUBLANES, 4 * p_n))
    flat = lambda a: a.reshape((depth * g_n,) + a.shape[2:])
    return flat(kin.astype(BF16)), flat(vin.astype(BF16)), flat(win.astype(BF16)), flat(a_t)


def _s5_kernel(u_ref, kin_ref, vin_ref, win_ref, a_ref, y_ref, s_scr, xa_scr, xb_scr, *, ng, ncc, ncl, bsz):
    rows = (ncc + ncl) * bsz
    half = 2 * S5_STATE
    for g in range(ng):
        s_scr[g] = jnp.dot(u_ref[g], vin_ref[g], preferred_element_type=F32)
    lane = lax.broadcasted_iota(jnp.int32, (bsz, 2 * half), 1)
    fwd_lane = (lane % half) < S5_STATE
    a_re = [a_ref[g, :, :half] for g in range(ng)]
    a_im = [a_ref[g, :, half:] for g in range(ng)]

    def step(fc, bc, xs):
        rf = pl.ds(pl.multiple_of(fc * bsz, bsz), bsz)
        rb = pl.ds(pl.multiple_of(bc * bsz, bsz), bsz)
        out = []
        for g in range(ng):
            x = xs[g]
            xa_scr[g, rf, :] = x
            xb_scr[g, rb, :] = x
            s = jnp.where(fwd_lane, s_scr[g, rf, :], s_scr[g, rb, :])
            xr, xi = x[:, :half], x[:, half:]
            nr = a_re[g] * xr - a_im[g] * xi + s[:, :half]
            ni = a_re[g] * xi + a_im[g] * xr + s[:, half:]
            out.append(jnp.concatenate([nr, ni], axis=1))
        return tuple(out)

    xs = tuple(jnp.zeros((bsz, 2 * half), F32) for _ in range(ng))
    xs = lax.fori_loop(0, ncc, lambda i, c: step(i, ncc - 1 - i, c), xs)
    xs = lax.fori_loop(0, ncl, lambda i, c: step(ncc + i, ncc + ncl - 1 - i, c), xs)
    lane_r = lax.broadcasted_iota(jnp.int32, (rows, 2 * half), 1)
    fwd_r = (lane_r % half) < S5_STATE
    for g in range(ng):
        x_in = jnp.where(fwd_r, xa_scr[g], xb_scr[g]).astype(BF16)
        y = (jnp.dot(u_ref[g], kin_ref[g], preferred_element_type=F32)
             + jnp.dot(x_in, win_ref[g], preferred_element_type=F32))
        y_ref[g] = y.astype(y_ref.dtype)


def _s5(u_t, kin, vin, win, a_t, l, bsz, ncc, ncl, ng=4):
    g_n, rows, w = u_t.shape
    nblk = g_n // ng
    wspec = pl.BlockSpec((ng, w, w), lambda i: (l * nblk + i, 0, 0))
    return pl.pallas_call(
        functools.partial(_s5_kernel, ng=ng, ncc=ncc, ncl=ncl, bsz=bsz),
        grid=(nblk,),
        in_specs=[pl.BlockSpec((ng, rows, w), lambda i: (i, 0, 0)), wspec, wspec, wspec,
                  pl.BlockSpec((ng, SUBLANES, w), lambda i: (l * nblk + i, 0, 0))],
        out_specs=pl.BlockSpec((ng, rows, w), lambda i: (i, 0, 0)),
        out_shape=jax.ShapeDtypeStruct((g_n, rows, w), BF16),
        scratch_shapes=[pltpu.VMEM((ng, rows, w), F32)] * 3,
        compiler_params=_cparams(("parallel",)),
        name="s5_mixer",
    )(u_t, kin, vin, win, a_t)


_RL_TOK = S5_CHUNK * S5_CHUNK
_GPL = LANES // S5_GROUP
_NSLAB = S5_WIDTH // LANES


def _lane_block(shape):
    return lax.broadcasted_iota(jnp.int32, shape, 1) // S5_GROUP


def _to_chunked_kernel(cu_ref, lu_ref, o_ref, x_scr, y_scr, *, bsz):
    i = pl.program_id(0)

    def fill(src):
        for b in range(bsz):
            for j in range(_NSLAB):
                x_scr[b, j] = src[b, :, j * LANES:(j + 1) * LANES].astype(F32)

    @pl.when(i == 0)
    def _():
        fill(cu_ref)

    @pl.when(i > 0)
    def _():
        fill(lu_ref)

    blk = _lane_block((S5_CHUNK, LANES))

    def per_batch(b, carry):
        for j in range(_NSLAB):
            for half in range(2):
                v = [x_scr[b, j, pl.ds(half * _GPL + tl, S5_CHUNK, stride=S5_CHUNK), :] for tl in range(_GPL)]
                for q in range(_GPL):
                    out = jnp.zeros((S5_CHUNK, LANES), F32)
                    for tl in range(_GPL):
                        sh = ((tl - q) * S5_GROUP) % LANES
                        piece = v[tl] if sh == 0 else pltpu.roll(v[tl], sh, 1)
                        out = jnp.where(blk == tl, piece, out)
                    y_scr[j * _GPL + q, half, pl.ds(b, S5_CHUNK, stride=bsz), :] = out
        return carry

    lax.fori_loop(0, bsz, per_batch, 0)
    for g in range(S5_GROUPS):
        for half in range(2):
            o_ref[g, :, half * LANES:(half + 1) * LANES] = y_scr[g, half].astype(o_ref.dtype)


def _to_chunked(proj_ctx3, proj_lat3, bsz, n_ctx, seq):
    assert n_ctx == _RL_TOK and seq % _RL_TOK == 0
    nlat = seq // _RL_TOK
    ub = _MY_OFF['u'] // S5_WIDTH
    rows = S5_CHUNK * bsz
    return pl.pallas_call(
        functools.partial(_to_chunked_kernel, bsz=bsz),
        grid=(nlat + 1,),
        in_specs=[pl.BlockSpec((bsz, _RL_TOK, S5_WIDTH), lambda i: (0, 0, ub)),
                  pl.BlockSpec((bsz, _RL_TOK, S5_WIDTH), lambda i: (0, jnp.maximum(i - 1, 0), ub))],
        out_specs=pl.BlockSpec((S5_GROUPS, rows, 2 * LANES), lambda i: (0, i, 0)),
        out_shape=jax.ShapeDtypeStruct((S5_GROUPS, (nlat + 1) * rows, 2 * LANES), BF16),
        scratch_shapes=[pltpu.VMEM((bsz, _NSLAB, _RL_TOK, LANES), F32),
                        pltpu.VMEM((S5_GROUPS, 2, rows, LANES), F32)],
        compiler_params=_cparams(("arbitrary",)),
        name="s5_to_chunked",
    )(proj_ctx3, proj_lat3)


def _from_chunked_kernel(y_ref, co_ref, lo_ref, z_scr, w_scr, *, bsz):
    i = pl.program_id(0)
    for g in range(S5_GROUPS):
        for half in range(2):
            z_scr[g, half] = y_ref[g, :, half * LANES:(half + 1) * LANES].astype(F32)
    blk = _lane_block((S5_CHUNK, LANES))

    def per_batch(b, carry):
        for j in range(_NSLAB):
            for half in range(2):
                o = [z_scr[j * _GPL + q, half, pl.ds(b, S5_CHUNK, stride=bsz), :] for q in range(_GPL)]
                for tl in range(_GPL):
                    out = jnp.zeros((S5_CHUNK, LANES), F32)
                    for q in range(_GPL):
                        sh = ((q - tl) * S5_GROUP) % LANES
                        piece = o[q] if sh == 0 else pltpu.roll(o[q], sh, 1)
                        out = jnp.where(blk == q, piece, out)
                    w_scr[b, j, pl.ds(half * _GPL + tl, S5_CHUNK, stride=S5_CHUNK), :] = out
        return carry

    lax.fori_loop(0, bsz, per_batch, 0)

    def drain(dst):
        for b in range(bsz):
            for j in range(_NSLAB):
                dst[b, :, j * LANES:(j + 1) * LANES] = w_scr[b, j].astype(dst.dtype)

    @pl.when(i == 0)
    def _():
        drain(co_ref)

    @pl.when(i > 0)
    def _():
        drain(lo_ref)


def _from_chunked(y_t, bsz, n_ctx, seq):
    nlat = seq // _RL_TOK
    rows = S5_CHUNK * bsz
    return pl.pallas_call(
        functools.partial(_from_chunked_kernel, bsz=bsz),
        grid=(nlat + 1,),
        in_specs=[pl.BlockSpec((S5_GROUPS, rows, 2 * LANES), lambda i: (0, i, 0))],
        out_specs=[pl.BlockSpec((bsz, _RL_TOK, S5_WIDTH), lambda i: (0, 0, 0)),
                   pl.BlockSpec((bsz, _RL_TOK, S5_WIDTH), lambda i: (0, jnp.maximum(i - 1, 0), 0))],
        out_shape=[jax.ShapeDtypeStruct((bsz, n_ctx, S5_WIDTH), BF16),
                   jax.ShapeDtypeStruct((bsz, seq, S5_WIDTH), BF16)],
        scratch_shapes=[pltpu.VMEM((S5_GROUPS, 2, rows, LANES), F32),
                        pltpu.VMEM((bsz, _NSLAB, _RL_TOK, LANES), F32)],
        compiler_params=_cparams(("arbitrary",)),
        name="s5_from_chunked",
    )(y_t)


def _rotary_tables(seq):
    quarter = RET_DK // 4
    pos = jnp.arange(seq)
    inv_freq = ROPE_BASE ** (-jnp.arange(quarter, dtype=F32) / quarter)
    ang_r = (pos // GRID_W).astype(F32)[:, None] * inv_freq[None, :]
    ang_c = (pos % GRID_W).astype(F32)[:, None] * inv_freq[None, :]
    cos = jnp.concatenate([jnp.cos(ang_r)] * 2 + [jnp.cos(ang_c)] * 2, axis=-1)
    sin = jnp.concatenate([jnp.sin(ang_r)] * 2 + [jnp.sin(ang_c)] * 2, axis=-1)
    cos = jnp.tile(cos, (1, RET_HEADS))
    sin = jnp.tile(sin, (1, RET_HEADS))
    p = np.zeros((RET_QK_WIDTH, RET_QK_WIDTH), np.float32)
    for d in range(RET_QK_WIDTH):
        if d % (2 * quarter) < quarter:
            p[d + quarter, d] = -1.0
        else:
            p[d - quarter, d] = 1.0
    return cos, sin, jnp.asarray(p, BF16)


def _ret_kernel(lg_ref, q_ref, k_ref, v_ref, g_ref, cq_ref, ck_ref, cv_ref, cg_ref,
                cos_ref, sin_ref, p_ref, o_ref, co_ref, krot_scr, sin_scr, dm_scr, *, layer, need_ctx, ncl):
    t = RET_T
    qkw, vw = RET_QK_WIDTH, RET_WIDTH

    def per_head(shape, axis, width, d):
        head = lax.broadcasted_iota(jnp.int32, shape, axis) // width
        out = jnp.zeros(shape, F32)
        for h in range(RET_HEADS):
            out = jnp.where(head == h, lg_ref[layer, d * RET_HEADS + h], out)
        return out

    row = lax.broadcasted_iota(jnp.int32, (t, qkw), 0).astype(F32)
    lgf = per_head((t, qkw), 1, RET_DK, 0)
    lgb = per_head((t, qkw), 1, RET_DK, 1)
    qdec_f = jnp.exp((row + 1.0) * lgf)
    qdec_b = jnp.exp((t - row) * lgb)
    kdec_f = jnp.exp((t - 1.0 - row) * lgf)
    kdec_b = jnp.exp(row * lgb)
    cdec_f = jnp.exp(float(t) * per_head((qkw, vw), 0, RET_DK, 0))
    cdec_b = jnp.exp(float(t) * per_head((qkw, vw), 0, RET_DK, 1))
    blk = (lax.broadcasted_iota(jnp.int32, (qkw, vw), 0) // RET_DK
           == lax.broadcasted_iota(jnp.int32, (qkw, vw), 1) // RET_DV)
    head_lane = lax.broadcasted_iota(jnp.int32, (t, qkw), 1) // RET_DK

    ii = lax.broadcasted_iota(jnp.int32, (t, t), 0)
    jj = lax.broadcasted_iota(jnp.int32, (t, t), 1)
    dif = (ii - jj).astype(F32)
    for h in range(RET_HEADS):
        df = jnp.where(dif >= 0, jnp.exp(jnp.where(dif >= 0, dif, 0.0) * lg_ref[layer, h]), 0.0)
        db = jnp.where(dif < 0, jnp.exp(jnp.where(dif < 0, -dif, 0.0) * lg_ref[layer, RET_HEADS + h]), 0.0)
        dm_scr[h] = df + db

    def rotary(x_bf, c):
        rows = pl.ds(c * t, t)
        swapped = jnp.dot(x_bf, p_ref[...], preferred_element_type=F32)
        return x_bf.astype(F32) * cos_ref[rows, :] + swapped * sin_ref[rows, :]

    krot_scr[pl.ds(0, t), :] = ck_ref[...]
    for c in range(ncl):
        krot_scr[pl.ds((c + 1) * t, t), :] = rotary(k_ref[pl.ds(c * t, t), :], c).astype(BF16)

    def v_chunk(c):
        return cv_ref[...] if c == 0 else v_ref[pl.ds((c - 1) * t, t), :]

    def kv(c, kdec):
        kd = (krot_scr[pl.ds(c * t, t), :].astype(F32) * kdec).astype(BF16)
        return lax.dot_general(kd, v_chunk(c), (((0,), (0,)), ((), ())), preferred_element_type=F32)

    s = jnp.zeros((qkw, vw), F32)
    for c in range(ncl + 1):
        sin_scr[c, pl.ds(0, qkw), :] = jnp.where(blk, s, 0.0).astype(BF16)
        if c < ncl:
            s = cdec_f * s + kv(c, kdec_f)
    sin_scr[0, pl.ds(qkw, qkw), :] = jnp.zeros((qkw, vw), BF16)
    s = kv(0, kdec_b)
    for c in range(ncl, 0, -1):
        sin_scr[c, pl.ds(qkw, qkw), :] = jnp.where(blk, s, 0.0).astype(BF16)
        if c > 1:
            s = cdec_b * s + kv(c, kdec_b)

    for c in range(0 if need_ctx else 1, ncl + 1):
        if c == 0:
            q = cq_ref[...].astype(F32)
            gate = cg_ref[...].astype(F32)
        else:
            q = rotary(q_ref[pl.ds((c - 1) * t, t), :], c - 1)
            gate = g_ref[pl.ds((c - 1) * t, t), :].astype(F32)
        q_bf = q.astype(BF16)
        cross = (jnp.dot((q * qdec_f).astype(BF16), sin_scr[c, pl.ds(0, qkw), :], preferred_element_type=F32)
                 + jnp.dot((q * qdec_b).astype(BF16), sin_scr[c, pl.ds(qkw, qkw), :], preferred_element_type=F32))
        k_c = krot_scr[pl.ds(c * t, t), :]
        v_c = v_chunk(c)
        outs = []
        for h in range(RET_HEADS):
            qm = jnp.where(head_lane == h, q_bf, jnp.zeros_like(q_bf))
            sc = lax.dot_general(qm, k_c, (((1,), (1,)), ((), ())), preferred_element_type=F32)
            sc = (sc * dm_scr[h]).astype(BF16)
            o = (jnp.dot(sc, v_c[:, h * RET_DV:(h + 1) * RET_DV], preferred_element_type=F32)
                 + cross[:, h * RET_DV:(h + 1) * RET_DV])
            mu = jnp.mean(o, axis=-1, keepdims=True)
            var = jnp.mean(jnp.square(o - mu), axis=-1, keepdims=True)
            outs.append((o - mu) * lax.rsqrt(var + GN_EPS))
        y = gate * _sigmoid(gate) * jnp.concatenate(outs, axis=1)
        if c == 0:
            co_ref[...] = y.astype(co_ref.dtype)
        else:
            o_ref[pl.ds((c - 1) * t, t), :] = y.astype(o_ref.dtype)
    if not need_ctx:
        co_ref[...] = jnp.zeros(co_ref.shape, co_ref.dtype)


def _retention(proj_lat, proj_ctx, lg, cos, sin, pmat, l, bsz, seq, n_ctx, need_ctx):
    t = RET_T
    ncl = seq // t
    assert n_ctx == t
    qb, kb = _MY_OFF['rq'] // RET_QK_WIDTH, _MY_OFF['rk'] // RET_QK_WIDTH
    vb, gb = _MY_OFF['rv'] // RET_WIDTH, _MY_OFF['rg'] // RET_WIDTH

    def col(n, w, j):
        return pl.BlockSpec((n, w), lambda b, j=j: (b, j))

    const = lambda shape: pl.BlockSpec(shape, lambda b: (0,) * len(shape))
    return pl.pallas_call(
        functools.partial(_ret_kernel, layer=l, need_ctx=need_ctx, ncl=ncl),
        grid=(bsz,),
        in_specs=[pl.BlockSpec(memory_space=pltpu.SMEM),
                  col(seq, RET_QK_WIDTH, qb), col(seq, RET_QK_WIDTH, kb), col(seq, RET_WIDTH, vb), col(seq, RET_WIDTH, gb),
                  col(n_ctx, RET_QK_WIDTH, qb), col(n_ctx, RET_QK_WIDTH, kb), col(n_ctx, RET_WIDTH, vb), col(n_ctx, RET_WIDTH, gb),
                  const((seq, RET_QK_WIDTH)), const((seq, RET_QK_WIDTH)), const((RET_QK_WIDTH, RET_QK_WIDTH))],
        out_specs=[pl.BlockSpec((seq, RET_WIDTH), lambda b: (b, 0)),
                   pl.BlockSpec((n_ctx, RET_WIDTH), lambda b: (b, 0))],
        out_shape=[jax.ShapeDtypeStruct((bsz * seq, RET_WIDTH), BF16),
                   jax.ShapeDtypeStruct((bsz * n_ctx, RET_WIDTH), BF16)],
        scratch_shapes=[pltpu.VMEM((seq + n_ctx, RET_QK_WIDTH), BF16),
                        pltpu.VMEM((ncl + 1, 2 * RET_QK_WIDTH, RET_WIDTH), BF16),
                        pltpu.VMEM((RET_HEADS, t, t), F32)],
        compiler_params=_cparams(("parallel",)),
        name="retention",
    )(lg, proj_lat, proj_lat, proj_lat, proj_lat, proj_ctx, proj_ctx, proj_ctx, proj_ctx, cos, sin, pmat)


def _na_block_start(kblk, rows):
    return jnp.clip(kblk * NA_QROWS - NA_ROWS // 2, 0, rows - NA_KROWS)


def _na_bias(rpb, rows):
    depth = rpb.shape[0]
    nq, nk = NA_QROWS * GRID_W, NA_KROWS * GRID_W
    nblk = rows // NA_QROWS
    cols = np.arange(GRID_W)
    oh_c = (cols[None, None, :] - cols[None, :, None] + NA_COLS - 1
            == np.arange(2 * NA_COLS - 1)[:, None, None])
    cs = np.clip(cols - NA_COLS // 2, 0, GRID_W - NA_COLS)
    valid_c = (cols[None, :] >= cs[:, None]) & (cols[None, :] < cs[:, None] + NA_COLS)
    oh_r, valid = [], []
    for kblk in (0, 1, nblk - 1):
        r0 = kblk * NA_QROWS
        ks = int(np.clip(r0 - NA_ROWS // 2, 0, rows - NA_KROWS))
        qr = r0 + np.arange(NA_QROWS)[:, None]
        kr = ks + np.arange(NA_KROWS)[None, :]
        oh_r.append(kr[None] - qr[None] + NA_ROWS - 1 == np.arange(2 * NA_ROWS - 1)[:, None, None])
        rs = np.clip(qr - NA_ROWS // 2, 0, rows - NA_ROWS)
        valid_r = (kr >= rs) & (kr < rs + NA_ROWS)
        valid.append(valid_r[:, None, :, None] & valid_c[None, :, None, :])
    oh_r = jnp.asarray(np.stack(oh_r), F32)
    valid = jnp.asarray(np.stack(valid).reshape(3, nq, nk))
    t1 = jnp.einsum('lhab,bqk->lhaqk', rpb.astype(F32), jnp.asarray(oh_c, F32), precision=HIGHEST)
    bias = jnp.einsum('pars,lhaqk->lphrqsk', oh_r, t1, precision=HIGHEST)
    bias = bias.reshape(depth, 3, NA_HEADS, nq, nk)
    bias = jnp.where(valid[None, :, None], bias, NEG_INF).astype(BF16)
    return bias.reshape(depth * 3, NA_HEADS, nq, nk)


def _attend(q_pair, k_list, v_list, bias_list):
    lane = lax.broadcasted_iota(jnp.int32, q_pair.shape, 1) // NA_HEAD_DIM
    outs = []
    for hh in range(2):
        qm = jnp.where(lane == hh, q_pair, jnp.zeros_like(q_pair))
        ss = []
        for k_i, b_i in zip(k_list, bias_list):
            s = lax.dot_general(qm, k_i, (((1,), (1,)), ((), ())), preferred_element_type=F32)
            if b_i is not None:
                s = s + b_i[hh].astype(F32)
            ss.append(s)
        m = ss[0].max(axis=-1, keepdims=True)
        for s in ss[1:]:
            m = jnp.maximum(m, s.max(axis=-1, keepdims=True))
        ps = [jnp.exp(s - m) for s in ss]
        den = ps[0].sum(axis=-1, keepdims=True)
        for p in ps[1:]:
            den = den + p.sum(axis=-1, keepdims=True)
        acc = jnp.dot(ps[0].astype(BF16), v_list[0], preferred_element_type=F32)
        for p, v_i in zip(ps[1:], v_list[1:]):
            acc = acc + jnp.dot(p.astype(BF16), v_i, preferred_element_type=F32)
        outs.append(acc / den)
    return jnp.where(lane == 0, outs[0], outs[1])


def _na_kernel(q_ref, k_ref, v_ref, ck_ref, cv_ref, bias_ref, o_ref, *, rows):
    nk = NA_KROWS * GRID_W
    ks = pl.multiple_of(_na_block_start(pl.program_id(1), rows) * GRID_W, NA_QROWS * GRID_W)
    for hp in range(NA_HEADS // 2):
        ln = pl.ds(hp * LANES, LANES)
        y = _attend(q_ref[:, ln],
                    [k_ref[pl.ds(ks, nk), ln], ck_ref[:, ln]],
                    [v_ref[pl.ds(ks, nk), ln], cv_ref[:, ln]],
                    [(bias_ref[0, 2 * hp], bias_ref[0, 2 * hp + 1]), None])
        o_ref[:, ln] = y.astype(o_ref.dtype)


def _na(proj_lat, proj_ctx, bias, l, bsz, seq, n_ctx):
    rows = seq // GRID_W
    nq, nk = NA_QROWS * GRID_W, NA_KROWS * GRID_W
    nblk = seq // nq
    qb, kb, vb = (_MY_OFF[n] // NA_WIDTH for n in ('nq', 'nk', 'nv'))

    def pat(k):
        return l * 3 + jnp.where(k == 0, 0, jnp.where(k == nblk - 1, 2, 1))

    return pl.pallas_call(
        functools.partial(_na_kernel, rows=rows),
        grid=(bsz, nblk),
        in_specs=[pl.BlockSpec((nq, NA_WIDTH), lambda b, k: (b * nblk + k, qb)),
                  pl.BlockSpec((seq, NA_WIDTH), lambda b, k: (b, kb)),
                  pl.BlockSpec((seq, NA_WIDTH), lambda b, k: (b, vb)),
                  pl.BlockSpec((n_ctx, NA_WIDTH), lambda b, k: (b, kb)),
                  pl.BlockSpec((n_ctx, NA_WIDTH), lambda b, k: (b, vb)),
                  pl.BlockSpec((1, NA_HEADS, nq, nk), lambda b, k: (pat(k), 0, 0, 0))],
        out_specs=pl.BlockSpec((nq, NA_WIDTH), lambda b, k: (b * nblk + k, 0)),
        out_shape=jax.ShapeDtypeStruct((bsz * seq, NA_WIDTH), BF16),
        compiler_params=_cparams(("parallel", "arbitrary")),
        name="neighborhood_attention",
    )(proj_lat, proj_lat, proj_lat, proj_ctx, proj_ctx, bias)


def _ctx_attn_kernel(q_ref, k_ref, v_ref, o_ref):
    for hp in range(NA_HEADS // 2):
        ln = pl.ds(hp * LANES, LANES)
        y = _attend(q_ref[:, ln], [k_ref[:, ln]], [v_ref[:, ln]], [None])
        o_ref[:, ln] = y.astype(o_ref.dtype)


def _ctx_attn(proj_ctx, bsz, n_ctx):
    qb, kb, vb = (_MY_OFF[n] // NA_WIDTH for n in ('nq', 'nk', 'nv'))
    spec = lambda j: pl.BlockSpec((n_ctx, NA_WIDTH), lambda b: (b, j))
    return pl.pallas_call(
        _ctx_attn_kernel,
        grid=(bsz,),
        in_specs=[spec(qb), spec(kb), spec(vb)],
        out_specs=pl.BlockSpec((n_ctx, NA_WIDTH), lambda b: (b, 0)),
        out_shape=jax.ShapeDtypeStruct((bsz * n_ctx, NA_WIDTH), BF16),
        compiler_params=_cparams(("parallel",)),
        name="context_attention",
    )(proj_ctx, proj_ctx, proj_ctx)


def _gelu_tanh(x):
    return 0.5 * x * (1.0 + jnp.tanh(math.sqrt(2.0 / math.pi) * (x + 0.044715 * (x * x * x))))


def _merge_kernel(x_ref, g0_ref, g1_ref, g2_ref, ys5_ref, yret_ref, yna_ref, wglu_ref, bglu_ref,
                  wbs5_ref, wbret_ref, wbna_ref, wout_ref, gate_ref, o_ref):
    ge = _gelu_tanh(ys5_ref[...].astype(F32))
    z = jnp.dot(ge.astype(BF16), wglu_ref[0], preferred_element_type=F32) + bglu_ref[0]
    s5 = (ge * _sigmoid(z)).astype(BF16)
    m = (_sigmoid(g0_ref[...].astype(F32)) * jnp.dot(s5, wbs5_ref[0], preferred_element_type=F32)
         + _sigmoid(g1_ref[...].astype(F32)) * jnp.dot(yret_ref[...], wbret_ref[0], preferred_element_type=F32)
         + _sigmoid(g2_ref[...].astype(F32)) * jnp.dot(yna_ref[...], wbna_ref[0], preferred_element_type=F32))
    o_ref[...] = x_ref[...] + gate_ref[0] * jnp.dot(m.astype(BF16), wout_ref[0], preferred_element_type=F32)


def _merge(x2, proj, ys5, yret, yna, mods, mod_row, l, wts, tm):
    m, d = x2.shape
    base = l * MOD_ROWS * 6
    rowblk = lambda w, j=0: pl.BlockSpec((tm, w), lambda i, j=j: (i, j))
    return pl.pallas_call(
        _merge_kernel,
        grid=(m // tm,),
        in_specs=[rowblk(d), rowblk(d, 0), rowblk(d, 1), rowblk(d, 2),
                  rowblk(S5_WIDTH), rowblk(RET_WIDTH), rowblk(NA_WIDTH)]
                 + [_layer_spec(w, l) for w in wts]
                 + [pl.BlockSpec((1, 1, d), lambda i: (base + mod_row(i) * 6 + 2, 0, 0))],
        out_specs=rowblk(d),
        out_shape=jax.ShapeDtypeStruct((m, d), F32),
        compiler_params=_cparams(("parallel",)),
        name="merge_residual",
    )(x2, proj, proj, proj, ys5, yret, yna, *wts, mods)


def _ffn_kernel(x_ref, sh_ref, sc_ref, gate_ref, wg_ref, wu_ref, wd_ref, fn_ref, o_ref, *, final, th):
    x = x_ref[...]
    h = (_rms(x) * (1.0 + sc_ref[0]) + sh_ref[0]).astype(BF16)
    hidden = wg_ref.shape[2]
    acc = jnp.zeros(x.shape, F32)
    for j in range(hidden // th):
        a = jnp.dot(h, wg_ref[0, :, j * th:(j + 1) * th], preferred_element_type=F32)
        b = jnp.dot(h, wu_ref[0, :, j * th:(j + 1) * th], preferred_element_type=F32)
        act = (a * _sigmoid(a) * b).astype(BF16)
        acc = acc + jnp.dot(act, wd_ref[0, j * th:(j + 1) * th, :], preferred_element_type=F32)
    y = x + gate_ref[0] * acc
    if final:
        y = _rms(y) * fn_ref[...]
    o_ref[...] = y


def _ffn(x2, mods, mod_row, l, wg, wu, wd, fn, tm, final):
    m, d = x2.shape
    base = l * MOD_ROWS * 6
    modspec = lambda k: pl.BlockSpec((1, 1, d), lambda i, k=k: (base + mod_row(i) * 6 + k, 0, 0))
    return pl.pallas_call(
        functools.partial(_ffn_kernel, final=final, th=256),
        grid=(m // tm,),
        in_specs=[pl.BlockSpec((tm, d), lambda i: (i, 0)), modspec(3), modspec(4), modspec(5),
                  _layer_spec(wg, l), _layer_spec(wu, l), _layer_spec(wd, l),
                  pl.BlockSpec(fn.shape, lambda i: (0, 0))],
        out_specs=pl.BlockSpec((tm, d), lambda i: (i, 0)),
        out_shape=jax.ShapeDtypeStruct((m, d), F32),
        compiler_params=_cparams(("parallel",)),
        name="swiglu_residual",
    )(x2, mods, mods, mods, wg, wu, wd, fn)


def kernel(x, c, ctx, c_ctx, w_ada, b_ada, w_in, s5_lam_re, s5_lam_im, s5_log_dt, s5_b_re, s5_b_im,
           s5_c_re, s5_c_im, s5_d, s5_w_glu, s5_b_glu, ret_theta, na_rpb, w_branch_s5, w_branch_ret,
           w_branch_na, w_out, w_ffn_gate, w_ffn_up, w_ffn_down, final_norm):
    bsz, seq, d = x.shape
    n_ctx = ctx.shape[1]
    depth = w_ada.shape[0]
    rows = seq // GRID_W
    ctx_row = bsz
    assert bsz == SUBLANES and bsz + 1 <= MOD_ROWS

    cvec = jnp.zeros((MOD_ROWS, d), F32).at[:bsz].set(c).at[ctx_row].set(c_ctx)
    mods = _ada(cvec, w_ada, b_ada).reshape(depth * MOD_ROWS * 6, 1, d)

    col_scale = np.ones((N_IN,), np.float32)
    col_scale[_MY_OFF['nq']:_MY_OFF['nq'] + NA_WIDTH] = NA_HEAD_DIM ** -0.5
    col_scale[_MY_OFF['rk']:_MY_OFF['rk'] + RET_QK_WIDTH] = RET_DK ** -0.5
    w_in_k = (w_in[:, :, _COL_PERM] * col_scale).astype(BF16)

    cos, sin, pmat = _rotary_tables(seq)
    log_gamma = jax.nn.log_sigmoid(ret_theta.astype(F32)).reshape(depth, 2 * RET_HEADS)
    kin, vin, win, a_t = _s5_weights(s5_lam_re, s5_lam_im, s5_log_dt, s5_b_re, s5_b_im, s5_c_re, s5_c_im, s5_d)
    na_bias = _na_bias(na_rpb, rows)
    merge_w = (s5_w_glu.astype(BF16), s5_b_glu.reshape(depth, 1, -1).astype(F32), w_branch_s5.astype(BF16),
               w_branch_ret.astype(BF16), w_branch_na.astype(BF16), w_out.astype(BF16))
    ffn_w = (w_ffn_gate.astype(BF16), w_ffn_up.astype(BF16), w_ffn_down.astype(BF16),
             final_norm.reshape(1, d).astype(F32))

    tm_proj, tm = 2048, 512
    lat_row = lambda t: (lambda i: i // (seq // t))
    ctx_mod_row = lambda i: ctx_row
    ncc, ncl = n_ctx // S5_CHUNK, seq // S5_CHUNK

    x2 = x.reshape(bsz * seq, d)
    c2 = ctx.reshape(bsz * n_ctx, d)
    for l in range(depth):
        need_ctx = l < depth - 1
        proj_lat = _inproj(x2, mods, lat_row(tm_proj), w_in_k, l, tm_proj, 1664)
        proj_ctx = _inproj(c2, mods, ctx_mod_row, w_in_k, l, tm_proj, 1664)

        u_t = _to_chunked(proj_ctx.reshape(bsz, n_ctx, N_IN), proj_lat.reshape(bsz, seq, N_IN), bsz, n_ctx, seq)
        y_t = _s5(u_t, kin, vin, win, a_t, l, bsz, ncc, ncl)
        ys5_ctx, ys5_lat = _from_chunked(y_t, bsz, n_ctx, seq)

        yret_lat, yret_ctx = _retention(proj_lat, proj_ctx, log_gamma, cos, sin, pmat, l, bsz, seq, n_ctx, need_ctx)
        yna_lat = _na(proj_lat, proj_ctx, na_bias, l, bsz, seq, n_ctx)

        x2 = _merge(x2, proj_lat, ys5_lat.reshape(bsz * seq, S5_WIDTH), yret_lat, yna_lat, mods, lat_row(tm), l, merge_w, tm)
        x2 = _ffn(x2, mods, lat_row(tm), l, *ffn_w, tm, final=not need_ctx)
        if need_ctx:
            yna_ctx = _ctx_attn(proj_ctx, bsz, n_ctx)
            c2 = _merge(c2, proj_ctx, ys5_ctx.reshape(bsz * n_ctx, S5_WIDTH), yret_ctx, yna_ctx, mods, ctx_mod_row, l,
                        merge_w, tm)
            c2 = _ffn(c2, mods, ctx_mod_row, l, *ffn_w, tm, final=False)
    return x2.reshape(bsz, seq, d)
```

```python
import functools
import math

import numpy as np
import jax
import jax.numpy as jnp
from jax import lax
from jax.experimental import pallas as pl
from jax.experimental.pallas import tpu as pltpu

F32 = jnp.float32
BF16 = jnp.bfloat16
HIGHEST = lax.Precision.HIGHEST

D_MODEL = 1024
GRID_W = 64
S5_WIDTH = 512
S5_GROUP = 16
S5_GROUPS = S5_WIDTH // S5_GROUP
S5_STATE = 64
S5_CHUNK = 16
RET_HEADS = 4
RET_DK = 64
RET_DV = 128
RET_QK_WIDTH = RET_HEADS * RET_DK
RET_WIDTH = RET_HEADS * RET_DV
RET_T = 256
NA_HEADS = 8
NA_HEAD_DIM = 64
NA_WIDTH = NA_HEADS * NA_HEAD_DIM
NA_ROWS = 8
NA_COLS = 16
NA_QROWS = 4
NA_KROWS = 12
NA_NLAG = 2 * NA_ROWS
N_BRANCH = 3
ROPE_BASE = 10000.0
RMS_EPS = 1e-6
GN_EPS = 1e-5
NEG_INF = -1e30
LANES = 128
SUBLANES = 8
MOD_ROWS = 16

_REF_SPLIT = (S5_WIDTH, RET_QK_WIDTH, RET_WIDTH, NA_WIDTH, NA_WIDTH,
              RET_QK_WIDTH, RET_WIDTH, NA_WIDTH, N_BRANCH * D_MODEL)
_REF_NAMES = ('u', 'rk', 'rv', 'nk', 'nv', 'rq', 'rg', 'nq', 'gates')
_REF_OFF = dict(zip(_REF_NAMES, np.concatenate([[0], np.cumsum(_REF_SPLIT)[:-1]]).tolist()))
_REF_W = dict(zip(_REF_NAMES, _REF_SPLIT))
_MY_ORDER = ('gates', 'u', 'rv', 'nk', 'nv', 'rg', 'nq', 'rk', 'rq')
_MY_OFF = {}
_o = 0
for _n in _MY_ORDER:
    _MY_OFF[_n] = _o
    _o += _REF_W[_n]
N_IN = _o
_COL_SCALE = {'nq': NA_HEAD_DIM ** -0.5, 'rk': RET_DK ** -0.5}

VMEM_LIMIT = 56 * 1024 * 1024


def _cparams(sem):
    return pltpu.CompilerParams(dimension_semantics=sem, vmem_limit_bytes=VMEM_LIMIT)


def _sigmoid(x):
    return 1.0 / (1.0 + jnp.exp(-x))


def _rms(x):
    return x * lax.rsqrt(jnp.mean(x * x, axis=-1, keepdims=True) + RMS_EPS)


def _layer_spec(arr, l):
    nd = arr.ndim
    return pl.BlockSpec((1,) + arr.shape[1:], lambda *_: (l,) + (0,) * (nd - 1))


def _ada_kernel(c_ref, w_ref, b_ref, o_ref):
    c = c_ref[...]
    s = c * _sigmoid(c)
    o_ref[0] = jnp.dot(s, w_ref[0], preferred_element_type=F32, precision=HIGHEST) + b_ref[0]


def _ada(cvec, w_ada, b_ada):
    depth, d, n = w_ada.shape
    tn = 1536
    rows = cvec.shape[0]
    return pl.pallas_call(
        _ada_kernel,
        grid=(depth, n // tn),
        in_specs=[pl.BlockSpec((rows, d), lambda l, j: (0, 0)),
                  pl.BlockSpec((1, d, tn), lambda l, j: (l, 0, j)),
                  pl.BlockSpec((1, 1, tn), lambda l, j: (l, 0, j))],
        out_specs=pl.BlockSpec((1, rows, tn), lambda l, j: (l, 0, j)),
        out_shape=jax.ShapeDtypeStruct((depth, rows, n), F32),
        compiler_params=_cparams(("parallel", "parallel")),
        name="ada_mod",
    )(cvec, w_ada, b_ada.reshape(depth, 1, n))


def _inproj_kernel(x_ref, sh_ref, sc_ref, w_ref, o_ref, h_ref):
    @pl.when(pl.program_id(1) == 0)
    def _():
        h = _rms(x_ref[...]) * (1.0 + sc_ref[0]) + sh_ref[0]
        h_ref[...] = h.astype(BF16)

    o_ref[...] = jnp.dot(h_ref[...], w_ref[0], preferred_element_type=F32).astype(o_ref.dtype)


def _inproj(x2, mods, mod_row, w, l, tm, tn):
    m, d = x2.shape
    n = w.shape[2]
    base = l * MOD_ROWS * 6
    return pl.pallas_call(
        _inproj_kernel,
        grid=(m // tm, n // tn),
        in_specs=[pl.BlockSpec((tm, d), lambda i, j: (i, 0)),
                  pl.BlockSpec((1, 1, d), lambda i, j: (base + mod_row(i) * 6 + 0, 0, 0)),
                  pl.BlockSpec((1, 1, d), lambda i, j: (base + mod_row(i) * 6 + 1, 0, 0)),
                  pl.BlockSpec((1, d, tn), lambda i, j: (l, 0, j))],
        out_specs=pl.BlockSpec((tm, tn), lambda i, j: (i, j)),
        out_shape=jax.ShapeDtypeStruct((m, n), BF16),
        scratch_shapes=[pltpu.VMEM((tm, d), BF16)],
        compiler_params=_cparams(("parallel", "arbitrary")),
        name="in_proj",
    )(x2, mods, mods, w)


def _s5w_kernel(lam_ref, btr_ref, bti_ref, ctr_ref, cti_ref, dd_ref, kin_ref, vin_ref, win_ref, at_ref, *, ng):
    t_n, h_n = S5_CHUNK, S5_GROUP
    width = t_n * h_n
    lane = lax.broadcasted_iota(jnp.int32, (1, LANES), 1)
    f_lane = lane < S5_STATE
    tau = lax.broadcasted_iota(jnp.int32, (3 * SUBLANES, LANES), 0).astype(F32)
    lane_w = lax.broadcasted_iota(jnp.int32, (h_n, width), 1)
    for g in range(ng):
        lam_re, lam_im, dt = lam_ref[g, 0:1, :], lam_ref[g, 1:2, :], lam_ref[g, 2:3, :]
        mag = jnp.exp(tau * (lam_re * dt))
        ang = tau * (lam_im * dt)
        pr, pi = mag * jnp.cos(ang), mag * jnp.sin(ang)
        ab_re, ab_im = pr[1:2], pi[1:2]
        den = lam_re * lam_re + lam_im * lam_im
        f_re = ((ab_re - 1.0) * lam_re + ab_im * lam_im) / den
        f_im = (ab_im * lam_re - (ab_re - 1.0) * lam_im) / den
        btr, bti = btr_ref[g], bti_ref[g]
        bbr = f_re * btr - f_im * bti
        bbi = f_re * bti + f_im * btr
        ctr, cti = ctr_ref[g], cti_ref[g]

        def powers(pf, pb):
            rr = [jnp.broadcast_to(jnp.where(f_lane, pr[pf[t]:pf[t] + 1], pr[pb[t]:pb[t] + 1]), (h_n, LANES))
                  for t in range(t_n)]
            ri = [jnp.broadcast_to(jnp.where(f_lane, pi[pf[t]:pf[t] + 1], pi[pb[t]:pb[t] + 1]), (h_n, LANES))
                  for t in range(t_n)]
            return jnp.concatenate(rr, axis=0), jnp.concatenate(ri, axis=0)

        tile = lambda a: jnp.concatenate([a] * t_n, axis=0)
        bbr_t, bbi_t, ctr_t, cti_t = tile(bbr), tile(bbi), tile(ctr), tile(cti)

        xr, xi = powers([t_n - 1 - t for t in range(t_n)], list(range(t_n)))
        vin = jnp.concatenate([xr * bbr_t - xi * bbi_t, xr * bbi_t + xi * bbr_t], axis=1)
        vin_ref[g] = vin.astype(vin_ref.dtype)

        yr, yi = powers([t + 1 for t in range(t_n)], [t_n - t for t in range(t_n)])
        win_t = jnp.concatenate([ctr_t * yr - cti_t * yi, -(ctr_t * yi + cti_t * yr)], axis=1)
        win_ref[g] = win_t.T.astype(win_ref.dtype)

        zr, zi = powers(list(range(t_n)), [t_n - 1 - t for t in range(t_n)])
        fmat = jnp.concatenate([ctr_t * zr - cti_t * zi, ctr_t * zi + cti_t * zr], axis=1)
        dn = (((1,), (1,)), ((), ()))
        lhs_f = jnp.concatenate([jnp.where(f_lane, bbr, 0.0), jnp.where(f_lane, -bbi, 0.0)], axis=1)
        lhs_b = jnp.concatenate([jnp.where(f_lane, 0.0, bbr), jnp.where(f_lane, 0.0, -bbi)], axis=1)
        w_f = lax.dot_general(lhs_f, fmat, dn, preferred_element_type=F32, precision=HIGHEST)
        w_b = lax.dot_general(lhs_b, fmat, dn, preferred_element_type=F32, precision=HIGHEST)
        w_f = w_f + jnp.concatenate([dd_ref[g], jnp.zeros((h_n, width - LANES), F32)], axis=1)
        blocks = []
        for t in range(t_n):
            sh_f = t * h_n
            sh_b = (width - (t_n - 1 - t) * h_n) % width
            fw = w_f if sh_f == 0 else pltpu.roll(w_f, sh_f, 1)
            bw = w_b if sh_b == 0 else pltpu.roll(w_b, sh_b, 1)
            blocks.append(jnp.where(lane_w >= t * h_n, fw, 0.0) + jnp.where(lane_w < (t + 1) * h_n, bw, 0.0))
        kin_ref[g] = jnp.concatenate(blocks, axis=0).astype(kin_ref.dtype)
        at_ref[g] = jnp.broadcast_to(jnp.concatenate([pr[t_n:t_n + 1], pi[t_n:t_n + 1]], axis=1),
                                     (SUBLANES, 2 * LANES))


def _s5_weights(lam_re, lam_im, log_dt, b_re, b_im, c_re, c_im, d_skip, ng=4):
    depth, _, g_n, p_n = lam_re.shape
    h_n = S5_GROUP
    n = depth * g_n
    width = S5_CHUNK * h_n
    pair = lambda a: a.astype(F32).transpose(0, 2, 1, 3).reshape(n, 1, 2 * p_n)
    dt = jnp.broadcast_to(jnp.exp(log_dt.astype(F32))[..., None], lam_re.shape)
    lam = jnp.concatenate([pair(lam_re), pair(lam_im), pair(dt), jnp.zeros((n, SUBLANES - 3, 2 * p_n), F32)], axis=1)
    bt = lambda a: a.astype(F32).transpose(0, 2, 4, 1, 3).reshape(n, h_n, 2 * p_n)
    ct = lambda a: a.astype(F32).transpose(0, 2, 3, 1, 4).reshape(n, h_n, 2 * p_n)
    dd = jnp.eye(h_n, LANES, dtype=F32)[None] * d_skip.astype(F32).reshape(n, h_n, 1)
    small = pl.BlockSpec((ng, h_n, LANES), lambda i: (i, 0, 0))
    big = pl.BlockSpec((ng, width, width), lambda i: (i, 0, 0))
    return pl.pallas_call(
        functools.partial(_s5w_kernel, ng=ng),
        grid=(n // ng,),
        in_specs=[pl.BlockSpec((ng, SUBLANES, LANES), lambda i: (i, 0, 0)), small, small, small, small, small],
        out_specs=[big, big, big, pl.BlockSpec((ng, SUBLANES, width), lambda i: (i, 0, 0))],
        out_shape=[jax.ShapeDtypeStruct((n, width, width), BF16)] * 3
                  + [jax.ShapeDtypeStruct((n, SUBLANES, width), F32)],
        compiler_params=_cparams(("parallel",)),
        name="s5_weights",
    )(lam, bt(b_re), bt(b_im), ct(c_re), ct(c_im), dd)


def _s5_kernel(u_ref, kin_ref, vin_ref, win_ref, a_ref, y_ref, s_scr, xa_scr, xb_scr, *, ng, ncc, ncl, bsz):
    rows = (ncc + ncl) * bsz
    half = 2 * S5_STATE
    for g in range(ng):
        s_scr[g] = jnp.dot(u_ref[g], vin_ref[g], preferred_element_type=F32)
    lane = lax.broadcasted_iota(jnp.int32, (bsz, 2 * half), 1)
    fwd_lane = (lane % half) < S5_STATE
    a_re = [a_ref[g, :, :half] for g in range(ng)]
    a_im = [a_ref[g, :, half:] for g in range(ng)]

    def step(fc, bc, xs):
        rf = pl.ds(pl.multiple_of(fc * bsz, bsz), bsz)
        rb = pl.ds(pl.multiple_of(bc * bsz, bsz), bsz)
        out = []
        for g in range(ng):
            x = xs[g]
            xa_scr[g, rf, :] = x
            xb_scr[g, rb, :] = x
            s = jnp.where(fwd_lane, s_scr[g, rf, :], s_scr[g, rb, :])
            xr, xi = x[:, :half], x[:, half:]
            nr = a_re[g] * xr - a_im[g] * xi + s[:, :half]
            ni = a_re[g] * xi + a_im[g] * xr + s[:, half:]
            out.append(jnp.concatenate([nr, ni], axis=1))
        return tuple(out)

    xs = tuple(jnp.zeros((bsz, 2 * half), F32) for _ in range(ng))
    xs = lax.fori_loop(0, ncc, lambda i, c: step(i, ncc - 1 - i, c), xs)
    xs = lax.fori_loop(0, ncl, lambda i, c: step(ncc + i, ncc + ncl - 1 - i, c), xs)
    lane_r = lax.broadcasted_iota(jnp.int32, (rows, 2 * half), 1)
    fwd_r = (lane_r % half) < S5_STATE
    for g in range(ng):
        x_in = jnp.where(fwd_r, xa_scr[g], xb_scr[g]).astype(BF16)
        y = (jnp.dot(u_ref[g], kin_ref[g], preferred_element_type=F32)
             + jnp.dot(x_in, win_ref[g], preferred_element_type=F32))
        y_ref[g] = y.astype(y_ref.dtype)


def _s5(u_t, kin, vin, win, a_t, l, bsz, ncc, ncl, ng=4):
    g_n, rows, w = u_t.shape
    nblk = g_n // ng
    wspec = pl.BlockSpec((ng, w, w), lambda i: (l * nblk + i, 0, 0))
    return pl.pallas_call(
        functools.partial(_s5_kernel, ng=ng, ncc=ncc, ncl=ncl, bsz=bsz),
        grid=(nblk,),
        in_specs=[pl.BlockSpec((ng, rows, w), lambda i: (i, 0, 0)), wspec, wspec, wspec,
                  pl.BlockSpec((ng, SUBLANES, w), lambda i: (l * nblk + i, 0, 0))],
        out_specs=pl.BlockSpec((ng, rows, w), lambda i: (i, 0, 0)),
        out_shape=jax.ShapeDtypeStruct((g_n, rows, w), BF16),
        scratch_shapes=[pltpu.VMEM((ng, rows, w), F32)] * 3,
        compiler_params=_cparams(("parallel",)),
        name="s5_mixer",
    )(u_t, kin, vin, win, a_t)


_RL_TOK = S5_CHUNK * S5_CHUNK
_GPL = LANES // S5_GROUP
_NSLAB = S5_WIDTH // LANES


def _block_transpose(tiles):
    blk = lax.broadcasted_iota(jnp.int32, tiles[0].shape, 1) // S5_GROUP
    tiles = list(tiles)
    s = _GPL // 2
    while s >= 1:
        hi = (blk & s) != 0
        for a in range(_GPL):
            if a & s:
                continue
            b = a + s
            ta, tb = tiles[a], tiles[b]
            tiles[a] = jnp.where(hi, pltpu.roll(tb, s * S5_GROUP, 1), ta)
            tiles[b] = jnp.where(hi, tb, pltpu.roll(ta, LANES - s * S5_GROUP, 1))
        s //= 2
    return tiles


def _to_chunked_kernel(cu_ref, lu_ref, o_ref, x_scr, y_scr, *, bsz):
    i = pl.program_id(0)

    def fill(src):
        for b in range(bsz):
            for j in range(_NSLAB):
                x_scr[b, j] = src[b, :, j * LANES:(j + 1) * LANES].astype(F32)

    @pl.when(i == 0)
    def _():
        fill(cu_ref)

    @pl.when(i > 0)
    def _():
        fill(lu_ref)

    def per_batch(b, carry):
        for j in range(_NSLAB):
            for half in range(2):
                v = [x_scr[b, j, pl.ds(half * _GPL + tl, S5_CHUNK, stride=S5_CHUNK), :] for tl in range(_GPL)]
                for q, out in enumerate(_block_transpose(v)):
                    y_scr[j * _GPL + q, half, pl.ds(b, S5_CHUNK, stride=bsz), :] = out
        return carry

    lax.fori_loop(0, bsz, per_batch, 0)
    for g in range(S5_GROUPS):
        for half in range(2):
            o_ref[g, :, half * LANES:(half + 1) * LANES] = y_scr[g, half].astype(o_ref.dtype)


def _to_chunked(proj_ctx3, proj_lat3, bsz, n_ctx, seq):
    assert n_ctx == _RL_TOK and seq % _RL_TOK == 0
    nlat = seq // _RL_TOK
    ub = _MY_OFF['u'] // S5_WIDTH
    rows = S5_CHUNK * bsz
    return pl.pallas_call(
        functools.partial(_to_chunked_kernel, bsz=bsz),
        grid=(nlat + 1,),
        in_specs=[pl.BlockSpec((bsz, _RL_TOK, S5_WIDTH), lambda i: (0, 0, ub)),
                  pl.BlockSpec((bsz, _RL_TOK, S5_WIDTH), lambda i: (0, jnp.maximum(i - 1, 0), ub))],
        out_specs=pl.BlockSpec((S5_GROUPS, rows, 2 * LANES), lambda i: (0, i, 0)),
        out_shape=jax.ShapeDtypeStruct((S5_GROUPS, (nlat + 1) * rows, 2 * LANES), BF16),
        scratch_shapes=[pltpu.VMEM((bsz, _NSLAB, _RL_TOK, LANES), F32),
                        pltpu.VMEM((S5_GROUPS, 2, rows, LANES), F32)],
        compiler_params=_cparams(("arbitrary",)),
        name="s5_to_chunked",
    )(proj_ctx3, proj_lat3)


def _from_chunked_kernel(y_ref, co_ref, lo_ref, z_scr, w_scr, *, bsz):
    i = pl.program_id(0)
    for g in range(S5_GROUPS):
        for half in range(2):
            z_scr[g, half] = y_ref[g, :, half * LANES:(half + 1) * LANES].astype(F32)

    def per_batch(b, carry):
        for j in range(_NSLAB):
            for half in range(2):
                o = [z_scr[j * _GPL + q, half, pl.ds(b, S5_CHUNK, stride=bsz), :] for q in range(_GPL)]
                for tl, out in enumerate(_block_transpose(o)):
                    w_scr[b, j, pl.ds(half * _GPL + tl, S5_CHUNK, stride=S5_CHUNK), :] = out
        return carry

    lax.fori_loop(0, bsz, per_batch, 0)

    def drain(dst):
        for b in range(bsz):
            for j in range(_NSLAB):
                dst[b, :, j * LANES:(j + 1) * LANES] = w_scr[b, j].astype(dst.dtype)

    @pl.when(i == 0)
    def _():
        drain(co_ref)

    @pl.when(i > 0)
    def _():
        drain(lo_ref)


def _from_chunked(y_t, bsz, n_ctx, seq):
    nlat = seq // _RL_TOK
    rows = S5_CHUNK * bsz
    return pl.pallas_call(
        functools.partial(_from_chunked_kernel, bsz=bsz),
        grid=(nlat + 1,),
        in_specs=[pl.BlockSpec((S5_GROUPS, rows, 2 * LANES), lambda i: (0, i, 0))],
        out_specs=[pl.BlockSpec((bsz, _RL_TOK, S5_WIDTH), lambda i: (0, 0, 0)),
                   pl.BlockSpec((bsz, _RL_TOK, S5_WIDTH), lambda i: (0, jnp.maximum(i - 1, 0), 0))],
        out_shape=[jax.ShapeDtypeStruct((bsz, n_ctx, S5_WIDTH), BF16),
                   jax.ShapeDtypeStruct((bsz, seq, S5_WIDTH), BF16)],
        scratch_shapes=[pltpu.VMEM((S5_GROUPS, 2, rows, LANES), F32),
                        pltpu.VMEM((bsz, _NSLAB, _RL_TOK, LANES), F32)],
        compiler_params=_cparams(("arbitrary",)),
        name="s5_from_chunked",
    )(y_t)


def _rotary_tables(seq):
    quarter = RET_DK // 4
    pos = jnp.arange(seq)
    inv_freq = ROPE_BASE ** (-jnp.arange(quarter, dtype=F32) / quarter)
    ang_r = (pos // GRID_W).astype(F32)[:, None] * inv_freq[None, :]
    ang_c = (pos % GRID_W).astype(F32)[:, None] * inv_freq[None, :]
    cos = jnp.concatenate([jnp.cos(ang_r)] * 2 + [jnp.cos(ang_c)] * 2, axis=-1)
    sin = jnp.concatenate([jnp.sin(ang_r)] * 2 + [jnp.sin(ang_c)] * 2, axis=-1)
    cos = jnp.tile(cos, (1, RET_HEADS))
    sin = jnp.tile(sin, (1, RET_HEADS))
    p = np.zeros((RET_QK_WIDTH, RET_QK_WIDTH), np.float32)
    for d in range(RET_QK_WIDTH):
        if d % (2 * quarter) < quarter:
            p[d + quarter, d] = -1.0
        else:
            p[d - quarter, d] = 1.0
    return cos, sin, jnp.asarray(p, BF16)


def _ret_kernel(lg_ref, q_ref, k_ref, v_ref, g_ref, cq_ref, ck_ref, cv_ref, cg_ref,
                cos_ref, sin_ref, p_ref, o_ref, co_ref, krot_scr, sin_scr, dm_scr, *, layer, need_ctx, ncl):
    t = RET_T
    qkw, vw = RET_QK_WIDTH, RET_WIDTH

    def per_head(shape, axis, width, d):
        head = lax.broadcasted_iota(jnp.int32, shape, axis) // width
        out = jnp.zeros(shape, F32)
        for h in range(RET_HEADS):
            out = jnp.where(head == h, lg_ref[layer, d * RET_HEADS + h], out)
        return out

    row = lax.broadcasted_iota(jnp.int32, (t, qkw), 0).astype(F32)
    lgf = per_head((t, qkw), 1, RET_DK, 0)
    lgb = per_head((t, qkw), 1, RET_DK, 1)
    qdec_f = jnp.exp((row + 1.0) * lgf)
    qdec_b = jnp.exp((t - row) * lgb)
    kdec_f = jnp.exp((t - 1.0 - row) * lgf)
    kdec_b = jnp.exp(row * lgb)
    cdec_f = jnp.exp(float(t) * per_head((qkw, vw), 0, RET_DK, 0))
    cdec_b = jnp.exp(float(t) * per_head((qkw, vw), 0, RET_DK, 1))
    blk = (lax.broadcasted_iota(jnp.int32, (qkw, vw), 0) // RET_DK
           == lax.broadcasted_iota(jnp.int32, (qkw, vw), 1) // RET_DV)
    head_lane = lax.broadcasted_iota(jnp.int32, (t, qkw), 1) // RET_DK

    ii = lax.broadcasted_iota(jnp.int32, (t, t), 0)
    jj = lax.broadcasted_iota(jnp.int32, (t, t), 1)
    dif = (ii - jj).astype(F32)
    for h in range(RET_HEADS):
        df = jnp.where(dif >= 0, jnp.exp(jnp.where(dif >= 0, dif, 0.0) * lg_ref[layer, h]), 0.0)
        db = jnp.where(dif < 0, jnp.exp(jnp.where(dif < 0, -dif, 0.0) * lg_ref[layer, RET_HEADS + h]), 0.0)
        dm_scr[h] = df + db

    def rotary(x_bf, c):
        rows = pl.ds(c * t, t)
        swapped = jnp.dot(x_bf, p_ref[...], preferred_element_type=F32)
        return x_bf.astype(F32) * cos_ref[rows, :] + swapped * sin_ref[rows, :]

    krot_scr[pl.ds(0, t), :] = ck_ref[...]
    for c in range(ncl):
        krot_scr[pl.ds((c + 1) * t, t), :] = rotary(k_ref[pl.ds(c * t, t), :], c).astype(BF16)

    def v_chunk(c):
        return cv_ref[...] if c == 0 else v_ref[pl.ds((c - 1) * t, t), :]

    def kv(c, kdec):
        kd = (krot_scr[pl.ds(c * t, t), :].astype(F32) * kdec).astype(BF16)
        return lax.dot_general(kd, v_chunk(c), (((0,), (0,)), ((), ())), preferred_element_type=F32)

    s = jnp.zeros((qkw, vw), F32)
    for c in range(ncl + 1):
        sin_scr[c, pl.ds(0, qkw), :] = jnp.where(blk, s, 0.0).astype(BF16)
        if c < ncl:
            s = cdec_f * s + kv(c, kdec_f)
    sin_scr[0, pl.ds(qkw, qkw), :] = jnp.zeros((qkw, vw), BF16)
    s = kv(0, kdec_b)
    for c in range(ncl, 0, -1):
        sin_scr[c, pl.ds(qkw, qkw), :] = jnp.where(blk, s, 0.0).astype(BF16)
        if c > 1:
            s = cdec_b * s + kv(c, kdec_b)

    for c in range(0 if need_ctx else 1, ncl + 1):
        if c == 0:
            q = cq_ref[...].astype(F32)
            gate = cg_ref[...].astype(F32)
        else:
            q = rotary(q_ref[pl.ds((c - 1) * t, t), :], c - 1)
            gate = g_ref[pl.ds((c - 1) * t, t), :].astype(F32)
        q_bf = q.astype(BF16)
        cross = (jnp.dot((q * qdec_f).astype(BF16), sin_scr[c, pl.ds(0, qkw), :], preferred_element_type=F32)
                 + jnp.dot((q * qdec_b).astype(BF16), sin_scr[c, pl.ds(qkw, qkw), :], preferred_element_type=F32))
        k_c = krot_scr[pl.ds(c * t, t), :]
        v_c = v_chunk(c)
        outs = []
        for h in range(RET_HEADS):
            qm = jnp.where(head_lane == h, q_bf, jnp.zeros_like(q_bf))
            sc = lax.dot_general(qm, k_c, (((1,), (1,)), ((), ())), preferred_element_type=F32)
            sc = (sc * dm_scr[h]).astype(BF16)
            o = (jnp.dot(sc, v_c[:, h * RET_DV:(h + 1) * RET_DV], preferred_element_type=F32)
                 + cross[:, h * RET_DV:(h + 1) * RET_DV])
            mu = jnp.mean(o, axis=-1, keepdims=True)
            var = jnp.mean(jnp.square(o - mu), axis=-1, keepdims=True)
            outs.append((o - mu) * lax.rsqrt(var + GN_EPS))
        y = gate * _sigmoid(gate) * jnp.concatenate(outs, axis=1)
        if c == 0:
            co_ref[...] = y.astype(co_ref.dtype)
        else:
            o_ref[pl.ds((c - 1) * t, t), :] = y.astype(o_ref.dtype)
    if not need_ctx:
        co_ref[...] = jnp.zeros(co_ref.shape, co_ref.dtype)


def _retention(proj_lat, proj_ctx, lg, cos, sin, pmat, l, bsz, seq, n_ctx, need_ctx):
    t = RET_T
    ncl = seq // t
    assert n_ctx == t
    qb, kb = _MY_OFF['rq'] // RET_QK_WIDTH, _MY_OFF['rk'] // RET_QK_WIDTH
    vb, gb = _MY_OFF['rv'] // RET_WIDTH, _MY_OFF['rg'] // RET_WIDTH

    def col(n, w, j):
        return pl.BlockSpec((n, w), lambda b, j=j: (b, j))

    const = lambda shape: pl.BlockSpec(shape, lambda b: (0,) * len(shape))
    return pl.pallas_call(
        functools.partial(_ret_kernel, layer=l, need_ctx=need_ctx, ncl=ncl),
        grid=(bsz,),
        in_specs=[pl.BlockSpec(memory_space=pltpu.SMEM),
                  col(seq, RET_QK_WIDTH, qb), col(seq, RET_QK_WIDTH, kb), col(seq, RET_WIDTH, vb), col(seq, RET_WIDTH, gb),
                  col(n_ctx, RET_QK_WIDTH, qb), col(n_ctx, RET_QK_WIDTH, kb), col(n_ctx, RET_WIDTH, vb), col(n_ctx, RET_WIDTH, gb),
                  const((seq, RET_QK_WIDTH)), const((seq, RET_QK_WIDTH)), const((RET_QK_WIDTH, RET_QK_WIDTH))],
        out_specs=[pl.BlockSpec((seq, RET_WIDTH), lambda b: (b, 0)),
                   pl.BlockSpec((n_ctx, RET_WIDTH), lambda b: (b, 0))],
        out_shape=[jax.ShapeDtypeStruct((bsz * seq, RET_WIDTH), BF16),
                   jax.ShapeDtypeStruct((bsz * n_ctx, RET_WIDTH), BF16)],
        scratch_shapes=[pltpu.VMEM((seq + n_ctx, RET_QK_WIDTH), BF16),
                        pltpu.VMEM((ncl + 1, 2 * RET_QK_WIDTH, RET_WIDTH), BF16),
                        pltpu.VMEM((RET_HEADS, t, t), F32)],
        compiler_params=_cparams(("parallel",)),
        name="retention",
    )(lg, proj_lat, proj_lat, proj_lat, proj_lat, proj_ctx, proj_ctx, proj_ctx, proj_ctx, cos, sin, pmat)


def _na_block_start(kblk, rows):
    return jnp.clip(kblk * NA_QROWS - NA_ROWS // 2, 0, rows - NA_KROWS)


def _na_bias_table(rpb):
    depth, heads, nlag, ncol = rpb.shape
    cols = np.arange(GRID_W)
    cs = np.clip(cols - NA_COLS // 2, 0, GRID_W - NA_COLS)
    valid_c = (cols[None, :] >= cs[:, None]) & (cols[None, :] < cs[:, None] + NA_COLS)
    pad = GRID_W - 1
    padded = jnp.pad(rpb.astype(F32), ((0, 0), (0, 0), (0, 0), (pad, pad)))
    toe = jnp.stack([padded[..., pad + NA_COLS - 1 - qc: pad + NA_COLS - 1 - qc + GRID_W] for qc in range(GRID_W)],
                    axis=-2)
    toe = jnp.where(jnp.asarray(valid_c), toe, NEG_INF)
    toe = jnp.pad(toe, ((0, 0), (0, 0), (1, NA_NLAG + 1 - nlag - 1), (0, 0), (0, 0)))
    table = jnp.concatenate([toe[:, :, :NA_NLAG], toe[:, :, 1:NA_NLAG + 1]], axis=-1)
    return table.reshape(depth, heads * NA_NLAG, GRID_W, 2 * GRID_W).astype(BF16)


def _attend(q_pair, k_list, v_list, bias_list):
    lane = lax.broadcasted_iota(jnp.int32, q_pair.shape, 1) // NA_HEAD_DIM
    outs = []
    for hh in range(2):
        qm = jnp.where(lane == hh, q_pair, jnp.zeros_like(q_pair))
        ss = []
        for k_i, b_i in zip(k_list, bias_list):
            s = lax.dot_general(qm, k_i, (((1,), (1,)), ((), ())), preferred_element_type=F32)
            if b_i is not None:
                s = s + b_i(hh)
            ss.append(s)
        m = ss[0].max(axis=-1, keepdims=True)
        for s in ss[1:]:
            m = jnp.maximum(m, s.max(axis=-1, keepdims=True))
        ps = [jnp.exp(s - m) for s in ss]
        den = ps[0].sum(axis=-1, keepdims=True)
        for p in ps[1:]:
            den = den + p.sum(axis=-1, keepdims=True)
        acc = jnp.dot(ps[0].astype(BF16), v_list[0], preferred_element_type=F32)
        for p, v_i in zip(ps[1:], v_list[1:]):
            acc = acc + jnp.dot(p.astype(BF16), v_i, preferred_element_type=F32)
        outs.append(acc / den)
    return jnp.where(lane == 0, outs[0], outs[1])


def _na_kernel(q_ref, k_ref, v_ref, ck_ref, cv_ref, tab_ref, o_ref, *, rows):
    nq, nk = NA_QROWS * GRID_W, NA_KROWS * GRID_W
    kblk = pl.program_id(1)
    r0 = kblk * NA_QROWS
    ks_row = _na_block_start(kblk, rows)
    ks = pl.multiple_of(ks_row * GRID_W, NA_QROWS * GRID_W)
    qr = r0 + lax.broadcasted_iota(jnp.int32, (nq, nk), 0) // GRID_W
    kr = ks_row + lax.broadcasted_iota(jnp.int32, (nq, nk), 1) // GRID_W
    rs = jnp.clip(qr - NA_ROWS // 2, 0, rows - NA_ROWS)
    row_mask = jnp.where((kr >= rs) & (kr < rs + NA_ROWS), 0.0, NEG_INF)

    def bias(h):
        row_blocks = []
        for qrl in range(NA_QROWS):
            tiles = []
            for kp in range(NA_KROWS // 2):
                lag = ks_row - r0 + 2 * kp - qrl + NA_ROWS - 1
                idx = h * NA_NLAG + jnp.clip(lag, -1, NA_NLAG - 2) + 1
                tiles.append(tab_ref[0, idx].astype(F32))
            row_blocks.append(jnp.concatenate(tiles, axis=1))
        return jnp.concatenate(row_blocks, axis=0) + row_mask

    for hp in range(NA_HEADS // 2):
        ln = pl.ds(hp * LANES, LANES)
        y = _attend(q_ref[:, ln],
                    [k_ref[pl.ds(ks, nk), ln], ck_ref[:, ln]],
                    [v_ref[pl.ds(ks, nk), ln], cv_ref[:, ln]],
                    [lambda hh, hp=hp: bias(2 * hp + hh), None])
        o_ref[:, ln] = y.astype(o_ref.dtype)


def _na(proj_lat, proj_ctx, table, l, bsz, seq, n_ctx):
    rows = seq // GRID_W
    nq = NA_QROWS * GRID_W
    nblk = seq // nq
    qb, kb, vb = (_MY_OFF[n] // NA_WIDTH for n in ('nq', 'nk', 'nv'))
    return pl.pallas_call(
        functools.partial(_na_kernel, rows=rows),
        grid=(bsz, nblk),
        in_specs=[pl.BlockSpec((nq, NA_WIDTH), lambda b, k: (b * nblk + k, qb)),
                  pl.BlockSpec((seq, NA_WIDTH), lambda b, k: (b, kb)),
                  pl.BlockSpec((seq, NA_WIDTH), lambda b, k: (b, vb)),
                  pl.BlockSpec((n_ctx, NA_WIDTH), lambda b, k: (b, kb)),
                  pl.BlockSpec((n_ctx, NA_WIDTH), lambda b, k: (b, vb)),
                  pl.BlockSpec((1,) + table.shape[1:], lambda b, k: (l, 0, 0, 0))],
        out_specs=pl.BlockSpec((nq, NA_WIDTH), lambda b, k: (b * nblk + k, 0)),
        out_shape=jax.ShapeDtypeStruct((bsz * seq, NA_WIDTH), BF16),
        compiler_params=_cparams(("parallel", "arbitrary")),
        name="neighborhood_attention",
    )(proj_lat, proj_lat, proj_lat, proj_ctx, proj_ctx, table)


def _ctx_attn_kernel(q_ref, k_ref, v_ref, o_ref):
    for hp in range(NA_HEADS // 2):
        ln = pl.ds(hp * LANES, LANES)
        y = _attend(q_ref[:, ln], [k_ref[:, ln]], [v_ref[:, ln]], [None])
        o_ref[:, ln] = y.astype(o_ref.dtype)


def _ctx_attn(proj_ctx, bsz, n_ctx):
    qb, kb, vb = (_MY_OFF[n] // NA_WIDTH for n in ('nq', 'nk', 'nv'))
    spec = lambda j: pl.BlockSpec((n_ctx, NA_WIDTH), lambda b: (b, j))
    return pl.pallas_call(
        _ctx_attn_kernel,
        grid=(bsz,),
        in_specs=[spec(qb), spec(kb), spec(vb)],
        out_specs=pl.BlockSpec((n_ctx, NA_WIDTH), lambda b: (b, 0)),
        out_shape=jax.ShapeDtypeStruct((bsz * n_ctx, NA_WIDTH), BF16),
        compiler_params=_cparams(("parallel",)),
        name="context_attention",
    )(proj_ctx, proj_ctx, proj_ctx)


def _gelu_tanh(x):
    return 0.5 * x * (1.0 + jnp.tanh(math.sqrt(2.0 / math.pi) * (x + 0.044715 * (x * x * x))))


def _merge_kernel(x_ref, g0_ref, g1_ref, g2_ref, ys5_ref, yret_ref, yna_ref, wglu_ref, bglu_ref,
                  wbs5_ref, wbret_ref, wbna_ref, wout_ref, gate_ref, o_ref):
    ge = _gelu_tanh(ys5_ref[...].astype(F32))
    z = jnp.dot(ge.astype(BF16), wglu_ref[0], preferred_element_type=F32) + bglu_ref[0]
    s5 = (ge * _sigmoid(z)).astype(BF16)
    m = (_sigmoid(g0_ref[...].astype(F32)) * jnp.dot(s5, wbs5_ref[0], preferred_element_type=F32)
         + _sigmoid(g1_ref[...].astype(F32)) * jnp.dot(yret_ref[...], wbret_ref[0], preferred_element_type=F32)
         + _sigmoid(g2_ref[...].astype(F32)) * jnp.dot(yna_ref[...], wbna_ref[0], preferred_element_type=F32))
    o_ref[...] = x_ref[...] + gate_ref[0] * jnp.dot(m.astype(BF16), wout_ref[0], preferred_element_type=F32)


def _merge(x2, proj, ys5, yret, yna, mods, mod_row, l, wts, tm):
    m, d = x2.shape
    base = l * MOD_ROWS * 6
    rowblk = lambda w, j=0: pl.BlockSpec((tm, w), lambda i, j=j: (i, j))
    return pl.pallas_call(
        _merge_kernel,
        grid=(m // tm,),
        in_specs=[rowblk(d), rowblk(d, 0), rowblk(d, 1), rowblk(d, 2),
                  rowblk(S5_WIDTH), rowblk(RET_WIDTH), rowblk(NA_WIDTH)]
                 + [_layer_spec(w, l) for w in wts]
                 + [pl.BlockSpec((1, 1, d), lambda i: (base + mod_row(i) * 6 + 2, 0, 0))],
        out_specs=rowblk(d),
        out_shape=jax.ShapeDtypeStruct((m, d), F32),
        compiler_params=_cparams(("parallel",)),
        name="merge_residual",
    )(x2, proj, proj, proj, ys5, yret, yna, *wts, mods)


def _ffn_kernel(x_ref, sh_ref, sc_ref, gate_ref, wg_ref, wu_ref, wd_ref, fn_ref, o_ref, *, final, th):
    x = x_ref[...]
    h = (_rms(x) * (1.0 + sc_ref[0]) + sh_ref[0]).astype(BF16)
    hidden = wg_ref.shape[2]
    acc = jnp.zeros(x.shape, F32)
    for j in range(hidden // th):
        a = jnp.dot(h, wg_ref[0, :, j * th:(j + 1) * th], preferred_element_type=F32)
        b = jnp.dot(h, wu_ref[0, :, j * th:(j + 1) * th], preferred_element_type=F32)
        act = (a * _sigmoid(a) * b).astype(BF16)
        acc = acc + jnp.dot(act, wd_ref[0, j * th:(j + 1) * th, :], preferred_element_type=F32)
    y = x + gate_ref[0] * acc
    if final:
        y = _rms(y) * fn_ref[...]
    o_ref[...] = y


def _ffn(x2, mods, mod_row, l, wg, wu, wd, fn, tm, final):
    m, d = x2.shape
    base = l * MOD_ROWS * 6
    modspec = lambda k: pl.BlockSpec((1, 1, d), lambda i, k=k: (base + mod_row(i) * 6 + k, 0, 0))
    return pl.pallas_call(
        functools.partial(_ffn_kernel, final=final, th=256),
        grid=(m // tm,),
        in_specs=[pl.BlockSpec((tm, d), lambda i: (i, 0)), modspec(3), modspec(4), modspec(5),
                  _layer_spec(wg, l), _layer_spec(wu, l), _layer_spec(wd, l),
                  pl.BlockSpec(fn.shape, lambda i: (0, 0))],
        out_specs=pl.BlockSpec((tm, d), lambda i: (i, 0)),
        out_shape=jax.ShapeDtypeStruct((m, d), F32),
        compiler_params=_cparams(("parallel",)),
        name="swiglu_residual",
    )(x2, mods, mods, mods, wg, wu, wd, fn)


def kernel(x, c, ctx, c_ctx, w_ada, b_ada, w_in, s5_lam_re, s5_lam_im, s5_log_dt, s5_b_re, s5_b_im,
           s5_c_re, s5_c_im, s5_d, s5_w_glu, s5_b_glu, ret_theta, na_rpb, w_branch_s5, w_branch_ret,
           w_branch_na, w_out, w_ffn_gate, w_ffn_up, w_ffn_down, final_norm):
    bsz, seq, d = x.shape
    n_ctx = ctx.shape[1]
    depth = w_ada.shape[0]
    ctx_row = bsz
    assert bsz == SUBLANES and bsz + 1 <= MOD_ROWS

    cvec = jnp.zeros((MOD_ROWS, d), F32).at[:bsz].set(c).at[ctx_row].set(c_ctx)
    mods = _ada(cvec, w_ada, b_ada).reshape(depth * MOD_ROWS * 6, 1, d)

    w_in_k = jnp.concatenate(
        [w_in[:, :, _REF_OFF[n]:_REF_OFF[n] + _REF_W[n]] * _COL_SCALE.get(n, 1.0) for n in _MY_ORDER],
        axis=2).astype(BF16)

    cos, sin, pmat = _rotary_tables(seq)
    log_gamma = jax.nn.log_sigmoid(ret_theta.astype(F32)).reshape(depth, 2 * RET_HEADS)
    kin, vin, win, a_t = _s5_weights(s5_lam_re, s5_lam_im, s5_log_dt, s5_b_re, s5_b_im, s5_c_re, s5_c_im, s5_d)
    na_table = _na_bias_table(na_rpb)
    merge_w = (s5_w_glu.astype(BF16), s5_b_glu.reshape(depth, 1, -1).astype(F32), w_branch_s5.astype(BF16),
               w_branch_ret.astype(BF16), w_branch_na.astype(BF16), w_out.astype(BF16))
    ffn_w = (w_ffn_gate.astype(BF16), w_ffn_up.astype(BF16), w_ffn_down.astype(BF16),
             final_norm.reshape(1, d).astype(F32))

    tm_proj, tm = 2048, 512
    lat_row = lambda t: (lambda i: i // (seq // t))
    ctx_mod_row = lambda i: ctx_row
    ncc, ncl = n_ctx // S5_CHUNK, seq // S5_CHUNK

    x2 = x.reshape(bsz * seq, d)
    c2 = ctx.reshape(bsz * n_ctx, d)
    for l in range(depth):
        need_ctx = l < depth - 1
        proj_lat = _inproj(x2, mods, lat_row(tm_proj), w_in_k, l, tm_proj, 1664)
        proj_ctx = _inproj(c2, mods, ctx_mod_row, w_in_k, l, tm_proj, 1664)

        u_t = _to_chunked(proj_ctx.reshape(bsz, n_ctx, N_IN), proj_lat.reshape(bsz, seq, N_IN), bsz, n_ctx, seq)
        y_t = _s5(u_t, kin, vin, win, a_t, l, bsz, ncc, ncl)
        ys5_ctx, ys5_lat = _from_chunked(y_t, bsz, n_ctx, seq)

        yret_lat, yret_ctx = _retention(proj_lat, proj_ctx, log_gamma, cos, sin, pmat, l, bsz, seq, n_ctx, need_ctx)
        yna_lat = _na(proj_lat, proj_ctx, na_table, l, bsz, seq, n_ctx)

        x2 = _merge(x2, proj_lat, ys5_lat.reshape(bsz * seq, S5_WIDTH), yret_lat, yna_lat, mods, lat_row(tm), l, merge_w, tm)
        x2 = _ffn(x2, mods, lat_row(tm), l, *ffn_w, tm, final=not need_ctx)
        if need_ctx:
            yna_ctx = _ctx_attn(proj_ctx, bsz, n_ctx)
            c2 = _merge(c2, proj_ctx, ys5_ctx.reshape(bsz * n_ctx, S5_WIDTH), yret_ctx, yna_ctx, mods, ctx_mod_row, l,
                        merge_w, tm)
            c2 = _ffn(c2, mods, ctx_mod_row, l, *ffn_w, tm, final=False)
    return x2.reshape(bsz, seq, d)
```

```python
import functools
import math

import numpy as np
import jax
import jax.numpy as jnp
from jax import lax
from jax.experimental import pallas as pl
from jax.experimental.pallas import tpu as pltpu

F32 = jnp.float32
BF16 = jnp.bfloat16
U32 = jnp.uint32
HIGHEST = lax.Precision.HIGHEST

D_MODEL = 1024
GRID_W = 64
S5_WIDTH = 512
S5_GROUP = 16
S5_GROUPS = S5_WIDTH // S5_GROUP
S5_STATE = 64
S5_CHUNK = 16
RET_HEADS = 4
RET_DK = 64
RET_DV = 128
RET_QK_WIDTH = RET_HEADS * RET_DK
RET_WIDTH = RET_HEADS * RET_DV
RET_T = 256
NA_HEADS = 8
NA_HEAD_DIM = 64
NA_WIDTH = NA_HEADS * NA_HEAD_DIM
NA_ROWS = 8
NA_COLS = 16
NA_QROWS = 4
NA_KROWS = 12
NA_NLAG = 2 * NA_ROWS
N_BRANCH = 3
ROPE_BASE = 10000.0
RMS_EPS = 1e-6
GN_EPS = 1e-5
NEG_INF = -1e30
LANES = 128
SUBLANES = 8
MOD_ROWS = 16

_REF_SPLIT = (S5_WIDTH, RET_QK_WIDTH, RET_WIDTH, NA_WIDTH, NA_WIDTH,
              RET_QK_WIDTH, RET_WIDTH, NA_WIDTH, N_BRANCH * D_MODEL)
_REF_NAMES = ('u', 'rk', 'rv', 'nk', 'nv', 'rq', 'rg', 'nq', 'gates')
_REF_OFF = dict(zip(_REF_NAMES, np.concatenate([[0], np.cumsum(_REF_SPLIT)[:-1]]).tolist()))
_REF_W = dict(zip(_REF_NAMES, _REF_SPLIT))
_MY_ORDER = ('gates', 'u', 'rv', 'nk', 'nv', 'rg', 'nq', 'rk', 'rq')
_MY_OFF = {}
_o = 0
for _n in _MY_ORDER:
    _MY_OFF[_n] = _o
    _o += _REF_W[_n]
N_IN = _o
LOG2E = math.log2(math.e)
_COL_SCALE = {'nq': NA_HEAD_DIM ** -0.5 * LOG2E, 'rk': RET_DK ** -0.5}

VMEM_LIMIT = 56 * 1024 * 1024


def _cparams(sem):
    return pltpu.CompilerParams(dimension_semantics=sem, vmem_limit_bytes=VMEM_LIMIT)


def _sigmoid(x):
    return 1.0 / (1.0 + jnp.exp(-x))


def _rms(x):
    return x * lax.rsqrt(jnp.mean(x * x, axis=-1, keepdims=True) + RMS_EPS)


def _layer_spec(arr, l):
    nd = arr.ndim
    return pl.BlockSpec((1,) + arr.shape[1:], lambda *_: (l,) + (0,) * (nd - 1))


def _ada_kernel(c_ref, w_ref, b_ref, o_ref):
    c = c_ref[...]
    s = c * _sigmoid(c)
    o_ref[0] = jnp.dot(s, w_ref[0], preferred_element_type=F32, precision=HIGHEST) + b_ref[0]


def _ada(cvec, w_ada, b_ada):
    depth, d, n = w_ada.shape
    tn = 1536
    rows = cvec.shape[0]
    return pl.pallas_call(
        _ada_kernel,
        grid=(depth, n // tn),
        in_specs=[pl.BlockSpec((rows, d), lambda l, j: (0, 0)),
                  pl.BlockSpec((1, d, tn), lambda l, j: (l, 0, j)),
                  pl.BlockSpec((1, 1, tn), lambda l, j: (l, 0, j))],
        out_specs=pl.BlockSpec((1, rows, tn), lambda l, j: (l, 0, j)),
        out_shape=jax.ShapeDtypeStruct((depth, rows, n), F32),
        compiler_params=_cparams(("parallel", "parallel")),
        name="ada_mod",
    )(cvec, w_ada, b_ada.reshape(depth, 1, n))


def _inproj_kernel(x_ref, sh_ref, sc_ref, w_ref, o_ref, h_ref):
    @pl.when(pl.program_id(1) == 0)
    def _():
        h = _rms(x_ref[...]) * (1.0 + sc_ref[0]) + sh_ref[0]
        h_ref[...] = h.astype(BF16)

    o_ref[...] = jnp.dot(h_ref[...], w_ref[0], preferred_element_type=F32).astype(o_ref.dtype)


def _inproj(x2, mods, mod_row, w, l, tm, tn):
    m, d = x2.shape
    n = w.shape[2]
    base = l * MOD_ROWS * 6
    return pl.pallas_call(
        _inproj_kernel,
        grid=(m // tm, n // tn),
        in_specs=[pl.BlockSpec((tm, d), lambda i, j: (i, 0)),
                  pl.BlockSpec((1, 1, d), lambda i, j: (base + mod_row(i) * 6 + 0, 0, 0)),
                  pl.BlockSpec((1, 1, d), lambda i, j: (base + mod_row(i) * 6 + 1, 0, 0)),
                  pl.BlockSpec((1, d, tn), lambda i, j: (l, 0, j))],
        out_specs=pl.BlockSpec((tm, tn), lambda i, j: (i, j)),
        out_shape=jax.ShapeDtypeStruct((m, n), BF16),
        scratch_shapes=[pltpu.VMEM((tm, d), BF16)],
        compiler_params=_cparams(("parallel", "arbitrary")),
        name="in_proj",
    )(x2, mods, mods, w)


def _s5w_kernel(lam_ref, btr_ref, bti_ref, ctr_ref, cti_ref, dd_ref, kin_ref, vin_ref, win_ref, at_ref, *, ng):
    t_n, h_n = S5_CHUNK, S5_GROUP
    width = t_n * h_n
    lane = lax.broadcasted_iota(jnp.int32, (1, LANES), 1)
    f_lane = lane < S5_STATE
    tau = lax.broadcasted_iota(jnp.int32, (3 * SUBLANES, LANES), 0).astype(F32)
    lane_w = lax.broadcasted_iota(jnp.int32, (h_n, width), 1)
    for g in range(ng):
        lam_re, lam_im, dt = lam_ref[g, 0:1, :], lam_ref[g, 1:2, :], lam_ref[g, 2:3, :]
        mag = jnp.exp(tau * (lam_re * dt))
        ang = tau * (lam_im * dt)
        pr, pi = mag * jnp.cos(ang), mag * jnp.sin(ang)
        ab_re, ab_im = pr[1:2], pi[1:2]
        den = lam_re * lam_re + lam_im * lam_im
        f_re = ((ab_re - 1.0) * lam_re + ab_im * lam_im) / den
        f_im = (ab_im * lam_re - (ab_re - 1.0) * lam_im) / den
        btr, bti = btr_ref[g], bti_ref[g]
        bbr = f_re * btr - f_im * bti
        bbi = f_re * bti + f_im * btr
        ctr, cti = ctr_ref[g], cti_ref[g]

        def powers(pf, pb):
            rr = [jnp.broadcast_to(jnp.where(f_lane, pr[pf[t]:pf[t] + 1], pr[pb[t]:pb[t] + 1]), (h_n, LANES))
                  for t in range(t_n)]
            ri = [jnp.broadcast_to(jnp.where(f_lane, pi[pf[t]:pf[t] + 1], pi[pb[t]:pb[t] + 1]), (h_n, LANES))
                  for t in range(t_n)]
            return jnp.concatenate(rr, axis=0), jnp.concatenate(ri, axis=0)

        tile = lambda a: jnp.concatenate([a] * t_n, axis=0)
        bbr_t, bbi_t, ctr_t, cti_t = tile(bbr), tile(bbi), tile(ctr), tile(cti)

        xr, xi = powers([t_n - 1 - t for t in range(t_n)], list(range(t_n)))
        vin = jnp.concatenate([xr * bbr_t - xi * bbi_t, xr * bbi_t + xi * bbr_t], axis=1)
        vin_ref[g] = vin.astype(vin_ref.dtype)

        yr, yi = powers([t + 1 for t in range(t_n)], [t_n - t for t in range(t_n)])
        win_t = jnp.concatenate([ctr_t * yr - cti_t * yi, -(ctr_t * yi + cti_t * yr)], axis=1)
        win_ref[g] = win_t.T.astype(win_ref.dtype)

        zr, zi = powers(list(range(t_n)), [t_n - 1 - t for t in range(t_n)])
        fmat = jnp.concatenate([ctr_t * zr - cti_t * zi, ctr_t * zi + cti_t * zr], axis=1)
        dn = (((1,), (1,)), ((), ()))
        lhs_f = jnp.concatenate([jnp.where(f_lane, bbr, 0.0), jnp.where(f_lane, -bbi, 0.0)], axis=1)
        lhs_b = jnp.concatenate([jnp.where(f_lane, 0.0, bbr), jnp.where(f_lane, 0.0, -bbi)], axis=1)
        w_f = lax.dot_general(lhs_f, fmat, dn, preferred_element_type=F32, precision=HIGHEST)
        w_b = lax.dot_general(lhs_b, fmat, dn, preferred_element_type=F32, precision=HIGHEST)
        w_f = w_f + jnp.concatenate([dd_ref[g], jnp.zeros((h_n, width - LANES), F32)], axis=1)
        blocks = []
        for t in range(t_n):
            sh_f = t * h_n
            sh_b = (width - (t_n - 1 - t) * h_n) % width
            fw = w_f if sh_f == 0 else pltpu.roll(w_f, sh_f, 1)
            bw = w_b if sh_b == 0 else pltpu.roll(w_b, sh_b, 1)
            blocks.append(jnp.where(lane_w >= t * h_n, fw, 0.0) + jnp.where(lane_w < (t + 1) * h_n, bw, 0.0))
        kin_ref[g] = jnp.concatenate(blocks, axis=0).astype(kin_ref.dtype)
        at_ref[g] = jnp.broadcast_to(jnp.concatenate([pr[t_n:t_n + 1], pi[t_n:t_n + 1]], axis=1),
                                     (SUBLANES, 2 * LANES))


def _s5_weights(lam_re, lam_im, log_dt, b_re, b_im, c_re, c_im, d_skip, ng=4):
    depth, _, g_n, p_n = lam_re.shape
    h_n = S5_GROUP
    n = depth * g_n
    width = S5_CHUNK * h_n
    pair = lambda a: a.astype(F32).transpose(0, 2, 1, 3).reshape(n, 1, 2 * p_n)
    dt = jnp.broadcast_to(jnp.exp(log_dt.astype(F32))[..., None], lam_re.shape)
    lam = jnp.concatenate([pair(lam_re), pair(lam_im), pair(dt), jnp.zeros((n, SUBLANES - 3, 2 * p_n), F32)], axis=1)
    bt = lambda a: a.astype(F32).transpose(0, 2, 4, 1, 3).reshape(n, h_n, 2 * p_n)
    ct = lambda a: a.astype(F32).transpose(0, 2, 3, 1, 4).reshape(n, h_n, 2 * p_n)
    dd = jnp.eye(h_n, LANES, dtype=F32)[None] * d_skip.astype(F32).reshape(n, h_n, 1)
    small = pl.BlockSpec((ng, h_n, LANES), lambda i: (i, 0, 0))
    big = pl.BlockSpec((ng, width, width), lambda i: (i, 0, 0))
    return pl.pallas_call(
        functools.partial(_s5w_kernel, ng=ng),
        grid=(n // ng,),
        in_specs=[pl.BlockSpec((ng, SUBLANES, LANES), lambda i: (i, 0, 0)), small, small, small, small, small],
        out_specs=[big, big, big, pl.BlockSpec((ng, SUBLANES, width), lambda i: (i, 0, 0))],
        out_shape=[jax.ShapeDtypeStruct((n, width, width), BF16)] * 3
                  + [jax.ShapeDtypeStruct((n, SUBLANES, width), F32)],
        compiler_params=_cparams(("parallel",)),
        name="s5_weights",
    )(lam, bt(b_re), bt(b_im), ct(c_re), ct(c_im), dd)


def _s5_kernel(u_ref, kin_ref, vin_ref, win_ref, a_ref, y_ref, s_scr, xa_scr, xb_scr, *, ng, ncc, ncl, bsz):
    rows = (ncc + ncl) * bsz
    half = 2 * S5_STATE
    for g in range(ng):
        s_scr[g] = jnp.dot(u_ref[g], vin_ref[g], preferred_element_type=F32)
    lane = lax.broadcasted_iota(jnp.int32, (bsz, 2 * half), 1)
    fwd_lane = (lane % half) < S5_STATE
    a_re = [a_ref[g, :, :half] for g in range(ng)]
    a_im = [a_ref[g, :, half:] for g in range(ng)]

    def step(fc, bc, xs):
        rf = pl.ds(pl.multiple_of(fc * bsz, bsz), bsz)
        rb = pl.ds(pl.multiple_of(bc * bsz, bsz), bsz)
        out = []
        for g in range(ng):
            x = xs[g]
            xa_scr[g, rf, :] = x
            xb_scr[g, rb, :] = x
            s = jnp.where(fwd_lane, s_scr[g, rf, :], s_scr[g, rb, :])
            xr, xi = x[:, :half], x[:, half:]
            nr = a_re[g] * xr - a_im[g] * xi + s[:, :half]
            ni = a_re[g] * xi + a_im[g] * xr + s[:, half:]
            out.append(jnp.concatenate([nr, ni], axis=1))
        return tuple(out)

    xs = tuple(jnp.zeros((bsz, 2 * half), F32) for _ in range(ng))
    xs = lax.fori_loop(0, ncc, lambda i, c: step(i, ncc - 1 - i, c), xs)
    xs = lax.fori_loop(0, ncl, lambda i, c: step(ncc + i, ncc + ncl - 1 - i, c), xs)
    lane_r = lax.broadcasted_iota(jnp.int32, (rows, 2 * half), 1)
    fwd_r = (lane_r % half) < S5_STATE
    for g in range(ng):
        x_in = jnp.where(fwd_r, xa_scr[g], xb_scr[g]).astype(BF16)
        y = (jnp.dot(u_ref[g], kin_ref[g], preferred_element_type=F32)
             + jnp.dot(x_in, win_ref[g], preferred_element_type=F32))
        y_ref[g] = y.astype(y_ref.dtype)


def _s5(u_t, kin, vin, win, a_t, l, bsz, ncc, ncl, ng=4):
    g_n, rows, w = u_t.shape
    nblk = g_n // ng
    wspec = pl.BlockSpec((ng, w, w), lambda i: (l * nblk + i, 0, 0))
    return pl.pallas_call(
        functools.partial(_s5_kernel, ng=ng, ncc=ncc, ncl=ncl, bsz=bsz),
        grid=(nblk,),
        in_specs=[pl.BlockSpec((ng, rows, w), lambda i: (i, 0, 0)), wspec, wspec, wspec,
                  pl.BlockSpec((ng, SUBLANES, w), lambda i: (l * nblk + i, 0, 0))],
        out_specs=pl.BlockSpec((ng, rows, w), lambda i: (i, 0, 0)),
        out_shape=jax.ShapeDtypeStruct((g_n, rows, w), BF16),
        scratch_shapes=[pltpu.VMEM((ng, rows, w), F32)] * 3,
        compiler_params=_cparams(("parallel",)),
        name="s5_mixer",
    )(u_t, kin, vin, win, a_t)


_RL_TOK = S5_CHUNK * S5_CHUNK
_GPL = LANES // S5_GROUP
_NSLAB = S5_WIDTH // LANES


def _block_transpose(tiles):
    blk = lax.broadcasted_iota(jnp.int32, tiles[0].shape, 1) // S5_GROUP
    tiles = list(tiles)
    s = _GPL // 2
    while s >= 1:
        hi = (blk & s) != 0
        for a in range(_GPL):
            if a & s:
                continue
            b = a + s
            ta, tb = tiles[a], tiles[b]
            tiles[a] = jnp.where(hi, pltpu.roll(tb, s * S5_GROUP, 1), ta)
            tiles[b] = jnp.where(hi, tb, pltpu.roll(ta, LANES - s * S5_GROUP, 1))
        s //= 2
    return tiles


def _pack_pair(lo, hi):
    lo32 = lax.bitcast_convert_type(lo.astype(F32), U32)
    hi32 = lax.bitcast_convert_type(hi.astype(F32), U32)
    return (lo32 >> 16) | hi32


def _unpack_pair(w, dtype):
    lo = lax.bitcast_convert_type(w << 16, F32)
    hi = lax.bitcast_convert_type(w & jnp.uint32(0xFFFF0000), F32)
    return lo.astype(dtype), hi.astype(dtype)


def _to_chunked_kernel(cu_ref, lu_ref, o_ref, x_scr, y_scr, *, bsz):
    i = pl.program_id(0)
    hrow = _RL_TOK // 2

    def fill(src):
        for b in range(bsz):
            for j in range(_NSLAB):
                ln = slice(j * LANES, (j + 1) * LANES)
                x_scr[b, j] = _pack_pair(src[b, :hrow, ln], src[b, hrow:, ln])

    @pl.when(i == 0)
    def _():
        fill(cu_ref)

    @pl.when(i > 0)
    def _():
        fill(lu_ref)

    def per_batch(b, carry):
        for j in range(_NSLAB):
            for half in range(2):
                v = [x_scr[b, j, pl.ds(half * _GPL + tl, SUBLANES, stride=S5_CHUNK), :] for tl in range(_GPL)]
                for q, out in enumerate(_block_transpose(v)):
                    y_scr[j * _GPL + q, half, pl.ds(b, SUBLANES, stride=bsz), :] = out
        return carry

    lax.fori_loop(0, bsz, per_batch, 0)
    hout = SUBLANES * bsz
    for g in range(S5_GROUPS):
        for half in range(2):
            lo, hi = _unpack_pair(y_scr[g, half], o_ref.dtype)
            o_ref[g, :hout, half * LANES:(half + 1) * LANES] = lo
            o_ref[g, hout:, half * LANES:(half + 1) * LANES] = hi


def _to_chunked(proj_ctx3, proj_lat3, bsz, n_ctx, seq):
    assert n_ctx == _RL_TOK and seq % _RL_TOK == 0
    nlat = seq // _RL_TOK
    ub = _MY_OFF['u'] // S5_WIDTH
    rows = S5_CHUNK * bsz
    return pl.pallas_call(
        functools.partial(_to_chunked_kernel, bsz=bsz),
        grid=(nlat + 1,),
        in_specs=[pl.BlockSpec((bsz, _RL_TOK, S5_WIDTH), lambda i: (0, 0, ub)),
                  pl.BlockSpec((bsz, _RL_TOK, S5_WIDTH), lambda i: (0, jnp.maximum(i - 1, 0), ub))],
        out_specs=pl.BlockSpec((S5_GROUPS, rows, 2 * LANES), lambda i: (0, i, 0)),
        out_shape=jax.ShapeDtypeStruct((S5_GROUPS, (nlat + 1) * rows, 2 * LANES), BF16),
        scratch_shapes=[pltpu.VMEM((bsz, _NSLAB, _RL_TOK // 2, LANES), U32),
                        pltpu.VMEM((S5_GROUPS, 2, rows // 2, LANES), U32)],
        compiler_params=_cparams(("arbitrary",)),
        name="s5_to_chunked",
    )(proj_ctx3, proj_lat3)


def _from_chunked_kernel(y_ref, co_ref, lo_ref, z_scr, w_scr, *, bsz):
    i = pl.program_id(0)
    hin = SUBLANES * bsz
    for g in range(S5_GROUPS):
        for half in range(2):
            ln = slice(half * LANES, (half + 1) * LANES)
            z_scr[g, half] = _pack_pair(y_ref[g, :hin, ln], y_ref[g, hin:, ln])

    def per_batch(b, carry):
        for j in range(_NSLAB):
            for half in range(2):
                o = [z_scr[j * _GPL + q, half, pl.ds(b, SUBLANES, stride=bsz), :] for q in range(_GPL)]
                for tl, out in enumerate(_block_transpose(o)):
                    w_scr[b, j, pl.ds(half * _GPL + tl, SUBLANES, stride=S5_CHUNK), :] = out
        return carry

    lax.fori_loop(0, bsz, per_batch, 0)
    hrow = _RL_TOK // 2

    def drain(dst):
        for b in range(bsz):
            for j in range(_NSLAB):
                lo, hi = _unpack_pair(w_scr[b, j], dst.dtype)
                dst[b, :hrow, j * LANES:(j + 1) * LANES] = lo
                dst[b, hrow:, j * LANES:(j + 1) * LANES] = hi

    @pl.when(i == 0)
    def _():
        drain(co_ref)

    @pl.when(i > 0)
    def _():
        drain(lo_ref)


def _from_chunked(y_t, bsz, n_ctx, seq):
    nlat = seq // _RL_TOK
    rows = S5_CHUNK * bsz
    return pl.pallas_call(
        functools.partial(_from_chunked_kernel, bsz=bsz),
        grid=(nlat + 1,),
        in_specs=[pl.BlockSpec((S5_GROUPS, rows, 2 * LANES), lambda i: (0, i, 0))],
        out_specs=[pl.BlockSpec((bsz, _RL_TOK, S5_WIDTH), lambda i: (0, 0, 0)),
                   pl.BlockSpec((bsz, _RL_TOK, S5_WIDTH), lambda i: (0, jnp.maximum(i - 1, 0), 0))],
        out_shape=[jax.ShapeDtypeStruct((bsz, n_ctx, S5_WIDTH), BF16),
                   jax.ShapeDtypeStruct((bsz, seq, S5_WIDTH), BF16)],
        scratch_shapes=[pltpu.VMEM((S5_GROUPS, 2, rows // 2, LANES), U32),
                        pltpu.VMEM((bsz, _NSLAB, _RL_TOK // 2, LANES), U32)],
        compiler_params=_cparams(("arbitrary",)),
        name="s5_from_chunked",
    )(y_t)


def _rotary_tables(seq):
    quarter = RET_DK // 4
    pos = jnp.arange(seq)
    inv_freq = ROPE_BASE ** (-jnp.arange(quarter, dtype=F32) / quarter)
    ang_r = (pos // GRID_W).astype(F32)[:, None] * inv_freq[None, :]
    ang_c = (pos % GRID_W).astype(F32)[:, None] * inv_freq[None, :]
    cos = jnp.concatenate([jnp.cos(ang_r)] * 2 + [jnp.cos(ang_c)] * 2, axis=-1)
    sin = jnp.concatenate([jnp.sin(ang_r)] * 2 + [jnp.sin(ang_c)] * 2, axis=-1)
    cos = jnp.tile(cos, (1, RET_HEADS))
    sin = jnp.tile(sin, (1, RET_HEADS))
    p = np.zeros((RET_QK_WIDTH, RET_QK_WIDTH), np.float32)
    for d in range(RET_QK_WIDTH):
        if d % (2 * quarter) < quarter:
            p[d + quarter, d] = -1.0
        else:
            p[d - quarter, d] = 1.0
    return cos, sin, jnp.asarray(p, BF16)


def _ret_kernel(lg_ref, q_ref, k_ref, v_ref, g_ref, cq_ref, ck_ref, cv_ref, cg_ref,
                cos_ref, sin_ref, p_ref, o_ref, co_ref, krot_scr, sin_scr, dm_scr, *, layer, need_ctx, ncl):
    t = RET_T
    qkw, vw = RET_QK_WIDTH, RET_WIDTH

    def per_head(shape, axis, width, d):
        head = lax.broadcasted_iota(jnp.int32, shape, axis) // width
        out = jnp.zeros(shape, F32)
        for h in range(RET_HEADS):
            out = jnp.where(head == h, lg_ref[layer, d * RET_HEADS + h], out)
        return out

    row = lax.broadcasted_iota(jnp.int32, (t, qkw), 0).astype(F32)
    lgf = per_head((t, qkw), 1, RET_DK, 0)
    lgb = per_head((t, qkw), 1, RET_DK, 1)
    qdec_f = jnp.exp((row + 1.0) * lgf)
    qdec_b = jnp.exp((t - row) * lgb)
    kdec_f = jnp.exp((t - 1.0 - row) * lgf)
    kdec_b = jnp.exp(row * lgb)
    cdec_f = jnp.exp(float(t) * per_head((qkw, vw), 0, RET_DK, 0))
    cdec_b = jnp.exp(float(t) * per_head((qkw, vw), 0, RET_DK, 1))
    blk = (lax.broadcasted_iota(jnp.int32, (qkw, vw), 0) // RET_DK
           == lax.broadcasted_iota(jnp.int32, (qkw, vw), 1) // RET_DV)
    head_lane = lax.broadcasted_iota(jnp.int32, (t, qkw), 1) // RET_DK

    ii = lax.broadcasted_iota(jnp.int32, (t, t), 0)
    jj = lax.broadcasted_iota(jnp.int32, (t, t), 1)
    dif = (ii - jj).astype(F32)
    for h in range(RET_HEADS):
        df = jnp.where(dif >= 0, jnp.exp(jnp.where(dif >= 0, dif, 0.0) * lg_ref[layer, h]), 0.0)
        db = jnp.where(dif < 0, jnp.exp(jnp.where(dif < 0, -dif, 0.0) * lg_ref[layer, RET_HEADS + h]), 0.0)
        dm_scr[h] = df + db

    def rotary(x_bf, c):
        rows = pl.ds(c * t, t)
        swapped = jnp.dot(x_bf, p_ref[...], preferred_element_type=F32)
        return x_bf.astype(F32) * cos_ref[rows, :] + swapped * sin_ref[rows, :]

    krot_scr[pl.ds(0, t), :] = ck_ref[...]
    for c in range(ncl):
        krot_scr[pl.ds((c + 1) * t, t), :] = rotary(k_ref[pl.ds(c * t, t), :], c).astype(BF16)

    def v_chunk(c):
        return cv_ref[...] if c == 0 else v_ref[pl.ds((c - 1) * t, t), :]

    def kv(c, kdec):
        kd = (krot_scr[pl.ds(c * t, t), :].astype(F32) * kdec).astype(BF16)
        return lax.dot_general(kd, v_chunk(c), (((0,), (0,)), ((), ())), preferred_element_type=F32)

    s = jnp.zeros((qkw, vw), F32)
    for c in range(ncl + 1):
        sin_scr[c, pl.ds(0, qkw), :] = jnp.where(blk, s, 0.0).astype(BF16)
        if c < ncl:
            s = cdec_f * s + kv(c, kdec_f)
    sin_scr[0, pl.ds(qkw, qkw), :] = jnp.zeros((qkw, vw), BF16)
    s = kv(0, kdec_b)
    for c in range(ncl, 0, -1):
        sin_scr[c, pl.ds(qkw, qkw), :] = jnp.where(blk, s, 0.0).astype(BF16)
        if c > 1:
            s = cdec_b * s + kv(c, kdec_b)

    for c in range(0 if need_ctx else 1, ncl + 1):
        if c == 0:
            q = cq_ref[...].astype(F32)
            gate = cg_ref[...].astype(F32)
        else:
            q = rotary(q_ref[pl.ds((c - 1) * t, t), :], c - 1)
            gate = g_ref[pl.ds((c - 1) * t, t), :].astype(F32)
        q_bf = q.astype(BF16)
        cross = (jnp.dot((q * qdec_f).astype(BF16), sin_scr[c, pl.ds(0, qkw), :], preferred_element_type=F32)
                 + jnp.dot((q * qdec_b).astype(BF16), sin_scr[c, pl.ds(qkw, qkw), :], preferred_element_type=F32))
        k_c = krot_scr[pl.ds(c * t, t), :]
        v_c = v_chunk(c)
        outs = []
        for h in range(RET_HEADS):
            qm = jnp.where(head_lane == h, q_bf, jnp.zeros_like(q_bf))
            sc = lax.dot_general(qm, k_c, (((1,), (1,)), ((), ())), preferred_element_type=F32)
            sc = (sc * dm_scr[h]).astype(BF16)
            o = (jnp.dot(sc, v_c[:, h * RET_DV:(h + 1) * RET_DV], preferred_element_type=F32)
                 + cross[:, h * RET_DV:(h + 1) * RET_DV])
            mu = jnp.mean(o, axis=-1, keepdims=True)
            var = jnp.mean(jnp.square(o - mu), axis=-1, keepdims=True)
            outs.append((o - mu) * lax.rsqrt(var + GN_EPS))
        y = gate * _sigmoid(gate) * jnp.concatenate(outs, axis=1)
        if c == 0:
            co_ref[...] = y.astype(co_ref.dtype)
        else:
            o_ref[pl.ds((c - 1) * t, t), :] = y.astype(o_ref.dtype)
    if not need_ctx:
        co_ref[...] = jnp.zeros(co_ref.shape, co_ref.dtype)


def _retention(proj_lat, proj_ctx, lg, cos, sin, pmat, l, bsz, seq, n_ctx, need_ctx):
    t = RET_T
    ncl = seq // t
    assert n_ctx == t
    qb, kb = _MY_OFF['rq'] // RET_QK_WIDTH, _MY_OFF['rk'] // RET_QK_WIDTH
    vb, gb = _MY_OFF['rv'] // RET_WIDTH, _MY_OFF['rg'] // RET_WIDTH

    def col(n, w, j):
        return pl.BlockSpec((n, w), lambda b, j=j: (b, j))

    const = lambda shape: pl.BlockSpec(shape, lambda b: (0,) * len(shape))
    return pl.pallas_call(
        functools.partial(_ret_kernel, layer=l, need_ctx=need_ctx, ncl=ncl),
        grid=(bsz,),
        in_specs=[pl.BlockSpec(memory_space=pltpu.SMEM),
                  col(seq, RET_QK_WIDTH, qb), col(seq, RET_QK_WIDTH, kb), col(seq, RET_WIDTH, vb), col(seq, RET_WIDTH, gb),
                  col(n_ctx, RET_QK_WIDTH, qb), col(n_ctx, RET_QK_WIDTH, kb), col(n_ctx, RET_WIDTH, vb), col(n_ctx, RET_WIDTH, gb),
                  const((seq, RET_QK_WIDTH)), const((seq, RET_QK_WIDTH)), const((RET_QK_WIDTH, RET_QK_WIDTH))],
        out_specs=[pl.BlockSpec((seq, RET_WIDTH), lambda b: (b, 0)),
                   pl.BlockSpec((n_ctx, RET_WIDTH), lambda b: (b, 0))],
        out_shape=[jax.ShapeDtypeStruct((bsz * seq, RET_WIDTH), BF16),
                   jax.ShapeDtypeStruct((bsz * n_ctx, RET_WIDTH), BF16)],
        scratch_shapes=[pltpu.VMEM((seq + n_ctx, RET_QK_WIDTH), BF16),
                        pltpu.VMEM((ncl + 1, 2 * RET_QK_WIDTH, RET_WIDTH), BF16),
                        pltpu.VMEM((RET_HEADS, t, t), F32)],
        compiler_params=_cparams(("parallel",)),
        name="retention",
    )(lg, proj_lat, proj_lat, proj_lat, proj_lat, proj_ctx, proj_ctx, proj_ctx, proj_ctx, cos, sin, pmat)


def _na_block_start(kblk, rows):
    return jnp.clip(kblk * NA_QROWS - NA_ROWS // 2, 0, rows - NA_KROWS)


def _na_bias_table(rpb):
    depth, heads, nlag, ncol = rpb.shape
    cols = np.arange(GRID_W)
    cs = np.clip(cols - NA_COLS // 2, 0, GRID_W - NA_COLS)
    valid_c = (cols[None, :] >= cs[:, None]) & (cols[None, :] < cs[:, None] + NA_COLS)
    pad = GRID_W - 1
    padded = jnp.pad(rpb.astype(F32) * LOG2E, ((0, 0), (0, 0), (0, 0), (pad, pad)))
    toe = jnp.stack([padded[..., pad + NA_COLS - 1 - qc: pad + NA_COLS - 1 - qc + GRID_W] for qc in range(GRID_W)],
                    axis=-2)
    toe = jnp.where(jnp.asarray(valid_c), toe, NEG_INF)
    toe = jnp.pad(toe, ((0, 0), (0, 0), (1, NA_NLAG + 1 - nlag - 1), (0, 0), (0, 0)))
    table = jnp.concatenate([toe[:, :, :NA_NLAG], toe[:, :, 1:NA_NLAG + 1]], axis=-1)
    return table.reshape(depth, heads * NA_NLAG, GRID_W, 2 * GRID_W)


def _attend(q_pair, k_list, v_list, bias_list):
    lane = lax.broadcasted_iota(jnp.int32, q_pair.shape, 1) // NA_HEAD_DIM
    outs = []
    for hh in range(2):
        qm = jnp.where(lane == hh, q_pair, jnp.zeros_like(q_pair))
        ss = []
        for k_i, b_i in zip(k_list, bias_list):
            s = lax.dot_general(qm, k_i, (((1,), (1,)), ((), ())), preferred_element_type=F32)
            if b_i is not None:
                s = s + b_i(hh)
            ss.append(s)
        m = ss[0].max(axis=-1, keepdims=True)
        for s in ss[1:]:
            m = jnp.maximum(m, s.max(axis=-1, keepdims=True))
        ps = [jnp.exp2(s - m) for s in ss]
        den = ps[0].sum(axis=-1, keepdims=True)
        for p in ps[1:]:
            den = den + p.sum(axis=-1, keepdims=True)
        acc = jnp.dot(ps[0].astype(BF16), v_list[0], preferred_element_type=F32)
        for p, v_i in zip(ps[1:], v_list[1:]):
            acc = acc + jnp.dot(p.astype(BF16), v_i, preferred_element_type=F32)
        outs.append(acc / den)
    return jnp.where(lane == 0, outs[0], outs[1])


def _na_kernel(q_ref, k_ref, v_ref, ck_ref, cv_ref, tab_ref, o_ref, *, rows):
    nq, nk = NA_QROWS * GRID_W, NA_KROWS * GRID_W
    kblk = pl.program_id(1)
    r0 = kblk * NA_QROWS
    ks_row = _na_block_start(kblk, rows)
    ks = pl.multiple_of(ks_row * GRID_W, NA_QROWS * GRID_W)
    qr = r0 + lax.broadcasted_iota(jnp.int32, (nq, nk), 0) // GRID_W
    kr = ks_row + lax.broadcasted_iota(jnp.int32, (nq, nk), 1) // GRID_W
    rs = jnp.clip(qr - NA_ROWS // 2, 0, rows - NA_ROWS)
    row_mask = jnp.where((kr >= rs) & (kr < rs + NA_ROWS), 0.0, NEG_INF)

    def bias(h):
        row_blocks = []
        for qrl in range(NA_QROWS):
            tiles = []
            for kp in range(NA_KROWS // 2):
                lag = ks_row - r0 + 2 * kp - qrl + NA_ROWS - 1
                idx = h * NA_NLAG + jnp.clip(lag, -1, NA_NLAG - 2) + 1
                tiles.append(tab_ref[0, idx])
            row_blocks.append(jnp.concatenate(tiles, axis=1))
        return jnp.concatenate(row_blocks, axis=0) + row_mask

    for hp in range(NA_HEADS // 2):
        ln = pl.ds(hp * LANES, LANES)
        y = _attend(q_ref[:, ln],
                    [k_ref[pl.ds(ks, nk), ln], ck_ref[:, ln]],
                    [v_ref[pl.ds(ks, nk), ln], cv_ref[:, ln]],
                    [lambda hh, hp=hp: bias(2 * hp + hh), None])
        o_ref[:, ln] = y.astype(o_ref.dtype)


def _na(proj_lat, proj_ctx, table, l, bsz, seq, n_ctx):
    rows = seq // GRID_W
    nq = NA_QROWS * GRID_W
    nblk = seq // nq
    qb, kb, vb = (_MY_OFF[n] // NA_WIDTH for n in ('nq', 'nk', 'nv'))
    return pl.pallas_call(
        functools.partial(_na_kernel, rows=rows),
        grid=(bsz, nblk),
        in_specs=[pl.BlockSpec((nq, NA_WIDTH), lambda b, k: (b * nblk + k, qb)),
                  pl.BlockSpec((seq, NA_WIDTH), lambda b, k: (b, kb)),
                  pl.BlockSpec((seq, NA_WIDTH), lambda b, k: (b, vb)),
                  pl.BlockSpec((n_ctx, NA_WIDTH), lambda b, k: (b, kb)),
                  pl.BlockSpec((n_ctx, NA_WIDTH), lambda b, k: (b, vb)),
                  pl.BlockSpec((1,) + table.shape[1:], lambda b, k: (l, 0, 0, 0))],
        out_specs=pl.BlockSpec((nq, NA_WIDTH), lambda b, k: (b * nblk + k, 0)),
        out_shape=jax.ShapeDtypeStruct((bsz * seq, NA_WIDTH), BF16),
        compiler_params=_cparams(("parallel", "arbitrary")),
        name="neighborhood_attention",
    )(proj_lat, proj_lat, proj_lat, proj_ctx, proj_ctx, table)


def _ctx_attn_kernel(q_ref, k_ref, v_ref, o_ref):
    for hp in range(NA_HEADS // 2):
        ln = pl.ds(hp * LANES, LANES)
        y = _attend(q_ref[:, ln], [k_ref[:, ln]], [v_ref[:, ln]], [None])
        o_ref[:, ln] = y.astype(o_ref.dtype)


def _ctx_attn(proj_ctx, bsz, n_ctx):
    qb, kb, vb = (_MY_OFF[n] // NA_WIDTH for n in ('nq', 'nk', 'nv'))
    spec = lambda j: pl.BlockSpec((n_ctx, NA_WIDTH), lambda b: (b, j))
    return pl.pallas_call(
        _ctx_attn_kernel,
        grid=(bsz,),
        in_specs=[spec(qb), spec(kb), spec(vb)],
        out_specs=pl.BlockSpec((n_ctx, NA_WIDTH), lambda b: (b, 0)),
        out_shape=jax.ShapeDtypeStruct((bsz * n_ctx, NA_WIDTH), BF16),
        compiler_params=_cparams(("parallel",)),
        name="context_attention",
    )(proj_ctx, proj_ctx, proj_ctx)


def _gelu_tanh(x):
    return 0.5 * x * (1.0 + jnp.tanh(math.sqrt(2.0 / math.pi) * (x + 0.044715 * (x * x * x))))


def _merge_kernel(x_ref, g0_ref, g1_ref, g2_ref, ys5_ref, yret_ref, yna_ref, wglu_ref, bglu_ref,
                  wbs5_ref, wbret_ref, wbna_ref, wout_ref, gate_ref, o_ref):
    ge = _gelu_tanh(ys5_ref[...].astype(F32))
    z = jnp.dot(ge.astype(BF16), wglu_ref[0], preferred_element_type=F32) + bglu_ref[0]
    s5 = (ge * _sigmoid(z)).astype(BF16)
    m = (_sigmoid(g0_ref[...].astype(F32)) * jnp.dot(s5, wbs5_ref[0], preferred_element_type=F32)
         + _sigmoid(g1_ref[...].astype(F32)) * jnp.dot(yret_ref[...], wbret_ref[0], preferred_element_type=F32)
         + _sigmoid(g2_ref[...].astype(F32)) * jnp.dot(yna_ref[...], wbna_ref[0], preferred_element_type=F32))
    o_ref[...] = x_ref[...] + gate_ref[0] * jnp.dot(m.astype(BF16), wout_ref[0], preferred_element_type=F32)


def _merge(x2, proj, ys5, yret, yna, mods, mod_row, l, wts, tm):
    m, d = x2.shape
    base = l * MOD_ROWS * 6
    rowblk = lambda w, j=0: pl.BlockSpec((tm, w), lambda i, j=j: (i, j))
    return pl.pallas_call(
        _merge_kernel,
        grid=(m // tm,),
        in_specs=[rowblk(d), rowblk(d, 0), rowblk(d, 1), rowblk(d, 2),
                  rowblk(S5_WIDTH), rowblk(RET_WIDTH), rowblk(NA_WIDTH)]
                 + [_layer_spec(w, l) for w in wts]
                 + [pl.BlockSpec((1, 1, d), lambda i: (base + mod_row(i) * 6 + 2, 0, 0))],
        out_specs=rowblk(d),
        out_shape=jax.ShapeDtypeStruct((m, d), F32),
        compiler_params=_cparams(("parallel",)),
        name="merge_residual",
    )(x2, proj, proj, proj, ys5, yret, yna, *wts, mods)


def _ffn_kernel(x_ref, sh_ref, sc_ref, gate_ref, wg_ref, wu_ref, wd_ref, fn_ref, o_ref, *, final, th):
    x = x_ref[...]
    h = (_rms(x) * (1.0 + sc_ref[0]) + sh_ref[0]).astype(BF16)
    hidden = wg_ref.shape[2]
    acc = jnp.zeros(x.shape, F32)
    for j in range(hidden // th):
        a = jnp.dot(h, wg_ref[0, :, j * th:(j + 1) * th], preferred_element_type=F32)
        b = jnp.dot(h, wu_ref[0, :, j * th:(j + 1) * th], preferred_element_type=F32)
        act = (a * _sigmoid(a) * b).astype(BF16)
        acc = acc + jnp.dot(act, wd_ref[0, j * th:(j + 1) * th, :], preferred_element_type=F32)
    y = x + gate_ref[0] * acc
    if final:
        y = _rms(y) * fn_ref[...]
    o_ref[...] = y


def _ffn(x2, mods, mod_row, l, wg, wu, wd, fn, tm, final):
    m, d = x2.shape
    base = l * MOD_ROWS * 6
    modspec = lambda k: pl.BlockSpec((1, 1, d), lambda i, k=k: (base + mod_row(i) * 6 + k, 0, 0))
    return pl.pallas_call(
        functools.partial(_ffn_kernel, final=final, th=256),
        grid=(m // tm,),
        in_specs=[pl.BlockSpec((tm, d), lambda i: (i, 0)), modspec(3), modspec(4), modspec(5),
                  _layer_spec(wg, l), _layer_spec(wu, l), _layer_spec(wd, l),
                  pl.BlockSpec(fn.shape, lambda i: (0, 0))],
        out_specs=pl.BlockSpec((tm, d), lambda i: (i, 0)),
        out_shape=jax.ShapeDtypeStruct((m, d), F32),
        compiler_params=_cparams(("parallel",)),
        name="swiglu_residual",
    )(x2, mods, mods, mods, wg, wu, wd, fn)


def kernel(x, c, ctx, c_ctx, w_ada, b_ada, w_in, s5_lam_re, s5_lam_im, s5_log_dt, s5_b_re, s5_b_im,
           s5_c_re, s5_c_im, s5_d, s5_w_glu, s5_b_glu, ret_theta, na_rpb, w_branch_s5, w_branch_ret,
           w_branch_na, w_out, w_ffn_gate, w_ffn_up, w_ffn_down, final_norm):
    bsz, seq, d = x.shape
    n_ctx = ctx.shape[1]
    depth = w_ada.shape[0]
    ctx_row = bsz
    assert bsz == SUBLANES and bsz + 1 <= MOD_ROWS

    cvec = jnp.zeros((MOD_ROWS, d), F32).at[:bsz].set(c).at[ctx_row].set(c_ctx)
    mods = _ada(cvec, w_ada, b_ada).reshape(depth * MOD_ROWS * 6, 1, d)

    w_in_k = jnp.concatenate(
        [w_in[:, :, _REF_OFF[n]:_REF_OFF[n] + _REF_W[n]] * _COL_SCALE.get(n, 1.0) for n in _MY_ORDER],
        axis=2).astype(BF16)

    cos, sin, pmat = _rotary_tables(seq)
    log_gamma = jax.nn.log_sigmoid(ret_theta.astype(F32)).reshape(depth, 2 * RET_HEADS)
    kin, vin, win, a_t = _s5_weights(s5_lam_re, s5_lam_im, s5_log_dt, s5_b_re, s5_b_im, s5_c_re, s5_c_im, s5_d)
    na_table = _na_bias_table(na_rpb)
    merge_w = (s5_w_glu.astype(BF16), s5_b_glu.reshape(depth, 1, -1).astype(F32), w_branch_s5.astype(BF16),
               w_branch_ret.astype(BF16), w_branch_na.astype(BF16), w_out.astype(BF16))
    ffn_w = (w_ffn_gate.astype(BF16), w_ffn_up.astype(BF16), w_ffn_down.astype(BF16),
             final_norm.reshape(1, d).astype(F32))

    tm_proj, tn_proj, tm = 1024, N_IN // 2, 512
    lat_row = lambda t: (lambda i: i // (seq // t))
    ctx_mod_row = lambda i: ctx_row
    ncc, ncl = n_ctx // S5_CHUNK, seq // S5_CHUNK

    x2 = x.reshape(bsz * seq, d)
    c2 = ctx.reshape(bsz * n_ctx, d)
    for l in range(depth):
        need_ctx = l < depth - 1
        proj_lat = _inproj(x2, mods, lat_row(tm_proj), w_in_k, l, tm_proj, tn_proj)
        proj_ctx = _inproj(c2, mods, ctx_mod_row, w_in_k, l, tm_proj, tn_proj)

        u_t = _to_chunked(proj_ctx.reshape(bsz, n_ctx, N_IN), proj_lat.reshape(bsz, seq, N_IN), bsz, n_ctx, seq)
        y_t = _s5(u_t, kin, vin, win, a_t, l, bsz, ncc, ncl)
        ys5_ctx, ys5_lat = _from_chunked(y_t, bsz, n_ctx, seq)

        yret_lat, yret_ctx = _retention(proj_lat, proj_ctx, log_gamma, cos, sin, pmat, l, bsz, seq, n_ctx, need_ctx)
        yna_lat = _na(proj_lat, proj_ctx, na_table, l, bsz, seq, n_ctx)

        x2 = _merge(x2, proj_lat, ys5_lat.reshape(bsz * seq, S5_WIDTH), yret_lat, yna_lat, mods, lat_row(tm), l, merge_w, tm)
        x2 = _ffn(x2, mods, lat_row(tm), l, *ffn_w, tm, final=not need_ctx)
        if need_ctx:
            yna_ctx = _ctx_attn(proj_ctx, bsz, n_ctx)
            c2 = _merge(c2, proj_ctx, ys5_ctx.reshape(bsz * n_ctx, S5_WIDTH), yret_ctx, yna_ctx, mods, ctx_mod_row, l,
                        merge_w, tm)
            c2 = _ffn(c2, mods, ctx_mod_row, l, *ffn_w, tm, final=False)
    return x2.reshape(bsz, seq, d)
```

```python
import functools
import math

import numpy as np
import jax
import jax.numpy as jnp
from jax import lax
from jax.experimental import pallas as pl
from jax.experimental.pallas import tpu as pltpu

F32 = jnp.float32
BF16 = jnp.bfloat16
HIGHEST = lax.Precision.HIGHEST

D_MODEL = 1024
GRID_W = 64
S5_WIDTH = 512
S5_GROUP = 16
S5_GROUPS = S5_WIDTH // S5_GROUP
S5_STATE = 64
S5_CHUNK = 16
RET_HEADS = 4
RET_DK = 64
RET_DV = 128
RET_QK_WIDTH = RET_HEADS * RET_DK
RET_WIDTH = RET_HEADS * RET_DV
RET_T = 256
NA_HEADS = 8
NA_HEAD_DIM = 64
NA_WIDTH = NA_HEADS * NA_HEAD_DIM
NA_ROWS = 8
NA_COLS = 16
NA_QROWS = 4
NA_KROWS = 12
NA_NLAG = 2 * NA_ROWS
N_BRANCH = 3
ROPE_BASE = 10000.0
RMS_EPS = 1e-6
GN_EPS = 1e-5
NEG_INF = -1e30
LANES = 128
SUBLANES = 8
MOD_ROWS = 16

_REF_SPLIT = (S5_WIDTH, RET_QK_WIDTH, RET_WIDTH, NA_WIDTH, NA_WIDTH,
              RET_QK_WIDTH, RET_WIDTH, NA_WIDTH, N_BRANCH * D_MODEL)
_REF_NAMES = ('u', 'rk', 'rv', 'nk', 'nv', 'rq', 'rg', 'nq', 'gates')
_REF_OFF = dict(zip(_REF_NAMES, np.concatenate([[0], np.cumsum(_REF_SPLIT)[:-1]]).tolist()))
_REF_W = dict(zip(_REF_NAMES, _REF_SPLIT))
_MY_ORDER = ('gates', 'rk', 'rq', 'u', 'rv', 'nk', 'nv', 'rg', 'nq')
_MY_OFF = {}
_o = 0
for _n in _MY_ORDER:
    _MY_OFF[_n] = _o
    _o += _REF_W[_n]
N_IN = _o
LOG2E = math.log2(math.e)
_COL_SCALE = {'nq': NA_HEAD_DIM ** -0.5 * LOG2E, 'rk': RET_DK ** -0.5}

VMEM_LIMIT = 56 * 1024 * 1024


def _cparams(sem):
    return pltpu.CompilerParams(dimension_semantics=sem, vmem_limit_bytes=VMEM_LIMIT)


def _sigmoid(x):
    return 1.0 / (1.0 + jnp.exp(-x))


def _rms(x):
    return x * lax.rsqrt(jnp.mean(x * x, axis=-1, keepdims=True) + RMS_EPS)


def _layer_spec(arr, l):
    nd = arr.ndim
    return pl.BlockSpec((1,) + arr.shape[1:], lambda *_: (l,) + (0,) * (nd - 1))


def _ada_kernel(c_ref, w_ref, b_ref, o_ref):
    c = c_ref[...]
    s = c * _sigmoid(c)
    o_ref[0] = jnp.dot(s, w_ref[0], preferred_element_type=F32, precision=HIGHEST) + b_ref[0]


def _ada(cvec, w_ada, b_ada):
    depth, d, n = w_ada.shape
    tn = 1536
    rows = cvec.shape[0]
    return pl.pallas_call(
        _ada_kernel,
        grid=(depth, n // tn),
        in_specs=[pl.BlockSpec((rows, d), lambda l, j: (0, 0)),
                  pl.BlockSpec((1, d, tn), lambda l, j: (l, 0, j)),
                  pl.BlockSpec((1, 1, tn), lambda l, j: (l, 0, j))],
        out_specs=pl.BlockSpec((1, rows, tn), lambda l, j: (l, 0, j)),
        out_shape=jax.ShapeDtypeStruct((depth, rows, n), F32),
        compiler_params=_cparams(("parallel", "parallel")),
        name="ada_mod",
    )(cvec, w_ada, b_ada.reshape(depth, 1, n))


_GPL = LANES // S5_GROUP
_NSLAB = S5_WIDTH // LANES


def _block_transpose(tiles):
    blk = lax.broadcasted_iota(jnp.int32, tiles[0].shape, 1) // S5_GROUP
    tiles = list(tiles)
    s = _GPL // 2
    while s >= 1:
        hi = (blk & s) != 0
        for a in range(_GPL):
            if a & s:
                continue
            b = a + s
            ta, tb = tiles[a], tiles[b]
            tiles[a] = jnp.where(hi, pltpu.roll(tb, s * S5_GROUP, 1), ta)
            tiles[b] = jnp.where(hi, tb, pltpu.roll(ta, LANES - s * S5_GROUP, 1))
        s //= 2
    return tiles


def _inproj_kernel(x_ref, sh_ref, sc_ref, w_ref, o_ref, ut_ref, h_ref, u_scr, *, u_off):
    @pl.when(pl.program_id(1) == 0)
    def _():
        h = _rms(x_ref[...]) * (1.0 + sc_ref[0]) + sh_ref[0]
        h_ref[...] = h.astype(BF16)

    res = jnp.dot(h_ref[...], w_ref[0], preferred_element_type=F32)
    o_ref[...] = res.astype(o_ref.dtype)
    for j in range(_NSLAB):
        u_scr[j] = res[:, u_off + j * LANES:u_off + (j + 1) * LANES]
    nrow = u_scr.shape[1] // S5_CHUNK
    for j in range(_NSLAB):
        for half in range(2):
            v = [u_scr[j, pl.ds(half * _GPL + tl, nrow, stride=S5_CHUNK), :] for tl in range(_GPL)]
            for q, out in enumerate(_block_transpose(v)):
                ut_ref[j * _GPL + q, :, half * LANES:(half + 1) * LANES] = out.astype(ut_ref.dtype)


def _inproj(x2, mods, mod_row, w, l, tm, tn):
    m, d = x2.shape
    n = w.shape[2]
    base = l * MOD_ROWS * 6
    nj = n // tn
    u_off = _MY_OFF['u'] - (nj - 1) * tn
    assert 0 <= u_off and u_off + S5_WIDTH <= tn
    return pl.pallas_call(
        functools.partial(_inproj_kernel, u_off=u_off),
        grid=(m // tm, nj),
        in_specs=[pl.BlockSpec((tm, d), lambda i, j: (i, 0)),
                  pl.BlockSpec((1, 1, d), lambda i, j: (base + mod_row(i) * 6 + 0, 0, 0)),
                  pl.BlockSpec((1, 1, d), lambda i, j: (base + mod_row(i) * 6 + 1, 0, 0)),
                  pl.BlockSpec((1, d, tn), lambda i, j: (l, 0, j))],
        out_specs=[pl.BlockSpec((tm, tn), lambda i, j: (i, j)),
                   pl.BlockSpec((S5_GROUPS, tm // S5_CHUNK, S5_CHUNK * S5_GROUP), lambda i, j: (0, i, 0))],
        out_shape=[jax.ShapeDtypeStruct((m, n), BF16),
                   jax.ShapeDtypeStruct((S5_GROUPS, m // S5_CHUNK, S5_CHUNK * S5_GROUP), BF16)],
        scratch_shapes=[pltpu.VMEM((tm, d), BF16), pltpu.VMEM((_NSLAB, tm, LANES), F32)],
        compiler_params=_cparams(("parallel", "arbitrary")),
        name="in_proj",
    )(x2, mods, mods, w)


def _s5w_kernel(lam_ref, btr_ref, bti_ref, ctr_ref, cti_ref, dd_ref, kin_ref, vin_ref, win_ref, at_ref, *, ng):
    t_n, h_n = S5_CHUNK, S5_GROUP
    width = t_n * h_n
    lane = lax.broadcasted_iota(jnp.int32, (1, LANES), 1)
    f_lane = lane < S5_STATE
    tau = lax.broadcasted_iota(jnp.int32, (3 * SUBLANES, LANES), 0).astype(F32)
    lane_w = lax.broadcasted_iota(jnp.int32, (h_n, width), 1)
    for g in range(ng):
        lam_re, lam_im, dt = lam_ref[g, 0:1, :], lam_ref[g, 1:2, :], lam_ref[g, 2:3, :]
        mag = jnp.exp(tau * (lam_re * dt))
        ang = tau * (lam_im * dt)
        pr, pi = mag * jnp.cos(ang), mag * jnp.sin(ang)
        ab_re, ab_im = pr[1:2], pi[1:2]
        den = lam_re * lam_re + lam_im * lam_im
        f_re = ((ab_re - 1.0) * lam_re + ab_im * lam_im) / den
        f_im = (ab_im * lam_re - (ab_re - 1.0) * lam_im) / den
        btr, bti = btr_ref[g], bti_ref[g]
        bbr = f_re * btr - f_im * bti
        bbi = f_re * bti + f_im * btr
        ctr, cti = ctr_ref[g], cti_ref[g]

        def powers(pf, pb):
            rr = [jnp.broadcast_to(jnp.where(f_lane, pr[pf[t]:pf[t] + 1], pr[pb[t]:pb[t] + 1]), (h_n, LANES))
                  for t in range(t_n)]
            ri = [jnp.broadcast_to(jnp.where(f_lane, pi[pf[t]:pf[t] + 1], pi[pb[t]:pb[t] + 1]), (h_n, LANES))
                  for t in range(t_n)]
            return jnp.concatenate(rr, axis=0), jnp.concatenate(ri, axis=0)

        tile = lambda a: jnp.concatenate([a] * t_n, axis=0)
        bbr_t, bbi_t, ctr_t, cti_t = tile(bbr), tile(bbi), tile(ctr), tile(cti)

        xr, xi = powers([t_n - 1 - t for t in range(t_n)], list(range(t_n)))
        vin = jnp.concatenate([xr * bbr_t - xi * bbi_t, xr * bbi_t + xi * bbr_t], axis=1)
        vin_ref[g] = vin.astype(vin_ref.dtype)

        yr, yi = powers([t + 1 for t in range(t_n)], [t_n - t for t in range(t_n)])
        win_t = jnp.concatenate([ctr_t * yr - cti_t * yi, -(ctr_t * yi + cti_t * yr)], axis=1)
        win_ref[g] = win_t.T.astype(win_ref.dtype)

        zr, zi = powers(list(range(t_n)), [t_n - 1 - t for t in range(t_n)])
        fmat = jnp.concatenate([ctr_t * zr - cti_t * zi, ctr_t * zi + cti_t * zr], axis=1)
        dn = (((1,), (1,)), ((), ()))
        lhs_f = jnp.concatenate([jnp.where(f_lane, bbr, 0.0), jnp.where(f_lane, -bbi, 0.0)], axis=1)
        lhs_b = jnp.concatenate([jnp.where(f_lane, 0.0, bbr), jnp.where(f_lane, 0.0, -bbi)], axis=1)
        w_f = lax.dot_general(lhs_f, fmat, dn, preferred_element_type=F32, precision=HIGHEST)
        w_b = lax.dot_general(lhs_b, fmat, dn, preferred_element_type=F32, precision=HIGHEST)
        w_f = w_f + jnp.concatenate([dd_ref[g], jnp.zeros((h_n, width - LANES), F32)], axis=1)
        blocks = []
        for t in range(t_n):
            sh_f = t * h_n
            sh_b = (width - (t_n - 1 - t) * h_n) % width
            fw = w_f if sh_f == 0 else pltpu.roll(w_f, sh_f, 1)
            bw = w_b if sh_b == 0 else pltpu.roll(w_b, sh_b, 1)
            blocks.append(jnp.where(lane_w >= t * h_n, fw, 0.0) + jnp.where(lane_w < (t + 1) * h_n, bw, 0.0))
        kin_ref[g] = jnp.concatenate(blocks, axis=0).astype(kin_ref.dtype)
        at_ref[g] = jnp.broadcast_to(jnp.concatenate([pr[t_n:t_n + 1], pi[t_n:t_n + 1]], axis=1),
                                     (SUBLANES, 2 * LANES))


def _s5_weights(lam_re, lam_im, log_dt, b_re, b_im, c_re, c_im, d_skip, ng=4):
    depth, _, g_n, p_n = lam_re.shape
    h_n = S5_GROUP
    n = depth * g_n
    width = S5_CHUNK * h_n
    pair = lambda a: a.astype(F32).transpose(0, 2, 1, 3).reshape(n, 1, 2 * p_n)
    dt = jnp.broadcast_to(jnp.exp(log_dt.astype(F32))[..., None], lam_re.shape)
    lam = jnp.concatenate([pair(lam_re), pair(lam_im), pair(dt), jnp.zeros((n, SUBLANES - 3, 2 * p_n), F32)], axis=1)
    bt = lambda a: a.astype(F32).transpose(0, 2, 4, 1, 3).reshape(n, h_n, 2 * p_n)
    ct = lambda a: a.astype(F32).transpose(0, 2, 3, 1, 4).reshape(n, h_n, 2 * p_n)
    dd = jnp.eye(h_n, LANES, dtype=F32)[None] * d_skip.astype(F32).reshape(n, h_n, 1)
    small = pl.BlockSpec((ng, h_n, LANES), lambda i: (i, 0, 0))
    big = pl.BlockSpec((ng, width, width), lambda i: (i, 0, 0))
    return pl.pallas_call(
        functools.partial(_s5w_kernel, ng=ng),
        grid=(n // ng,),
        in_specs=[pl.BlockSpec((ng, SUBLANES, LANES), lambda i: (i, 0, 0)), small, small, small, small, small],
        out_specs=[big, big, big, pl.BlockSpec((ng, SUBLANES, width), lambda i: (i, 0, 0))],
        out_shape=[jax.ShapeDtypeStruct((n, width, width), BF16)] * 3
                  + [jax.ShapeDtypeStruct((n, SUBLANES, width), F32)],
        compiler_params=_cparams(("parallel",)),
        name="s5_weights",
    )(lam, bt(b_re), bt(b_im), ct(c_re), ct(c_im), dd)


def _s5_pitch(n):
    p = -(-n // SUBLANES)
    return (p | 1) * SUBLANES


def _s5_kernel(uc_ref, ul_ref, kin_ref, vin_ref, win_ref, a_ref, yc_ref, yl_ref,
               sc_scr, sl_scr, xac_scr, xal_scr, xbc_scr, xbl_scr, *, ng, ncc, ncl, bsz):
    half = 2 * S5_STATE
    pc, plat = _s5_pitch(ncc), _s5_pitch(ncl)
    segs = ((uc_ref, yc_ref, sc_scr, xac_scr, xbc_scr, ncc, pc), (ul_ref, yl_ref, sl_scr, xal_scr, xbl_scr, ncl, plat))
    for u_ref, _, s_scr, _, _, n, pitch in segs:
        for g in range(ng):
            s = jnp.dot(u_ref[g], vin_ref[g], preferred_element_type=F32)
            for b in range(bsz):
                for k in range(2):
                    s_scr[g, k, pl.ds(b * pitch, n), :] = s[b * n:(b + 1) * n, k * half:(k + 1) * half]
    lane = lax.broadcasted_iota(jnp.int32, (bsz, half), 1)
    fwd_lane = lane < S5_STATE
    a_re = [a_ref[g, :, :half] for g in range(ng)]
    a_im = [a_ref[g, :, half:] for g in range(ng)]

    def make_step(s_scr, xa_scr, xb_scr, pitch):
        def step(fc, bc, xs):
            rf = pl.ds(fc, bsz, stride=pitch)
            rb = pl.ds(bc, bsz, stride=pitch)
            out = []
            for g in range(ng):
                xr, xi = xs[2 * g], xs[2 * g + 1]
                xa_scr[g, 0, rf, :] = xr
                xa_scr[g, 1, rf, :] = xi
                xb_scr[g, 0, rb, :] = xr
                xb_scr[g, 1, rb, :] = xi
                sr = jnp.where(fwd_lane, s_scr[g, 0, rf, :], s_scr[g, 0, rb, :])
                si = jnp.where(fwd_lane, s_scr[g, 1, rf, :], s_scr[g, 1, rb, :])
                out.append(a_re[g] * xr - a_im[g] * xi + sr)
                out.append(a_re[g] * xi + a_im[g] * xr + si)
            return tuple(out)
        return step

    step_c = make_step(sc_scr, xac_scr, xbc_scr, pc)
    step_l = make_step(sl_scr, xal_scr, xbl_scr, plat)
    xs = tuple(jnp.zeros((bsz, half), F32) for _ in range(2 * ng))
    xs = lax.fori_loop(0, ncc, lambda i, c: step_c(i, ncc - 1 - i, c), xs)
    xs = lax.fori_loop(0, ncl, lambda i, c: step_l(i, ncl - 1 - i, c), xs)
    for u_ref, y_ref, _, xa_scr, xb_scr, n, pitch in segs:
        fwd_r = lax.broadcasted_iota(jnp.int32, (n, half), 1) < S5_STATE
        for g in range(ng):
            rows = []
            for b in range(bsz):
                r = pl.ds(b * pitch, n)
                rows.append(jnp.concatenate([jnp.where(fwd_r, xa_scr[g, k, r, :], xb_scr[g, k, r, :]) for k in range(2)],
                                            axis=1))
            x_in = jnp.concatenate(rows, axis=0).astype(BF16)
            y = (jnp.dot(u_ref[g], kin_ref[g], preferred_element_type=F32)
                 + jnp.dot(x_in, win_ref[g], preferred_element_type=F32))
            y_ref[g] = y.astype(y_ref.dtype)


def _s5(u_ctx, u_lat, kin, vin, win, a_t, l, bsz, ng=4):
    g_n, rc, w = u_ctx.shape
    rl = u_lat.shape[1]
    ncc, ncl = rc // bsz, rl // bsz
    nblk = g_n // ng
    wspec = pl.BlockSpec((ng, w, w), lambda i: (l * nblk + i, 0, 0))
    uspec = lambda r: pl.BlockSpec((ng, r, w), lambda i: (i, 0, 0))
    scr = lambda n: pltpu.VMEM((ng, 2, bsz * _s5_pitch(n), LANES), F32)
    return pl.pallas_call(
        functools.partial(_s5_kernel, ng=ng, ncc=ncc, ncl=ncl, bsz=bsz),
        grid=(nblk,),
        in_specs=[uspec(rc), uspec(rl), wspec, wspec, wspec,
                  pl.BlockSpec((ng, SUBLANES, w), lambda i: (l * nblk + i, 0, 0))],
        out_specs=[uspec(rc), uspec(rl)],
        out_shape=[jax.ShapeDtypeStruct(u_ctx.shape, BF16), jax.ShapeDtypeStruct(u_lat.shape, BF16)],
        scratch_shapes=[scr(ncc), scr(ncl)] * 3,
        compiler_params=_cparams(("parallel",)),
        name="s5_mixer",
    )(u_ctx, u_lat, kin, vin, win, a_t)


def _rotary_tables(seq):
    quarter = RET_DK // 4
    pos = jnp.arange(seq)
    inv_freq = ROPE_BASE ** (-jnp.arange(quarter, dtype=F32) / quarter)
    ang_r = (pos // GRID_W).astype(F32)[:, None] * inv_freq[None, :]
    ang_c = (pos % GRID_W).astype(F32)[:, None] * inv_freq[None, :]
    cos = jnp.concatenate([jnp.cos(ang_r)] * 2 + [jnp.cos(ang_c)] * 2, axis=-1)
    sin = jnp.concatenate([jnp.sin(ang_r)] * 2 + [jnp.sin(ang_c)] * 2, axis=-1)
    cos = jnp.tile(cos, (1, RET_HEADS))
    sin = jnp.tile(sin, (1, RET_HEADS))
    p = np.zeros((RET_QK_WIDTH, RET_QK_WIDTH), np.float32)
    for d in range(RET_QK_WIDTH):
        if d % (2 * quarter) < quarter:
            p[d + quarter, d] = -1.0
        else:
            p[d - quarter, d] = 1.0
    return cos, sin, jnp.asarray(p, BF16)


def _ret_kernel(lg_ref, q_ref, k_ref, v_ref, g_ref, cq_ref, ck_ref, cv_ref, cg_ref,
                cos_ref, sin_ref, p_ref, o_ref, co_ref, krot_scr, sin_scr, dm_scr, *, layer, need_ctx, ncl):
    t = RET_T
    qkw, vw = RET_QK_WIDTH, RET_WIDTH

    def per_head(shape, axis, width, d):
        head = lax.broadcasted_iota(jnp.int32, shape, axis) // width
        out = jnp.zeros(shape, F32)
        for h in range(RET_HEADS):
            out = jnp.where(head == h, lg_ref[layer, d * RET_HEADS + h], out)
        return out

    row = lax.broadcasted_iota(jnp.int32, (t, qkw), 0).astype(F32)
    lgf = per_head((t, qkw), 1, RET_DK, 0)
    lgb = per_head((t, qkw), 1, RET_DK, 1)
    qdec_f = jnp.exp((row + 1.0) * lgf)
    qdec_b = jnp.exp((t - row) * lgb)
    kdec_f = jnp.exp((t - 1.0 - row) * lgf)
    kdec_b = jnp.exp(row * lgb)
    cdec_f = jnp.exp(float(t) * per_head((qkw, vw), 0, RET_DK, 0))
    cdec_b = jnp.exp(float(t) * per_head((qkw, vw), 0, RET_DK, 1))
    blk = (lax.broadcasted_iota(jnp.int32, (qkw, vw), 0) // RET_DK
           == lax.broadcasted_iota(jnp.int32, (qkw, vw), 1) // RET_DV)
    head_lane = lax.broadcasted_iota(jnp.int32, (t, qkw), 1) // RET_DK

    ii = lax.broadcasted_iota(jnp.int32, (t, t), 0)
    jj = lax.broadcasted_iota(jnp.int32, (t, t), 1)
    dif = (ii - jj).astype(F32)
    for h in range(RET_HEADS):
        df = jnp.where(dif >= 0, jnp.exp(jnp.where(dif >= 0, dif, 0.0) * lg_ref[layer, h]), 0.0)
        db = jnp.where(dif < 0, jnp.exp(jnp.where(dif < 0, -dif, 0.0) * lg_ref[layer, RET_HEADS + h]), 0.0)
        dm_scr[h] = df + db

    def rotary(x_bf, c):
        rows = pl.ds(c * t, t)
        swapped = jnp.dot(x_bf, p_ref[...], preferred_element_type=F32)
        return x_bf.astype(F32) * cos_ref[rows, :] + swapped * sin_ref[rows, :]

    krot_scr[pl.ds(0, t), :] = ck_ref[...]
    for c in range(ncl):
        krot_scr[pl.ds((c + 1) * t, t), :] = rotary(k_ref[pl.ds(c * t, t), :], c).astype(BF16)

    def v_chunk(c):
        return cv_ref[...] if c == 0 else v_ref[pl.ds((c - 1) * t, t), :]

    def kv(c, kdec):
        kd = (krot_scr[pl.ds(c * t, t), :].astype(F32) * kdec).astype(BF16)
        return lax.dot_general(kd, v_chunk(c), (((0,), (0,)), ((), ())), preferred_element_type=F32)

    s = jnp.zeros((qkw, vw), F32)
    for c in range(ncl + 1):
        sin_scr[c, pl.ds(0, qkw), :] = jnp.where(blk, s, 0.0).astype(BF16)
        if c < ncl:
            s = cdec_f * s + kv(c, kdec_f)
    sin_scr[0, pl.ds(qkw, qkw), :] = jnp.zeros((qkw, vw), BF16)
    s = kv(0, kdec_b)
    for c in range(ncl, 0, -1):
        sin_scr[c, pl.ds(qkw, qkw), :] = jnp.where(blk, s, 0.0).astype(BF16)
        if c > 1:
            s = cdec_b * s + kv(c, kdec_b)

    for c in range(0 if need_ctx else 1, ncl + 1):
        if c == 0:
            q = cq_ref[...].astype(F32)
            gate = cg_ref[...].astype(F32)
        else:
            q = rotary(q_ref[pl.ds((c - 1) * t, t), :], c - 1)
            gate = g_ref[pl.ds((c - 1) * t, t), :].astype(F32)
        q_bf = q.astype(BF16)
        cross = (jnp.dot((q * qdec_f).astype(BF16), sin_scr[c, pl.ds(0, qkw), :], preferred_element_type=F32)
                 + jnp.dot((q * qdec_b).astype(BF16), sin_scr[c, pl.ds(qkw, qkw), :], preferred_element_type=F32))
        k_c = krot_scr[pl.ds(c * t, t), :]
        v_c = v_chunk(c)
        outs = []
        for h in range(RET_HEADS):
            qm = jnp.where(head_lane == h, q_bf, jnp.zeros_like(q_bf))
            sc = lax.dot_general(qm, k_c, (((1,), (1,)), ((), ())), preferred_element_type=F32)
            sc = (sc * dm_scr[h]).astype(BF16)
            o = (jnp.dot(sc, v_c[:, h * RET_DV:(h + 1) * RET_DV], preferred_element_type=F32)
                 + cross[:, h * RET_DV:(h + 1) * RET_DV])
            mu = jnp.mean(o, axis=-1, keepdims=True)
            var = jnp.mean(jnp.square(o - mu), axis=-1, keepdims=True)
            outs.append((o - mu) * lax.rsqrt(var + GN_EPS))
        y = gate * _sigmoid(gate) * jnp.concatenate(outs, axis=1)
        if c == 0:
            co_ref[...] = y.astype(co_ref.dtype)
        else:
            o_ref[pl.ds((c - 1) * t, t), :] = y.astype(o_ref.dtype)
    if not need_ctx:
        co_ref[...] = jnp.zeros(co_ref.shape, co_ref.dtype)


def _retention(proj_lat, proj_ctx, lg, cos, sin, pmat, l, bsz, seq, n_ctx, need_ctx):
    t = RET_T
    ncl = seq // t
    assert n_ctx == t
    qb, kb = _MY_OFF['rq'] // RET_QK_WIDTH, _MY_OFF['rk'] // RET_QK_WIDTH
    vb, gb = _MY_OFF['rv'] // RET_WIDTH, _MY_OFF['rg'] // RET_WIDTH

    def col(n, w, j):
        return pl.BlockSpec((n, w), lambda b, j=j: (b, j))

    const = lambda shape: pl.BlockSpec(shape, lambda b: (0,) * len(shape))
    return pl.pallas_call(
        functools.partial(_ret_kernel, layer=l, need_ctx=need_ctx, ncl=ncl),
        grid=(bsz,),
        in_specs=[pl.BlockSpec(memory_space=pltpu.SMEM),
                  col(seq, RET_QK_WIDTH, qb), col(seq, RET_QK_WIDTH, kb), col(seq, RET_WIDTH, vb), col(seq, RET_WIDTH, gb),
                  col(n_ctx, RET_QK_WIDTH, qb), col(n_ctx, RET_QK_WIDTH, kb), col(n_ctx, RET_WIDTH, vb), col(n_ctx, RET_WIDTH, gb),
                  const((seq, RET_QK_WIDTH)), const((seq, RET_QK_WIDTH)), const((RET_QK_WIDTH, RET_QK_WIDTH))],
        out_specs=[pl.BlockSpec((seq, RET_WIDTH), lambda b: (b, 0)),
                   pl.BlockSpec((n_ctx, RET_WIDTH), lambda b: (b, 0))],
        out_shape=[jax.ShapeDtypeStruct((bsz * seq, RET_WIDTH), BF16),
                   jax.ShapeDtypeStruct((bsz * n_ctx, RET_WIDTH), BF16)],
        scratch_shapes=[pltpu.VMEM((seq + n_ctx, RET_QK_WIDTH), BF16),
                        pltpu.VMEM((ncl + 1, 2 * RET_QK_WIDTH, RET_WIDTH), BF16),
                        pltpu.VMEM((RET_HEADS, t, t), F32)],
        compiler_params=_cparams(("parallel",)),
        name="retention",
    )(lg, proj_lat, proj_lat, proj_lat, proj_lat, proj_ctx, proj_ctx, proj_ctx, proj_ctx, cos, sin, pmat)


def _na_block_start(kblk, rows):
    return jnp.clip(kblk * NA_QROWS - NA_ROWS // 2, 0, rows - NA_KROWS)


def _na_bias_table(rpb):
    depth, heads, nlag, ncol = rpb.shape
    cols = np.arange(GRID_W)
    cs = np.clip(cols - NA_COLS // 2, 0, GRID_W - NA_COLS)
    valid_c = (cols[None, :] >= cs[:, None]) & (cols[None, :] < cs[:, None] + NA_COLS)
    pad = GRID_W - 1
    padded = jnp.pad(rpb.astype(F32) * LOG2E, ((0, 0), (0, 0), (0, 0), (pad, pad)))
    toe = jnp.stack([padded[..., pad + NA_COLS - 1 - qc: pad + NA_COLS - 1 - qc + GRID_W] for qc in range(GRID_W)],
                    axis=-2)
    toe = jnp.where(jnp.asarray(valid_c), toe, NEG_INF)
    toe = jnp.pad(toe, ((0, 0), (0, 0), (1, NA_NLAG + 1 - nlag - 1), (0, 0), (0, 0)))
    table = jnp.concatenate([toe[:, :, :NA_NLAG], toe[:, :, 1:NA_NLAG + 1]], axis=-1)
    return table.reshape(depth, heads * NA_NLAG, GRID_W, 2 * GRID_W)


def _attend(q_pair, k_list, v_list, bias_list):
    lane = lax.broadcasted_iota(jnp.int32, q_pair.shape, 1) // NA_HEAD_DIM
    outs = []
    for hh in range(2):
        qm = jnp.where(lane == hh, q_pair, jnp.zeros_like(q_pair))
        ss = []
        for k_i, b_i in zip(k_list, bias_list):
            s = lax.dot_general(qm, k_i, (((1,), (1,)), ((), ())), preferred_element_type=F32)
            if b_i is not None:
                s = s + b_i(hh)
            ss.append(s)
        m = ss[0].max(axis=-1, keepdims=True)
        for s in ss[1:]:
            m = jnp.maximum(m, s.max(axis=-1, keepdims=True))
        ps = [jnp.exp2(s - m) for s in ss]
        den = ps[0].sum(axis=-1, keepdims=True)
        for p in ps[1:]:
            den = den + p.sum(axis=-1, keepdims=True)
        acc = jnp.dot(ps[0].astype(BF16), v_list[0], preferred_element_type=F32)
        for p, v_i in zip(ps[1:], v_list[1:]):
            acc = acc + jnp.dot(p.astype(BF16), v_i, preferred_element_type=F32)
        outs.append(acc / den)
    return jnp.where(lane == 0, outs[0], outs[1])


def _na_kernel(q_ref, k_ref, v_ref, ck_ref, cv_ref, tab_ref, o_ref, *, rows):
    nq, nk = NA_QROWS * GRID_W, NA_KROWS * GRID_W
    kblk = pl.program_id(1)
    r0 = kblk * NA_QROWS
    ks_row = _na_block_start(kblk, rows)
    ks = pl.multiple_of(ks_row * GRID_W, NA_QROWS * GRID_W)
    qr = r0 + lax.broadcasted_iota(jnp.int32, (nq, nk), 0) // GRID_W
    kr = ks_row + lax.broadcasted_iota(jnp.int32, (nq, nk), 1) // GRID_W
    rs = jnp.clip(qr - NA_ROWS // 2, 0, rows - NA_ROWS)
    row_mask = jnp.where((kr >= rs) & (kr < rs + NA_ROWS), 0.0, NEG_INF)

    def bias(h):
        row_blocks = []
        for qrl in range(NA_QROWS):
            tiles = []
            for kp in range(NA_KROWS // 2):
                lag = ks_row - r0 + 2 * kp - qrl + NA_ROWS - 1
                idx = h * NA_NLAG + jnp.clip(lag, -1, NA_NLAG - 2) + 1
                tiles.append(tab_ref[0, idx])
            row_blocks.append(jnp.concatenate(tiles, axis=1))
        return jnp.concatenate(row_blocks, axis=0) + row_mask

    for hp in range(NA_HEADS // 2):
        ln = pl.ds(hp * LANES, LANES)
        y = _attend(q_ref[:, ln],
                    [k_ref[pl.ds(ks, nk), ln], ck_ref[:, ln]],
                    [v_ref[pl.ds(ks, nk), ln], cv_ref[:, ln]],
                    [lambda hh, hp=hp: bias(2 * hp + hh), None])
        o_ref[:, ln] = y.astype(o_ref.dtype)


def _na(proj_lat, proj_ctx, table, l, bsz, seq, n_ctx):
    rows = seq // GRID_W
    nq = NA_QROWS * GRID_W
    nblk = seq // nq
    qb, kb, vb = (_MY_OFF[n] // NA_WIDTH for n in ('nq', 'nk', 'nv'))
    return pl.pallas_call(
        functools.partial(_na_kernel, rows=rows),
        grid=(bsz, nblk),
        in_specs=[pl.BlockSpec((nq, NA_WIDTH), lambda b, k: (b * nblk + k, qb)),
                  pl.BlockSpec((seq, NA_WIDTH), lambda b, k: (b, kb)),
                  pl.BlockSpec((seq, NA_WIDTH), lambda b, k: (b, vb)),
                  pl.BlockSpec((n_ctx, NA_WIDTH), lambda b, k: (b, kb)),
                  pl.BlockSpec((n_ctx, NA_WIDTH), lambda b, k: (b, vb)),
                  pl.BlockSpec((1,) + table.shape[1:], lambda b, k: (l, 0, 0, 0))],
        out_specs=pl.BlockSpec((nq, NA_WIDTH), lambda b, k: (b * nblk + k, 0)),
        out_shape=jax.ShapeDtypeStruct((bsz * seq, NA_WIDTH), BF16),
        compiler_params=_cparams(("parallel", "arbitrary")),
        name="neighborhood_attention",
    )(proj_lat, proj_lat, proj_lat, proj_ctx, proj_ctx, table)


def _ctx_attn_kernel(q_ref, k_ref, v_ref, o_ref):
    for hp in range(NA_HEADS // 2):
        ln = pl.ds(hp * LANES, LANES)
        y = _attend(q_ref[:, ln], [k_ref[:, ln]], [v_ref[:, ln]], [None])
        o_ref[:, ln] = y.astype(o_ref.dtype)


def _ctx_attn(proj_ctx, bsz, n_ctx):
    qb, kb, vb = (_MY_OFF[n] // NA_WIDTH for n in ('nq', 'nk', 'nv'))
    spec = lambda j: pl.BlockSpec((n_ctx, NA_WIDTH), lambda b: (b, j))
    return pl.pallas_call(
        _ctx_attn_kernel,
        grid=(bsz,),
        in_specs=[spec(qb), spec(kb), spec(vb)],
        out_specs=pl.BlockSpec((n_ctx, NA_WIDTH), lambda b: (b, 0)),
        out_shape=jax.ShapeDtypeStruct((bsz * n_ctx, NA_WIDTH), BF16),
        compiler_params=_cparams(("parallel",)),
        name="context_attention",
    )(proj_ctx, proj_ctx, proj_ctx)


def _gelu_tanh(x):
    return 0.5 * x * (1.0 + jnp.tanh(math.sqrt(2.0 / math.pi) * (x + 0.044715 * (x * x * x))))


def _merge_kernel(x_ref, g0_ref, g1_ref, g2_ref, yt_ref, yret_ref, yna_ref, wglu_ref, bglu_ref,
                  wbs5_ref, wbret_ref, wbna_ref, wout_ref, gate_ref, o_ref, w_scr):
    nrow = yt_ref.shape[1]
    for j in range(_NSLAB):
        for half in range(2):
            o = [yt_ref[j * _GPL + q, :, half * LANES:(half + 1) * LANES].astype(F32) for q in range(_GPL)]
            for tl, out in enumerate(_block_transpose(o)):
                w_scr[j, pl.ds(half * _GPL + tl, nrow, stride=S5_CHUNK), :] = out
    ge = _gelu_tanh(jnp.concatenate([w_scr[j] for j in range(_NSLAB)], axis=1))
    z = jnp.dot(ge.astype(BF16), wglu_ref[0], preferred_element_type=F32) + bglu_ref[0]
    s5 = (ge * _sigmoid(z)).astype(BF16)
    m = (_sigmoid(g0_ref[...].astype(F32)) * jnp.dot(s5, wbs5_ref[0], preferred_element_type=F32)
         + _sigmoid(g1_ref[...].astype(F32)) * jnp.dot(yret_ref[...], wbret_ref[0], preferred_element_type=F32)
         + _sigmoid(g2_ref[...].astype(F32)) * jnp.dot(yna_ref[...], wbna_ref[0], preferred_element_type=F32))
    o_ref[...] = x_ref[...] + gate_ref[0] * jnp.dot(m.astype(BF16), wout_ref[0], preferred_element_type=F32)


def _merge(x2, proj, y_t, yret, yna, mods, mod_row, l, wts, tm):
    m, d = x2.shape
    base = l * MOD_ROWS * 6
    rowblk = lambda w, j=0: pl.BlockSpec((tm, w), lambda i, j=j: (i, j))
    return pl.pallas_call(
        _merge_kernel,
        grid=(m // tm,),
        in_specs=[rowblk(d), rowblk(d, 0), rowblk(d, 1), rowblk(d, 2),
                  pl.BlockSpec((S5_GROUPS, tm // S5_CHUNK, S5_CHUNK * S5_GROUP), lambda i: (0, i, 0)),
                  rowblk(RET_WIDTH), rowblk(NA_WIDTH)]
                 + [_layer_spec(w, l) for w in wts]
                 + [pl.BlockSpec((1, 1, d), lambda i: (base + mod_row(i) * 6 + 2, 0, 0))],
        out_specs=rowblk(d),
        out_shape=jax.ShapeDtypeStruct((m, d), F32),
        scratch_shapes=[pltpu.VMEM((_NSLAB, tm, LANES), F32)],
        compiler_params=_cparams(("parallel",)),
        name="merge_residual",
    )(x2, proj, proj, proj, y_t, yret, yna, *wts, mods)


def _ffn_kernel(x_ref, sh_ref, sc_ref, gate_ref, wg_ref, wu_ref, wd_ref, fn_ref, o_ref, *, final, th):
    x = x_ref[...]
    h = (_rms(x) * (1.0 + sc_ref[0]) + sh_ref[0]).astype(BF16)
    hidden = wg_ref.shape[2]
    acc = jnp.zeros(x.shape, F32)
    for j in range(hidden // th):
        a = jnp.dot(h, wg_ref[0, :, j * th:(j + 1) * th], preferred_element_type=F32)
        b = jnp.dot(h, wu_ref[0, :, j * th:(j + 1) * th], preferred_element_type=F32)
        act = (a * _sigmoid(a) * b).astype(BF16)
        acc = acc + jnp.dot(act, wd_ref[0, j * th:(j + 1) * th, :], preferred_element_type=F32)
    y = x + gate_ref[0] * acc
    if final:
        y = _rms(y) * fn_ref[...]
    o_ref[...] = y


def _ffn(x2, mods, mod_row, l, wg, wu, wd, fn, tm, final):
    m, d = x2.shape
    base = l * MOD_ROWS * 6
    modspec = lambda k: pl.BlockSpec((1, 1, d), lambda i, k=k: (base + mod_row(i) * 6 + k, 0, 0))
    return pl.pallas_call(
        functools.partial(_ffn_kernel, final=final, th=256),
        grid=(m // tm,),
        in_specs=[pl.BlockSpec((tm, d), lambda i: (i, 0)), modspec(3), modspec(4), modspec(5),
                  _layer_spec(wg, l), _layer_spec(wu, l), _layer_spec(wd, l),
                  pl.BlockSpec(fn.shape, lambda i: (0, 0))],
        out_specs=pl.BlockSpec((tm, d), lambda i: (i, 0)),
        out_shape=jax.ShapeDtypeStruct((m, d), F32),
        compiler_params=_cparams(("parallel",)),
        name="swiglu_residual",
    )(x2, mods, mods, mods, wg, wu, wd, fn)


def kernel(x, c, ctx, c_ctx, w_ada, b_ada, w_in, s5_lam_re, s5_lam_im, s5_log_dt, s5_b_re, s5_b_im,
           s5_c_re, s5_c_im, s5_d, s5_w_glu, s5_b_glu, ret_theta, na_rpb, w_branch_s5, w_branch_ret,
           w_branch_na, w_out, w_ffn_gate, w_ffn_up, w_ffn_down, final_norm):
    bsz, seq, d = x.shape
    n_ctx = ctx.shape[1]
    depth = w_ada.shape[0]
    ctx_row = bsz
    assert bsz == SUBLANES and bsz + 1 <= MOD_ROWS

    cvec = jnp.zeros((MOD_ROWS, d), F32).at[:bsz].set(c).at[ctx_row].set(c_ctx)
    mods = _ada(cvec, w_ada, b_ada).reshape(depth * MOD_ROWS * 6, 1, d)

    w_in_k = jnp.concatenate(
        [w_in[:, :, _REF_OFF[n]:_REF_OFF[n] + _REF_W[n]] * _COL_SCALE.get(n, 1.0) for n in _MY_ORDER],
        axis=2).astype(BF16)

    cos, sin, pmat = _rotary_tables(seq)
    log_gamma = jax.nn.log_sigmoid(ret_theta.astype(F32)).reshape(depth, 2 * RET_HEADS)
    kin, vin, win, a_t = _s5_weights(s5_lam_re, s5_lam_im, s5_log_dt, s5_b_re, s5_b_im, s5_c_re, s5_c_im, s5_d)
    na_table = _na_bias_table(na_rpb)
    merge_w = (s5_w_glu.astype(BF16), s5_b_glu.reshape(depth, 1, -1).astype(F32), w_branch_s5.astype(BF16),
               w_branch_ret.astype(BF16), w_branch_na.astype(BF16), w_out.astype(BF16))
    ffn_w = (w_ffn_gate.astype(BF16), w_ffn_up.astype(BF16), w_ffn_down.astype(BF16),
             final_norm.reshape(1, d).astype(F32))

    tm_proj, tn_proj, tm = 1024, N_IN // 2, 512
    lat_row = lambda t: (lambda i: i // (seq // t))
    ctx_mod_row = lambda i: ctx_row

    x2 = x.reshape(bsz * seq, d)
    c2 = ctx.reshape(bsz * n_ctx, d)
    for l in range(depth):
        need_ctx = l < depth - 1
        proj_lat, u_lat = _inproj(x2, mods, lat_row(tm_proj), w_in_k, l, tm_proj, tn_proj)
        proj_ctx, u_ctx = _inproj(c2, mods, ctx_mod_row, w_in_k, l, tm_proj, tn_proj)

        ys5_ctx, ys5_lat = _s5(u_ctx, u_lat, kin, vin, win, a_t, l, bsz)
        yret_lat, yret_ctx = _retention(proj_lat, proj_ctx, log_gamma, cos, sin, pmat, l, bsz, seq, n_ctx, need_ctx)
        yna_lat = _na(proj_lat, proj_ctx, na_table, l, bsz, seq, n_ctx)

        x2 = _merge(x2, proj_lat, ys5_lat, yret_lat, yna_lat, mods, lat_row(tm), l, merge_w, tm)
        x2 = _ffn(x2, mods, lat_row(tm), l, *ffn_w, tm, final=not need_ctx)
        if need_ctx:
            yna_ctx = _ctx_attn(proj_ctx, bsz, n_ctx)
            c2 = _merge(c2, proj_ctx, ys5_ctx, yret_ctx, yna_ctx, mods, ctx_mod_row, l, merge_w, tm)
            c2 = _ffn(c2, mods, ctx_mod_row, l, *ffn_w, tm, final=False)
    return x2.reshape(bsz, seq, d)
```

```python
import functools
import math

import numpy as np
import jax
import jax.numpy as jnp
from jax import lax
from jax.experimental import pallas as pl
from jax.experimental.pallas import tpu as pltpu

F32 = jnp.float32
BF16 = jnp.bfloat16
HIGHEST = lax.Precision.HIGHEST

D_MODEL = 1024
GRID_W = 64
S5_WIDTH = 512
S5_GROUP = 16
S5_GROUPS = S5_WIDTH // S5_GROUP
S5_STATE = 64
S5_CHUNK = 16
RET_HEADS = 4
RET_DK = 64
RET_DV = 128
RET_QK_WIDTH = RET_HEADS * RET_DK
RET_WIDTH = RET_HEADS * RET_DV
RET_T = 256
NA_HEADS = 8
NA_HEAD_DIM = 64
NA_WIDTH = NA_HEADS * NA_HEAD_DIM
NA_ROWS = 8
NA_COLS = 16
NA_QROWS = 4
NA_KROWS = 12
NA_NLAG = 2 * NA_ROWS
N_BRANCH = 3
ROPE_BASE = 10000.0
RMS_EPS = 1e-6
GN_EPS = 1e-5
NEG_INF = -1e30
LANES = 128
SUBLANES = 8
MOD_ROWS = 16

_REF_SPLIT = (S5_WIDTH, RET_QK_WIDTH, RET_WIDTH, NA_WIDTH, NA_WIDTH,
              RET_QK_WIDTH, RET_WIDTH, NA_WIDTH, N_BRANCH * D_MODEL)
_REF_NAMES = ('u', 'rk', 'rv', 'nk', 'nv', 'rq', 'rg', 'nq', 'gates')
_REF_OFF = dict(zip(_REF_NAMES, np.concatenate([[0], np.cumsum(_REF_SPLIT)[:-1]]).tolist()))
_REF_W = dict(zip(_REF_NAMES, _REF_SPLIT))
_MY_ORDER = ('gates', 'rk', 'rq', 'u', 'rv', 'nk', 'nv', 'rg', 'nq')
_MY_OFF = {}
_o = 0
for _n in _MY_ORDER:
    _MY_OFF[_n] = _o
    _o += _REF_W[_n]
N_IN = _o
LOG2E = math.log2(math.e)
_COL_SCALE = {'nq': NA_HEAD_DIM ** -0.5 * LOG2E, 'rk': RET_DK ** -0.5}

VMEM_LIMIT = 56 * 1024 * 1024


def _cparams(sem):
    return pltpu.CompilerParams(dimension_semantics=sem, vmem_limit_bytes=VMEM_LIMIT)


def _sigmoid(x):
    return 1.0 / (1.0 + jnp.exp(-x))


def _rms(x):
    return x * lax.rsqrt(jnp.mean(x * x, axis=-1, keepdims=True) + RMS_EPS)


def _layer_spec(arr, l):
    nd = arr.ndim
    return pl.BlockSpec((1,) + arr.shape[1:], lambda *_: (l,) + (0,) * (nd - 1))


def _ada_kernel(c_ref, w_ref, b_ref, o_ref):
    c = c_ref[...]
    s = c * _sigmoid(c)
    o_ref[0] = jnp.dot(s, w_ref[0], preferred_element_type=F32, precision=HIGHEST) + b_ref[0]


def _ada(cvec, w_ada, b_ada):
    depth, d, n = w_ada.shape
    tn = 1536
    rows = cvec.shape[0]
    return pl.pallas_call(
        _ada_kernel,
        grid=(depth, n // tn),
        in_specs=[pl.BlockSpec((rows, d), lambda l, j: (0, 0)),
                  pl.BlockSpec((1, d, tn), lambda l, j: (l, 0, j)),
                  pl.BlockSpec((1, 1, tn), lambda l, j: (l, 0, j))],
        out_specs=pl.BlockSpec((1, rows, tn), lambda l, j: (l, 0, j)),
        out_shape=jax.ShapeDtypeStruct((depth, rows, n), F32),
        compiler_params=_cparams(("parallel", "parallel")),
        name="ada_mod",
    )(cvec, w_ada, b_ada.reshape(depth, 1, n))


_GPL = LANES // S5_GROUP
_NSLAB = S5_WIDTH // LANES


def _block_transpose(tiles):
    blk = lax.broadcasted_iota(jnp.int32, tiles[0].shape, 1) // S5_GROUP
    tiles = list(tiles)
    s = _GPL // 2
    while s >= 1:
        hi = (blk & s) != 0
        for a in range(_GPL):
            if a & s:
                continue
            b = a + s
            ta, tb = tiles[a], tiles[b]
            tiles[a] = jnp.where(hi, pltpu.roll(tb, s * S5_GROUP, 1), ta)
            tiles[b] = jnp.where(hi, tb, pltpu.roll(ta, LANES - s * S5_GROUP, 1))
        s //= 2
    return tiles


def _inproj_kernel(x_ref, sh_ref, sc_ref, w_ref, o_ref, ut_ref, h_ref, u_scr, *, u_off):
    @pl.when(pl.program_id(1) == 0)
    def _():
        h = _rms(x_ref[...]) * (1.0 + sc_ref[0]) + sh_ref[0]
        h_ref[...] = h.astype(BF16)

    res = jnp.dot(h_ref[...], w_ref[0], preferred_element_type=F32)
    o_ref[...] = res.astype(o_ref.dtype)
    for j in range(_NSLAB):
        u_scr[j] = res[:, u_off + j * LANES:u_off + (j + 1) * LANES]
    nrow = u_scr.shape[1] // S5_CHUNK
    for j in range(_NSLAB):
        for half in range(2):
            v = [u_scr[j, pl.ds(half * _GPL + tl, nrow, stride=S5_CHUNK), :] for tl in range(_GPL)]
            for q, out in enumerate(_block_transpose(v)):
                ut_ref[j * _GPL + q, :, half * LANES:(half + 1) * LANES] = out.astype(ut_ref.dtype)


def _inproj(x2, mods, mod_row, w, l, tm, tn):
    m, d = x2.shape
    n = w.shape[2]
    base = l * MOD_ROWS * 6
    nj = n // tn
    u_off = _MY_OFF['u'] - (nj - 1) * tn
    assert 0 <= u_off and u_off + S5_WIDTH <= tn
    return pl.pallas_call(
        functools.partial(_inproj_kernel, u_off=u_off),
        grid=(m // tm, nj),
        in_specs=[pl.BlockSpec((tm, d), lambda i, j: (i, 0)),
                  pl.BlockSpec((1, 1, d), lambda i, j: (base + mod_row(i) * 6 + 0, 0, 0)),
                  pl.BlockSpec((1, 1, d), lambda i, j: (base + mod_row(i) * 6 + 1, 0, 0)),
                  pl.BlockSpec((1, d, tn), lambda i, j: (l, 0, j))],
        out_specs=[pl.BlockSpec((tm, tn), lambda i, j: (i, j)),
                   pl.BlockSpec((S5_GROUPS, tm // S5_CHUNK, S5_CHUNK * S5_GROUP), lambda i, j: (0, i, 0))],
        out_shape=[jax.ShapeDtypeStruct((m, n), BF16),
                   jax.ShapeDtypeStruct((S5_GROUPS, m // S5_CHUNK, S5_CHUNK * S5_GROUP), BF16)],
        scratch_shapes=[pltpu.VMEM((tm, d), BF16), pltpu.VMEM((_NSLAB, tm, LANES), F32)],
        compiler_params=_cparams(("parallel", "arbitrary")),
        name="in_proj",
    )(x2, mods, mods, w)


def _s5w_kernel(lam_ref, btr_ref, bti_ref, ctr_ref, cti_ref, dd_ref, kin_ref, vin_ref, win_ref, at_ref, *, ng):
    t_n, h_n = S5_CHUNK, S5_GROUP
    width = t_n * h_n
    lane = lax.broadcasted_iota(jnp.int32, (1, LANES), 1)
    f_lane = lane < S5_STATE
    tau = lax.broadcasted_iota(jnp.int32, (3 * SUBLANES, LANES), 0).astype(F32)
    lane_w = lax.broadcasted_iota(jnp.int32, (h_n, width), 1)
    for g in range(ng):
        lam_re, lam_im, dt = lam_ref[g, 0:1, :], lam_ref[g, 1:2, :], lam_ref[g, 2:3, :]
        mag = jnp.exp(tau * (lam_re * dt))
        ang = tau * (lam_im * dt)
        pr, pi = mag * jnp.cos(ang), mag * jnp.sin(ang)
        ab_re, ab_im = pr[1:2], pi[1:2]
        den = lam_re * lam_re + lam_im * lam_im
        f_re = ((ab_re - 1.0) * lam_re + ab_im * lam_im) / den
        f_im = (ab_im * lam_re - (ab_re - 1.0) * lam_im) / den
        btr, bti = btr_ref[g], bti_ref[g]
        bbr = f_re * btr - f_im * bti
        bbi = f_re * bti + f_im * btr
        ctr, cti = ctr_ref[g], cti_ref[g]

        def powers(pf, pb):
            rr = [jnp.broadcast_to(jnp.where(f_lane, pr[pf[t]:pf[t] + 1], pr[pb[t]:pb[t] + 1]), (h_n, LANES))
                  for t in range(t_n)]
            ri = [jnp.broadcast_to(jnp.where(f_lane, pi[pf[t]:pf[t] + 1], pi[pb[t]:pb[t] + 1]), (h_n, LANES))
                  for t in range(t_n)]
            return jnp.concatenate(rr, axis=0), jnp.concatenate(ri, axis=0)

        tile = lambda a: jnp.concatenate([a] * t_n, axis=0)
        bbr_t, bbi_t, ctr_t, cti_t = tile(bbr), tile(bbi), tile(ctr), tile(cti)

        xr, xi = powers([t_n - 1 - t for t in range(t_n)], list(range(t_n)))
        vin = jnp.concatenate([xr * bbr_t - xi * bbi_t, xr * bbi_t + xi * bbr_t], axis=1)
        vin_ref[g] = vin.astype(vin_ref.dtype)

        yr, yi = powers([t + 1 for t in range(t_n)], [t_n - t for t in range(t_n)])
        win_t = jnp.concatenate([ctr_t * yr - cti_t * yi, -(ctr_t * yi + cti_t * yr)], axis=1)
        win_ref[g] = win_t.T.astype(win_ref.dtype)

        zr, zi = powers(list(range(t_n)), [t_n - 1 - t for t in range(t_n)])
        fmat = jnp.concatenate([ctr_t * zr - cti_t * zi, ctr_t * zi + cti_t * zr], axis=1)
        dn = (((1,), (1,)), ((), ()))
        lhs_f = jnp.concatenate([jnp.where(f_lane, bbr, 0.0), jnp.where(f_lane, -bbi, 0.0)], axis=1)
        lhs_b = jnp.concatenate([jnp.where(f_lane, 0.0, bbr), jnp.where(f_lane, 0.0, -bbi)], axis=1)
        w_f = lax.dot_general(lhs_f, fmat, dn, preferred_element_type=F32, precision=HIGHEST)
        w_b = lax.dot_general(lhs_b, fmat, dn, preferred_element_type=F32, precision=HIGHEST)
        w_f = w_f + jnp.concatenate([dd_ref[g], jnp.zeros((h_n, width - LANES), F32)], axis=1)
        blocks = []
        for t in range(t_n):
            sh_f = t * h_n
            sh_b = (width - (t_n - 1 - t) * h_n) % width
            fw = w_f if sh_f == 0 else pltpu.roll(w_f, sh_f, 1)
            bw = w_b if sh_b == 0 else pltpu.roll(w_b, sh_b, 1)
            blocks.append(jnp.where(lane_w >= t * h_n, fw, 0.0) + jnp.where(lane_w < (t + 1) * h_n, bw, 0.0))
        kin_ref[g] = jnp.concatenate(blocks, axis=0).astype(kin_ref.dtype)
        at_ref[g] = jnp.broadcast_to(jnp.concatenate([pr[t_n:t_n + 1], pi[t_n:t_n + 1]], axis=1),
                                     (SUBLANES, 2 * LANES))


def _s5_weights(lam_re, lam_im, log_dt, b_re, b_im, c_re, c_im, d_skip, ng=4):
    depth, _, g_n, p_n = lam_re.shape
    h_n = S5_GROUP
    n = depth * g_n
    width = S5_CHUNK * h_n
    pair = lambda a: a.astype(F32).transpose(0, 2, 1, 3).reshape(n, 1, 2 * p_n)
    dt = jnp.broadcast_to(jnp.exp(log_dt.astype(F32))[..., None], lam_re.shape)
    lam = jnp.concatenate([pair(lam_re), pair(lam_im), pair(dt), jnp.zeros((n, SUBLANES - 3, 2 * p_n), F32)], axis=1)
    bt = lambda a: a.astype(F32).transpose(0, 2, 4, 1, 3).reshape(n, h_n, 2 * p_n)
    ct = lambda a: a.astype(F32).transpose(0, 2, 3, 1, 4).reshape(n, h_n, 2 * p_n)
    dd = jnp.eye(h_n, LANES, dtype=F32)[None] * d_skip.astype(F32).reshape(n, h_n, 1)
    small = pl.BlockSpec((ng, h_n, LANES), lambda i: (i, 0, 0))
    big = pl.BlockSpec((ng, width, width), lambda i: (i, 0, 0))
    return pl.pallas_call(
        functools.partial(_s5w_kernel, ng=ng),
        grid=(n // ng,),
        in_specs=[pl.BlockSpec((ng, SUBLANES, LANES), lambda i: (i, 0, 0)), small, small, small, small, small],
        out_specs=[big, big, big, pl.BlockSpec((ng, SUBLANES, width), lambda i: (i, 0, 0))],
        out_shape=[jax.ShapeDtypeStruct((n, width, width), BF16)] * 3
                  + [jax.ShapeDtypeStruct((n, SUBLANES, width), F32)],
        compiler_params=_cparams(("parallel",)),
        name="s5_weights",
    )(lam, bt(b_re), bt(b_im), ct(c_re), ct(c_im), dd)


def _s5_pitch(n):
    p = -(-n // SUBLANES)
    return (p | 1) * SUBLANES


def _s5_kernel(uc_ref, ul_ref, kin_ref, vin_ref, win_ref, a_ref, yc_ref, yl_ref,
               sc_scr, sl_scr, xac_scr, xal_scr, xbc_scr, xbl_scr, *, ng, ncc, ncl, bsz):
    half = 2 * S5_STATE
    pc, plat = _s5_pitch(ncc), _s5_pitch(ncl)
    segs = ((uc_ref, yc_ref, sc_scr, xac_scr, xbc_scr, ncc, pc), (ul_ref, yl_ref, sl_scr, xal_scr, xbl_scr, ncl, plat))
    for u_ref, _, s_scr, _, _, n, pitch in segs:
        for g in range(ng):
            s = jnp.dot(u_ref[g], vin_ref[g], preferred_element_type=F32)
            for b in range(bsz):
                for k in range(2):
                    s_scr[g, k, pl.ds(b * pitch, n), :] = s[b * n:(b + 1) * n, k * half:(k + 1) * half]
    lane = lax.broadcasted_iota(jnp.int32, (bsz, half), 1)
    fwd_lane = lane < S5_STATE
    a_re = [a_ref[g, :, :half] for g in range(ng)]
    a_im = [a_ref[g, :, half:] for g in range(ng)]

    def make_step(s_scr, xa_scr, xb_scr, pitch):
        def step(fc, bc, xs):
            rf = pl.ds(fc, bsz, stride=pitch)
            rb = pl.ds(bc, bsz, stride=pitch)
            out = []
            for g in range(ng):
                xr, xi = xs[2 * g], xs[2 * g + 1]
                xa_scr[g, 0, rf, :] = xr
                xa_scr[g, 1, rf, :] = xi
                xb_scr[g, 0, rb, :] = xr
                xb_scr[g, 1, rb, :] = xi
                sr = jnp.where(fwd_lane, s_scr[g, 0, rf, :], s_scr[g, 0, rb, :])
                si = jnp.where(fwd_lane, s_scr[g, 1, rf, :], s_scr[g, 1, rb, :])
                out.append(a_re[g] * xr - a_im[g] * xi + sr)
                out.append(a_re[g] * xi + a_im[g] * xr + si)
            return tuple(out)
        return step

    step_c = make_step(sc_scr, xac_scr, xbc_scr, pc)
    step_l = make_step(sl_scr, xal_scr, xbl_scr, plat)
    xs = tuple(jnp.zeros((bsz, half), F32) for _ in range(2 * ng))
    xs = lax.fori_loop(0, ncc, lambda i, c: step_c(i, ncc - 1 - i, c), xs)
    xs = lax.fori_loop(0, ncl, lambda i, c: step_l(i, ncl - 1 - i, c), xs)
    for u_ref, y_ref, _, xa_scr, xb_scr, n, pitch in segs:
        fwd_r = lax.broadcasted_iota(jnp.int32, (n, half), 1) < S5_STATE
        for g in range(ng):
            rows = []
            for b in range(bsz):
                r = pl.ds(b * pitch, n)
                rows.append(jnp.concatenate([jnp.where(fwd_r, xa_scr[g, k, r, :], xb_scr[g, k, r, :]) for k in range(2)],
                                            axis=1))
            x_in = jnp.concatenate(rows, axis=0).astype(BF16)
            y = (jnp.dot(u_ref[g], kin_ref[g], preferred_element_type=F32)
                 + jnp.dot(x_in, win_ref[g], preferred_element_type=F32))
            y_ref[g] = y.astype(y_ref.dtype)


def _s5(u_ctx, u_lat, kin, vin, win, a_t, l, bsz, ng=4):
    g_n, rc, w = u_ctx.shape
    rl = u_lat.shape[1]
    ncc, ncl = rc // bsz, rl // bsz
    nblk = g_n // ng
    wspec = pl.BlockSpec((ng, w, w), lambda i: (l * nblk + i, 0, 0))
    uspec = lambda r: pl.BlockSpec((ng, r, w), lambda i: (i, 0, 0))
    scr = lambda n: pltpu.VMEM((ng, 2, bsz * _s5_pitch(n), LANES), F32)
    return pl.pallas_call(
        functools.partial(_s5_kernel, ng=ng, ncc=ncc, ncl=ncl, bsz=bsz),
        grid=(nblk,),
        in_specs=[uspec(rc), uspec(rl), wspec, wspec, wspec,
                  pl.BlockSpec((ng, SUBLANES, w), lambda i: (l * nblk + i, 0, 0))],
        out_specs=[uspec(rc), uspec(rl)],
        out_shape=[jax.ShapeDtypeStruct(u_ctx.shape, BF16), jax.ShapeDtypeStruct(u_lat.shape, BF16)],
        scratch_shapes=[scr(ncc), scr(ncl)] * 3,
        compiler_params=_cparams(("parallel",)),
        name="s5_mixer",
    )(u_ctx, u_lat, kin, vin, win, a_t)


def _rotary_tables(seq):
    quarter = RET_DK // 4
    pos = jnp.arange(seq)
    inv_freq = ROPE_BASE ** (-jnp.arange(quarter, dtype=F32) / quarter)
    ang_r = (pos // GRID_W).astype(F32)[:, None] * inv_freq[None, :]
    ang_c = (pos % GRID_W).astype(F32)[:, None] * inv_freq[None, :]
    cos = jnp.concatenate([jnp.cos(ang_r)] * 2 + [jnp.cos(ang_c)] * 2, axis=-1)
    sin = jnp.concatenate([jnp.sin(ang_r)] * 2 + [jnp.sin(ang_c)] * 2, axis=-1)
    cos = jnp.tile(cos, (1, RET_HEADS))
    sin = jnp.tile(sin, (1, RET_HEADS))
    p = np.zeros((RET_QK_WIDTH, RET_QK_WIDTH), np.float32)
    for d in range(RET_QK_WIDTH):
        if d % (2 * quarter) < quarter:
            p[d + quarter, d] = -1.0
        else:
            p[d - quarter, d] = 1.0
    return cos, sin, jnp.asarray(p, BF16)


def _ret_kernel(lg_ref, q_ref, k_ref, v_ref, g_ref, cq_ref, ck_ref, cv_ref, cg_ref,
                cos_ref, sin_ref, p_ref, o_ref, co_ref, krot_scr, sin_scr, dm_scr, *, layer, need_ctx, ncl):
    t = RET_T
    qkw, vw = RET_QK_WIDTH, RET_WIDTH

    def per_head(shape, axis, width, d):
        head = lax.broadcasted_iota(jnp.int32, shape, axis) // width
        out = jnp.zeros(shape, F32)
        for h in range(RET_HEADS):
            out = jnp.where(head == h, lg_ref[layer, d * RET_HEADS + h], out)
        return out

    row = lax.broadcasted_iota(jnp.int32, (t, qkw), 0).astype(F32)
    lgf = per_head((t, qkw), 1, RET_DK, 0)
    lgb = per_head((t, qkw), 1, RET_DK, 1)
    qdec_f = jnp.exp((row + 1.0) * lgf)
    qdec_b = jnp.exp((t - row) * lgb)
    kdec_f = jnp.exp((t - 1.0 - row) * lgf)
    kdec_b = jnp.exp(row * lgb)
    cdec_f = jnp.exp(float(t) * per_head((qkw, vw), 0, RET_DK, 0))
    cdec_b = jnp.exp(float(t) * per_head((qkw, vw), 0, RET_DK, 1))
    blk = (lax.broadcasted_iota(jnp.int32, (qkw, vw), 0) // RET_DK
           == lax.broadcasted_iota(jnp.int32, (qkw, vw), 1) // RET_DV)
    head_lane = lax.broadcasted_iota(jnp.int32, (t, qkw), 1) // RET_DK

    ii = lax.broadcasted_iota(jnp.int32, (t, t), 0)
    jj = lax.broadcasted_iota(jnp.int32, (t, t), 1)
    dif = (ii - jj).astype(F32)
    for h in range(RET_HEADS):
        df = jnp.where(dif >= 0, jnp.exp(jnp.where(dif >= 0, dif, 0.0) * lg_ref[layer, h]), 0.0)
        db = jnp.where(dif < 0, jnp.exp(jnp.where(dif < 0, -dif, 0.0) * lg_ref[layer, RET_HEADS + h]), 0.0)
        dm_scr[h] = df + db

    def rotary(x_bf, c):
        rows = pl.ds(c * t, t)
        swapped = jnp.dot(x_bf, p_ref[...], preferred_element_type=F32)
        return x_bf.astype(F32) * cos_ref[rows, :] + swapped * sin_ref[rows, :]

    krot_scr[pl.ds(0, t), :] = ck_ref[...]
    for c in range(ncl):
        krot_scr[pl.ds((c + 1) * t, t), :] = rotary(k_ref[pl.ds(c * t, t), :], c).astype(BF16)

    def v_chunk(c):
        return cv_ref[...] if c == 0 else v_ref[pl.ds((c - 1) * t, t), :]

    def kv(c, kdec):
        kd = (krot_scr[pl.ds(c * t, t), :].astype(F32) * kdec).astype(BF16)
        return lax.dot_general(kd, v_chunk(c), (((0,), (0,)), ((), ())), preferred_element_type=F32)

    s = jnp.zeros((qkw, vw), F32)
    for c in range(ncl + 1):
        sin_scr[c, pl.ds(0, qkw), :] = jnp.where(blk, s, 0.0).astype(BF16)
        if c < ncl:
            s = cdec_f * s + kv(c, kdec_f)
    sin_scr[0, pl.ds(qkw, qkw), :] = jnp.zeros((qkw, vw), BF16)
    s = kv(0, kdec_b)
    for c in range(ncl, 0, -1):
        sin_scr[c, pl.ds(qkw, qkw), :] = jnp.where(blk, s, 0.0).astype(BF16)
        if c > 1:
            s = cdec_b * s + kv(c, kdec_b)

    for c in range(0 if need_ctx else 1, ncl + 1):
        if c == 0:
            q = cq_ref[...].astype(F32)
            gate = cg_ref[...].astype(F32)
        else:
            q = rotary(q_ref[pl.ds((c - 1) * t, t), :], c - 1)
            gate = g_ref[pl.ds((c - 1) * t, t), :].astype(F32)
        q_bf = q.astype(BF16)
        cross = (jnp.dot((q * qdec_f).astype(BF16), sin_scr[c, pl.ds(0, qkw), :], preferred_element_type=F32)
                 + jnp.dot((q * qdec_b).astype(BF16), sin_scr[c, pl.ds(qkw, qkw), :], preferred_element_type=F32))
        k_c = krot_scr[pl.ds(c * t, t), :]
        v_c = v_chunk(c)
        outs = []
        q_heads = jnp.concatenate([jnp.where(head_lane == h, q_bf, jnp.zeros_like(q_bf)) for h in range(RET_HEADS)],
                                  axis=0)
        sc_heads = lax.dot_general(q_heads, k_c, (((1,), (1,)), ((), ())), preferred_element_type=F32)
        for h in range(RET_HEADS):
            sc = (sc_heads[h * t:(h + 1) * t] * dm_scr[h]).astype(BF16)
            o = (jnp.dot(sc, v_c[:, h * RET_DV:(h + 1) * RET_DV], preferred_element_type=F32)
                 + cross[:, h * RET_DV:(h + 1) * RET_DV])
            mu = jnp.mean(o, axis=-1, keepdims=True)
            var = jnp.mean(jnp.square(o - mu), axis=-1, keepdims=True)
            outs.append((o - mu) * lax.rsqrt(var + GN_EPS))
        y = gate * _sigmoid(gate) * jnp.concatenate(outs, axis=1)
        if c == 0:
            co_ref[...] = y.astype(co_ref.dtype)
        else:
            o_ref[pl.ds((c - 1) * t, t), :] = y.astype(o_ref.dtype)
    if not need_ctx:
        co_ref[...] = jnp.zeros(co_ref.shape, co_ref.dtype)


def _retention(proj_lat, proj_ctx, lg, cos, sin, pmat, l, bsz, seq, n_ctx, need_ctx):
    t = RET_T
    ncl = seq // t
    assert n_ctx == t
    qb, kb = _MY_OFF['rq'] // RET_QK_WIDTH, _MY_OFF['rk'] // RET_QK_WIDTH
    vb, gb = _MY_OFF['rv'] // RET_WIDTH, _MY_OFF['rg'] // RET_WIDTH

    def col(n, w, j):
        return pl.BlockSpec((n, w), lambda b, j=j: (b, j))

    const = lambda shape: pl.BlockSpec(shape, lambda b: (0,) * len(shape))
    return pl.pallas_call(
        functools.partial(_ret_kernel, layer=l, need_ctx=need_ctx, ncl=ncl),
        grid=(bsz,),
        in_specs=[pl.BlockSpec(memory_space=pltpu.SMEM),
                  col(seq, RET_QK_WIDTH, qb), col(seq, RET_QK_WIDTH, kb), col(seq, RET_WIDTH, vb), col(seq, RET_WIDTH, gb),
                  col(n_ctx, RET_QK_WIDTH, qb), col(n_ctx, RET_QK_WIDTH, kb), col(n_ctx, RET_WIDTH, vb), col(n_ctx, RET_WIDTH, gb),
                  const((seq, RET_QK_WIDTH)), const((seq, RET_QK_WIDTH)), const((RET_QK_WIDTH, RET_QK_WIDTH))],
        out_specs=[pl.BlockSpec((seq, RET_WIDTH), lambda b: (b, 0)),
                   pl.BlockSpec((n_ctx, RET_WIDTH), lambda b: (b, 0))],
        out_shape=[jax.ShapeDtypeStruct((bsz * seq, RET_WIDTH), BF16),
                   jax.ShapeDtypeStruct((bsz * n_ctx, RET_WIDTH), BF16)],
        scratch_shapes=[pltpu.VMEM((seq + n_ctx, RET_QK_WIDTH), BF16),
                        pltpu.VMEM((ncl + 1, 2 * RET_QK_WIDTH, RET_WIDTH), BF16),
                        pltpu.VMEM((RET_HEADS, t, t), F32)],
        compiler_params=_cparams(("parallel",)),
        name="retention",
    )(lg, proj_lat, proj_lat, proj_lat, proj_lat, proj_ctx, proj_ctx, proj_ctx, proj_ctx, cos, sin, pmat)


def _na_block_start(kblk, rows):
    return jnp.clip(kblk * NA_QROWS - NA_ROWS // 2, 0, rows - NA_KROWS)


def _na_bias_table(rpb):
    depth, heads, nlag, ncol = rpb.shape
    cols = np.arange(GRID_W)
    cs = np.clip(cols - NA_COLS // 2, 0, GRID_W - NA_COLS)
    valid_c = (cols[None, :] >= cs[:, None]) & (cols[None, :] < cs[:, None] + NA_COLS)
    pad = GRID_W - 1
    padded = jnp.pad(rpb.astype(F32) * LOG2E, ((0, 0), (0, 0), (0, 0), (pad, pad)))
    toe = jnp.stack([padded[..., pad + NA_COLS - 1 - qc: pad + NA_COLS - 1 - qc + GRID_W] for qc in range(GRID_W)],
                    axis=-2)
    toe = jnp.where(jnp.asarray(valid_c), toe, NEG_INF)
    toe = jnp.pad(toe, ((0, 0), (0, 0), (1, NA_NLAG + 1 - nlag - 1), (0, 0), (0, 0)))
    table = jnp.concatenate([toe[:, :, :NA_NLAG], toe[:, :, 1:NA_NLAG + 1]], axis=-1)
    return table.reshape(depth, heads * NA_NLAG, GRID_W, 2 * GRID_W)


def _attend(q_pair, k_list, v_list, bias_list):
    nq = q_pair.shape[0]
    lane = lax.broadcasted_iota(jnp.int32, q_pair.shape, 1) // NA_HEAD_DIM
    zero = jnp.zeros_like(q_pair)
    q2 = jnp.concatenate([jnp.where(lane == 0, q_pair, zero), jnp.where(lane == 1, q_pair, zero)], axis=0)
    ss = []
    for k_i, b_i in zip(k_list, bias_list):
        s = lax.dot_general(q2, k_i, (((1,), (1,)), ((), ())), preferred_element_type=F32)
        if b_i is not None:
            s = s + jnp.concatenate([b_i(0), b_i(1)], axis=0)
        ss.append(s)
    m = ss[0].max(axis=-1, keepdims=True)
    for s in ss[1:]:
        m = jnp.maximum(m, s.max(axis=-1, keepdims=True))
    ps = [jnp.exp2(s - m) for s in ss]
    den = ps[0].sum(axis=-1, keepdims=True)
    for p in ps[1:]:
        den = den + p.sum(axis=-1, keepdims=True)
    acc = jnp.dot(ps[0].astype(BF16), v_list[0], preferred_element_type=F32)
    for p, v_i in zip(ps[1:], v_list[1:]):
        acc = acc + jnp.dot(p.astype(BF16), v_i, preferred_element_type=F32)
    out = acc / den
    return jnp.where(lane == 0, out[:nq], out[nq:])


def _na_kernel(q_ref, k_ref, v_ref, ck_ref, cv_ref, tab_ref, o_ref, *, rows):
    nq, nk = NA_QROWS * GRID_W, NA_KROWS * GRID_W
    kblk = pl.program_id(1)
    r0 = kblk * NA_QROWS
    ks_row = _na_block_start(kblk, rows)
    ks = pl.multiple_of(ks_row * GRID_W, NA_QROWS * GRID_W)
    qr = r0 + lax.broadcasted_iota(jnp.int32, (nq, nk), 0) // GRID_W
    kr = ks_row + lax.broadcasted_iota(jnp.int32, (nq, nk), 1) // GRID_W
    rs = jnp.clip(qr - NA_ROWS // 2, 0, rows - NA_ROWS)
    row_mask = jnp.where((kr >= rs) & (kr < rs + NA_ROWS), 0.0, NEG_INF)

    def bias(h):
        row_blocks = []
        for qrl in range(NA_QROWS):
            tiles = []
            for kp in range(NA_KROWS // 2):
                lag = ks_row - r0 + 2 * kp - qrl + NA_ROWS - 1
                idx = h * NA_NLAG + jnp.clip(lag, -1, NA_NLAG - 2) + 1
                tiles.append(tab_ref[0, idx])
            row_blocks.append(jnp.concatenate(tiles, axis=1))
        return jnp.concatenate(row_blocks, axis=0) + row_mask

    for hp in range(NA_HEADS // 2):
        ln = pl.ds(hp * LANES, LANES)
        y = _attend(q_ref[:, ln],
                    [k_ref[pl.ds(ks, nk), ln], ck_ref[:, ln]],
                    [v_ref[pl.ds(ks, nk), ln], cv_ref[:, ln]],
                    [lambda hh, hp=hp: bias(2 * hp + hh), None])
        o_ref[:, ln] = y.astype(o_ref.dtype)


def _na(proj_lat, proj_ctx, table, l, bsz, seq, n_ctx):
    rows = seq // GRID_W
    nq = NA_QROWS * GRID_W
    nblk = seq // nq
    qb, kb, vb = (_MY_OFF[n] // NA_WIDTH for n in ('nq', 'nk', 'nv'))
    return pl.pallas_call(
        functools.partial(_na_kernel, rows=rows),
        grid=(bsz, nblk),
        in_specs=[pl.BlockSpec((nq, NA_WIDTH), lambda b, k: (b * nblk + k, qb)),
                  pl.BlockSpec((seq, NA_WIDTH), lambda b, k: (b, kb)),
                  pl.BlockSpec((seq, NA_WIDTH), lambda b, k: (b, vb)),
                  pl.BlockSpec((n_ctx, NA_WIDTH), lambda b, k: (b, kb)),
                  pl.BlockSpec((n_ctx, NA_WIDTH), lambda b, k: (b, vb)),
                  pl.BlockSpec((1,) + table.shape[1:], lambda b, k: (l, 0, 0, 0))],
        out_specs=pl.BlockSpec((nq, NA_WIDTH), lambda b, k: (b * nblk + k, 0)),
        out_shape=jax.ShapeDtypeStruct((bsz * seq, NA_WIDTH), BF16),
        compiler_params=_cparams(("parallel", "arbitrary")),
        name="neighborhood_attention",
    )(proj_lat, proj_lat, proj_lat, proj_ctx, proj_ctx, table)


def _ctx_attn_kernel(q_ref, k_ref, v_ref, o_ref):
    for hp in range(NA_HEADS // 2):
        ln = pl.ds(hp * LANES, LANES)
        y = _attend(q_ref[:, ln], [k_ref[:, ln]], [v_ref[:, ln]], [None])
        o_ref[:, ln] = y.astype(o_ref.dtype)


def _ctx_attn(proj_ctx, bsz, n_ctx):
    qb, kb, vb = (_MY_OFF[n] // NA_WIDTH for n in ('nq', 'nk', 'nv'))
    spec = lambda j: pl.BlockSpec((n_ctx, NA_WIDTH), lambda b: (b, j))
    return pl.pallas_call(
        _ctx_attn_kernel,
        grid=(bsz,),
        in_specs=[spec(qb), spec(kb), spec(vb)],
        out_specs=pl.BlockSpec((n_ctx, NA_WIDTH), lambda b: (b, 0)),
        out_shape=jax.ShapeDtypeStruct((bsz * n_ctx, NA_WIDTH), BF16),
        compiler_params=_cparams(("parallel",)),
        name="context_attention",
    )(proj_ctx, proj_ctx, proj_ctx)


def _gelu_tanh(x):
    return 0.5 * x * (1.0 + jnp.tanh(math.sqrt(2.0 / math.pi) * (x + 0.044715 * (x * x * x))))


def _merge_kernel(x_ref, g0_ref, g1_ref, g2_ref, yt_ref, yret_ref, yna_ref, wglu_ref, bglu_ref,
                  wbs5_ref, wbret_ref, wbna_ref, wout_ref, gate_ref, o_ref, w_scr):
    nrow = yt_ref.shape[1]
    for j in range(_NSLAB):
        for half in range(2):
            o = [yt_ref[j * _GPL + q, :, half * LANES:(half + 1) * LANES].astype(F32) for q in range(_GPL)]
            for tl, out in enumerate(_block_transpose(o)):
                w_scr[j, pl.ds(half * _GPL + tl, nrow, stride=S5_CHUNK), :] = out
    ge = _gelu_tanh(jnp.concatenate([w_scr[j] for j in range(_NSLAB)], axis=1))
    z = jnp.dot(ge.astype(BF16), wglu_ref[0], preferred_element_type=F32) + bglu_ref[0]
    s5 = (ge * _sigmoid(z)).astype(BF16)
    m = (_sigmoid(g0_ref[...].astype(F32)) * jnp.dot(s5, wbs5_ref[0], preferred_element_type=F32)
         + _sigmoid(g1_ref[...].astype(F32)) * jnp.dot(yret_ref[...], wbret_ref[0], preferred_element_type=F32)
         + _sigmoid(g2_ref[...].astype(F32)) * jnp.dot(yna_ref[...], wbna_ref[0], preferred_element_type=F32))
    o_ref[...] = x_ref[...] + gate_ref[0] * jnp.dot(m.astype(BF16), wout_ref[0], preferred_element_type=F32)


def _merge(x2, proj, y_t, yret, yna, mods, mod_row, l, wts, tm):
    m, d = x2.shape
    base = l * MOD_ROWS * 6
    rowblk = lambda w, j=0: pl.BlockSpec((tm, w), lambda i, j=j: (i, j))
    return pl.pallas_call(
        _merge_kernel,
        grid=(m // tm,),
        in_specs=[rowblk(d), rowblk(d, 0), rowblk(d, 1), rowblk(d, 2),
                  pl.BlockSpec((S5_GROUPS, tm // S5_CHUNK, S5_CHUNK * S5_GROUP), lambda i: (0, i, 0)),
                  rowblk(RET_WIDTH), rowblk(NA_WIDTH)]
                 + [_layer_spec(w, l) for w in wts]
                 + [pl.BlockSpec((1, 1, d), lambda i: (base + mod_row(i) * 6 + 2, 0, 0))],
        out_specs=rowblk(d),
        out_shape=jax.ShapeDtypeStruct((m, d), F32),
        scratch_shapes=[pltpu.VMEM((_NSLAB, tm, LANES), F32)],
        compiler_params=_cparams(("parallel",)),
        name="merge_residual",
    )(x2, proj, proj, proj, y_t, yret, yna, *wts, mods)


def _ffn_kernel(x_ref, sh_ref, sc_ref, gate_ref, wg_ref, wu_ref, wd_ref, fn_ref, o_ref, *, final, th):
    x = x_ref[...]
    h = (_rms(x) * (1.0 + sc_ref[0]) + sh_ref[0]).astype(BF16)
    hidden = wg_ref.shape[2]
    acc = jnp.zeros(x.shape, F32)
    for j in range(hidden // th):
        a = jnp.dot(h, wg_ref[0, :, j * th:(j + 1) * th], preferred_element_type=F32)
        b = jnp.dot(h, wu_ref[0, :, j * th:(j + 1) * th], preferred_element_type=F32)
        act = (a * _sigmoid(a) * b).astype(BF16)
        acc = acc + jnp.dot(act, wd_ref[0, j * th:(j + 1) * th, :], preferred_element_type=F32)
    y = x + gate_ref[0] * acc
    if final:
        y = _rms(y) * fn_ref[...]
    o_ref[...] = y


def _ffn(x2, mods, mod_row, l, wg, wu, wd, fn, tm, final):
    m, d = x2.shape
    base = l * MOD_ROWS * 6
    modspec = lambda k: pl.BlockSpec((1, 1, d), lambda i, k=k: (base + mod_row(i) * 6 + k, 0, 0))
    return pl.pallas_call(
        functools.partial(_ffn_kernel, final=final, th=256),
        grid=(m // tm,),
        in_specs=[pl.BlockSpec((tm, d), lambda i: (i, 0)), modspec(3), modspec(4), modspec(5),
                  _layer_spec(wg, l), _layer_spec(wu, l), _layer_spec(wd, l),
                  pl.BlockSpec(fn.shape, lambda i: (0, 0))],
        out_specs=pl.BlockSpec((tm, d), lambda i: (i, 0)),
        out_shape=jax.ShapeDtypeStruct((m, d), F32),
        compiler_params=_cparams(("parallel",)),
        name="swiglu_residual",
    )(x2, mods, mods, mods, wg, wu, wd, fn)


def kernel(x, c, ctx, c_ctx, w_ada, b_ada, w_in, s5_lam_re, s5_lam_im, s5_log_dt, s5_b_re, s5_b_im,
           s5_c_re, s5_c_im, s5_d, s5_w_glu, s5_b_glu, ret_theta, na_rpb, w_branch_s5, w_branch_ret,
           w_branch_na, w_out, w_ffn_gate, w_ffn_up, w_ffn_down, final_norm):
    bsz, seq, d = x.shape
    n_ctx = ctx.shape[1]
    depth = w_ada.shape[0]
    ctx_row = bsz
    assert bsz == SUBLANES and bsz + 1 <= MOD_ROWS

    cvec = jnp.zeros((MOD_ROWS, d), F32).at[:bsz].set(c).at[ctx_row].set(c_ctx)
    mods = _ada(cvec, w_ada, b_ada).reshape(depth * MOD_ROWS * 6, 1, d)

    w_in_k = jnp.concatenate(
        [w_in[:, :, _REF_OFF[n]:_REF_OFF[n] + _REF_W[n]] * _COL_SCALE.get(n, 1.0) for n in _MY_ORDER],
        axis=2).astype(BF16)

    cos, sin, pmat = _rotary_tables(seq)
    log_gamma = jax.nn.log_sigmoid(ret_theta.astype(F32)).reshape(depth, 2 * RET_HEADS)
    kin, vin, win, a_t = _s5_weights(s5_lam_re, s5_lam_im, s5_log_dt, s5_b_re, s5_b_im, s5_c_re, s5_c_im, s5_d)
    na_table = _na_bias_table(na_rpb)
    merge_w = (s5_w_glu.astype(BF16), s5_b_glu.reshape(depth, 1, -1).astype(F32), w_branch_s5.astype(BF16),
               w_branch_ret.astype(BF16), w_branch_na.astype(BF16), w_out.astype(BF16))
    ffn_w = (w_ffn_gate.astype(BF16), w_ffn_up.astype(BF16), w_ffn_down.astype(BF16),
             final_norm.reshape(1, d).astype(F32))

    tm_proj, tn_proj, tm = 1024, N_IN // 2, 512
    lat_row = lambda t: (lambda i: i // (seq // t))
    ctx_mod_row = lambda i: ctx_row

    x2 = x.reshape(bsz * seq, d)
    c2 = ctx.reshape(bsz * n_ctx, d)
    for l in range(depth):
        need_ctx = l < depth - 1
        proj_lat, u_lat = _inproj(x2, mods, lat_row(tm_proj), w_in_k, l, tm_proj, tn_proj)
        proj_ctx, u_ctx = _inproj(c2, mods, ctx_mod_row, w_in_k, l, tm_proj, tn_proj)

        ys5_ctx, ys5_lat = _s5(u_ctx, u_lat, kin, vin, win, a_t, l, bsz)
        yret_lat, yret_ctx = _retention(proj_lat, proj_ctx, log_gamma, cos, sin, pmat, l, bsz, seq, n_ctx, need_ctx)
        yna_lat = _na(proj_lat, proj_ctx, na_table, l, bsz, seq, n_ctx)

        x2 = _merge(x2, proj_lat, ys5_lat, yret_lat, yna_lat, mods, lat_row(tm), l, merge_w, tm)
        x2 = _ffn(x2, mods, lat_row(tm), l, *ffn_w, tm, final=not need_ctx)
        if need_ctx:
            yna_ctx = _ctx_attn(proj_ctx, bsz, n_ctx)
            c2 = _merge(c2, proj_ctx, ys5_ctx, yret_ctx, yna_ctx, mods, ctx_mod_row, l, merge_w, tm)
            c2 = _ffn(c2, mods, ctx_mod_row, l, *ffn_w, tm, final=False)
    return x2.reshape(bsz, seq, d)
```

```python
import functools
import math

import numpy as np
import jax
import jax.numpy as jnp
from jax import lax
from jax.experimental import pallas as pl
from jax.experimental.pallas import tpu as pltpu

F32 = jnp.float32
BF16 = jnp.bfloat16
HIGHEST = lax.Precision.HIGHEST

D_MODEL = 1024
GRID_W = 64
S5_WIDTH = 512
S5_GROUP = 16
S5_GROUPS = S5_WIDTH // S5_GROUP
S5_STATE = 64
S5_CHUNK = 16
RET_HEADS = 4
RET_DK = 64
RET_DV = 128
RET_QK_WIDTH = RET_HEADS * RET_DK
RET_WIDTH = RET_HEADS * RET_DV
RET_T = 256
NA_HEADS = 8
NA_HEAD_DIM = 64
NA_WIDTH = NA_HEADS * NA_HEAD_DIM
NA_ROWS = 8
NA_COLS = 16
NA_QROWS = 4
NA_KROWS = 12
NA_NLAG = 2 * NA_ROWS
N_BRANCH = 3
ROPE_BASE = 10000.0
RMS_EPS = 1e-6
GN_EPS = 1e-5
NEG_INF = -1e30
LANES = 128
SUBLANES = 8
MOD_ROWS = 16

_REF_SPLIT = (S5_WIDTH, RET_QK_WIDTH, RET_WIDTH, NA_WIDTH, NA_WIDTH,
              RET_QK_WIDTH, RET_WIDTH, NA_WIDTH, N_BRANCH * D_MODEL)
_REF_NAMES = ('u', 'rk', 'rv', 'nk', 'nv', 'rq', 'rg', 'nq', 'gates')
_REF_OFF = dict(zip(_REF_NAMES, np.concatenate([[0], np.cumsum(_REF_SPLIT)[:-1]]).tolist()))
_REF_W = dict(zip(_REF_NAMES, _REF_SPLIT))
_MY_ORDER = ('gates', 'rk', 'rq', 'u', 'rv', 'nk', 'nv', 'rg', 'nq')
_MY_OFF = {}
_o = 0
for _n in _MY_ORDER:
    _MY_OFF[_n] = _o
    _o += _REF_W[_n]
N_IN = _o
LOG2E = math.log2(math.e)
_COL_SCALE = {'nq': NA_HEAD_DIM ** -0.5 * LOG2E, 'rk': RET_DK ** -0.5}

VMEM_LIMIT = 56 * 1024 * 1024


def _cparams(sem):
    return pltpu.CompilerParams(dimension_semantics=sem, vmem_limit_bytes=VMEM_LIMIT)


def _sigmoid(x):
    return 1.0 / (1.0 + jnp.exp(-x))


def _rms(x):
    return x * lax.rsqrt(jnp.mean(x * x, axis=-1, keepdims=True) + RMS_EPS)


def _layer_spec(arr, l):
    nd = arr.ndim
    return pl.BlockSpec((1,) + arr.shape[1:], lambda *_: (l,) + (0,) * (nd - 1))


def _ada_kernel(c_ref, w_ref, b_ref, o_ref):
    c = c_ref[...]
    s = c * _sigmoid(c)
    o_ref[0] = jnp.dot(s, w_ref[0], preferred_element_type=F32, precision=HIGHEST) + b_ref[0]


def _ada(cvec, w_ada, b_ada):
    depth, d, n = w_ada.shape
    tn = 1536
    rows = cvec.shape[0]
    return pl.pallas_call(
        _ada_kernel,
        grid=(depth, n // tn),
        in_specs=[pl.BlockSpec((rows, d), lambda l, j: (0, 0)),
                  pl.BlockSpec((1, d, tn), lambda l, j: (l, 0, j)),
                  pl.BlockSpec((1, 1, tn), lambda l, j: (l, 0, j))],
        out_specs=pl.BlockSpec((1, rows, tn), lambda l, j: (l, 0, j)),
        out_shape=jax.ShapeDtypeStruct((depth, rows, n), F32),
        compiler_params=_cparams(("parallel", "parallel")),
        name="ada_mod",
    )(cvec, w_ada, b_ada.reshape(depth, 1, n))


_GPL = LANES // S5_GROUP
_NSLAB = S5_WIDTH // LANES


def _block_transpose(tiles):
    blk = lax.broadcasted_iota(jnp.int32, tiles[0].shape, 1) // S5_GROUP
    tiles = list(tiles)
    s = _GPL // 2
    while s >= 1:
        hi = (blk & s) != 0
        for a in range(_GPL):
            if a & s:
                continue
            b = a + s
            ta, tb = tiles[a], tiles[b]
            tiles[a] = jnp.where(hi, pltpu.roll(tb, s * S5_GROUP, 1), ta)
            tiles[b] = jnp.where(hi, tb, pltpu.roll(ta, LANES - s * S5_GROUP, 1))
        s //= 2
    return tiles


def _inproj_kernel(x_ref, sh_ref, sc_ref, w_ref, o_ref, ut_ref, h_ref, u_scr, *, u_off):
    @pl.when(pl.program_id(1) == 0)
    def _():
        h = _rms(x_ref[...]) * (1.0 + sc_ref[0]) + sh_ref[0]
        h_ref[...] = h.astype(BF16)

    res = jnp.dot(h_ref[...], w_ref[0], preferred_element_type=F32)
    o_ref[...] = res.astype(o_ref.dtype)
    for j in range(_NSLAB):
        u_scr[j] = res[:, u_off + j * LANES:u_off + (j + 1) * LANES]
    nrow = u_scr.shape[1] // S5_CHUNK
    for j in range(_NSLAB):
        for half in range(2):
            v = [u_scr[j, pl.ds(half * _GPL + tl, nrow, stride=S5_CHUNK), :] for tl in range(_GPL)]
            for q, out in enumerate(_block_transpose(v)):
                ut_ref[j * _GPL + q, :, half * LANES:(half + 1) * LANES] = out.astype(ut_ref.dtype)


def _inproj(x2, mods, mod_row, w, l, tm, tn):
    m, d = x2.shape
    n = w.shape[2]
    base = l * MOD_ROWS * 6
    nj = n // tn
    u_off = _MY_OFF['u'] - (nj - 1) * tn
    assert 0 <= u_off and u_off + S5_WIDTH <= tn
    return pl.pallas_call(
        functools.partial(_inproj_kernel, u_off=u_off),
        grid=(m // tm, nj),
        in_specs=[pl.BlockSpec((tm, d), lambda i, j: (i, 0)),
                  pl.BlockSpec((1, 1, d), lambda i, j: (base + mod_row(i) * 6 + 0, 0, 0)),
                  pl.BlockSpec((1, 1, d), lambda i, j: (base + mod_row(i) * 6 + 1, 0, 0)),
                  pl.BlockSpec((1, d, tn), lambda i, j: (l, 0, j))],
        out_specs=[pl.BlockSpec((tm, tn), lambda i, j: (i, j)),
                   pl.BlockSpec((S5_GROUPS, tm // S5_CHUNK, S5_CHUNK * S5_GROUP), lambda i, j: (0, i, 0))],
        out_shape=[jax.ShapeDtypeStruct((m, n), BF16),
                   jax.ShapeDtypeStruct((S5_GROUPS, m // S5_CHUNK, S5_CHUNK * S5_GROUP), BF16)],
        scratch_shapes=[pltpu.VMEM((tm, d), BF16), pltpu.VMEM((_NSLAB, tm, LANES), F32)],
        compiler_params=_cparams(("parallel", "arbitrary")),
        name="in_proj",
    )(x2, mods, mods, w)


def _s5w_kernel(lam_ref, btr_ref, bti_ref, ctr_ref, cti_ref, dd_ref, kin_ref, vin_ref, win_ref, at_ref, *, ng):
    t_n, h_n = S5_CHUNK, S5_GROUP
    width = t_n * h_n
    lane = lax.broadcasted_iota(jnp.int32, (1, LANES), 1)
    f_lane = lane < S5_STATE
    tau = lax.broadcasted_iota(jnp.int32, (3 * SUBLANES, LANES), 0).astype(F32)
    lane_w = lax.broadcasted_iota(jnp.int32, (h_n, width), 1)
    for g in range(ng):
        lam_re, lam_im, dt = lam_ref[g, 0:1, :], lam_ref[g, 1:2, :], lam_ref[g, 2:3, :]
        mag = jnp.exp(tau * (lam_re * dt))
        ang = tau * (lam_im * dt)
        pr, pi = mag * jnp.cos(ang), mag * jnp.sin(ang)
        ab_re, ab_im = pr[1:2], pi[1:2]
        den = lam_re * lam_re + lam_im * lam_im
        f_re = ((ab_re - 1.0) * lam_re + ab_im * lam_im) / den
        f_im = (ab_im * lam_re - (ab_re - 1.0) * lam_im) / den
        btr, bti = btr_ref[g], bti_ref[g]
        bbr = f_re * btr - f_im * bti
        bbi = f_re * bti + f_im * btr
        ctr, cti = ctr_ref[g], cti_ref[g]

        def powers(pf, pb):
            rr = [jnp.broadcast_to(jnp.where(f_lane, pr[pf[t]:pf[t] + 1], pr[pb[t]:pb[t] + 1]), (h_n, LANES))
                  for t in range(t_n)]
            ri = [jnp.broadcast_to(jnp.where(f_lane, pi[pf[t]:pf[t] + 1], pi[pb[t]:pb[t] + 1]), (h_n, LANES))
                  for t in range(t_n)]
            return jnp.concatenate(rr, axis=0), jnp.concatenate(ri, axis=0)

        tile = lambda a: jnp.concatenate([a] * t_n, axis=0)
        bbr_t, bbi_t, ctr_t, cti_t = tile(bbr), tile(bbi), tile(ctr), tile(cti)

        xr, xi = powers([t_n - 1 - t for t in range(t_n)], list(range(t_n)))
        vin = jnp.concatenate([xr * bbr_t - xi * bbi_t, xr * bbi_t + xi * bbr_t], axis=1)
        vin_ref[g] = vin.astype(vin_ref.dtype)

        yr, yi = powers([t + 1 for t in range(t_n)], [t_n - t for t in range(t_n)])
        win_t = jnp.concatenate([ctr_t * yr - cti_t * yi, -(ctr_t * yi + cti_t * yr)], axis=1)
        win_ref[g] = win_t.T.astype(win_ref.dtype)

        zr, zi = powers(list(range(t_n)), [t_n - 1 - t for t in range(t_n)])
        fmat = jnp.concatenate([ctr_t * zr - cti_t * zi, ctr_t * zi + cti_t * zr], axis=1)
        dn = (((1,), (1,)), ((), ()))
        lhs_f = jnp.concatenate([jnp.where(f_lane, bbr, 0.0), jnp.where(f_lane, -bbi, 0.0)], axis=1)
        lhs_b = jnp.concatenate([jnp.where(f_lane, 0.0, bbr), jnp.where(f_lane, 0.0, -bbi)], axis=1)
        w_f = lax.dot_general(lhs_f, fmat, dn, preferred_element_type=F32, precision=HIGHEST)
        w_b = lax.dot_general(lhs_b, fmat, dn, preferred_element_type=F32, precision=HIGHEST)
        w_f = w_f + jnp.concatenate([dd_ref[g], jnp.zeros((h_n, width - LANES), F32)], axis=1)
        blocks = []
        for t in range(t_n):
            sh_f = t * h_n
            sh_b = (width - (t_n - 1 - t) * h_n) % width
            fw = w_f if sh_f == 0 else pltpu.roll(w_f, sh_f, 1)
            bw = w_b if sh_b == 0 else pltpu.roll(w_b, sh_b, 1)
            blocks.append(jnp.where(lane_w >= t * h_n, fw, 0.0) + jnp.where(lane_w < (t + 1) * h_n, bw, 0.0))
        kin_ref[g] = jnp.concatenate(blocks, axis=0).astype(kin_ref.dtype)
        at_ref[g] = jnp.broadcast_to(jnp.concatenate([pr[t_n:t_n + 1], pi[t_n:t_n + 1]], axis=1),
                                     (SUBLANES, 2 * LANES))


def _s5_weights(lam_re, lam_im, log_dt, b_re, b_im, c_re, c_im, d_skip, ng=4):
    depth, _, g_n, p_n = lam_re.shape
    h_n = S5_GROUP
    n = depth * g_n
    width = S5_CHUNK * h_n
    pair = lambda a: a.astype(F32).transpose(0, 2, 1, 3).reshape(n, 1, 2 * p_n)
    dt = jnp.broadcast_to(jnp.exp(log_dt.astype(F32))[..., None], lam_re.shape)
    lam = jnp.concatenate([pair(lam_re), pair(lam_im), pair(dt), jnp.zeros((n, SUBLANES - 3, 2 * p_n), F32)], axis=1)
    bt = lambda a: a.astype(F32).transpose(0, 2, 4, 1, 3).reshape(n, h_n, 2 * p_n)
    ct = lambda a: a.astype(F32).transpose(0, 2, 3, 1, 4).reshape(n, h_n, 2 * p_n)
    dd = jnp.eye(h_n, LANES, dtype=F32)[None] * d_skip.astype(F32).reshape(n, h_n, 1)
    small = pl.BlockSpec((ng, h_n, LANES), lambda i: (i, 0, 0))
    big = pl.BlockSpec((ng, width, width), lambda i: (i, 0, 0))
    return pl.pallas_call(
        functools.partial(_s5w_kernel, ng=ng),
        grid=(n // ng,),
        in_specs=[pl.BlockSpec((ng, SUBLANES, LANES), lambda i: (i, 0, 0)), small, small, small, small, small],
        out_specs=[big, big, big, pl.BlockSpec((ng, SUBLANES, width), lambda i: (i, 0, 0))],
        out_shape=[jax.ShapeDtypeStruct((n, width, width), BF16)] * 3
                  + [jax.ShapeDtypeStruct((n, SUBLANES, width), F32)],
        compiler_params=_cparams(("parallel",)),
        name="s5_weights",
    )(lam, bt(b_re), bt(b_im), ct(c_re), ct(c_im), dd)


def _s5_pitch(n):
    p = -(-n // SUBLANES)
    return (p | 1) * SUBLANES


def _s5_kernel(uc_ref, ul_ref, kin_ref, vin_ref, win_ref, a_ref, yc_ref, yl_ref,
               sc_scr, sl_scr, xac_scr, xal_scr, xbc_scr, xbl_scr, *, ng, ncc, ncl, bsz):
    half = 2 * S5_STATE
    pc, plat = _s5_pitch(ncc), _s5_pitch(ncl)
    segs = ((uc_ref, yc_ref, sc_scr, xac_scr, xbc_scr, ncc, pc), (ul_ref, yl_ref, sl_scr, xal_scr, xbl_scr, ncl, plat))
    for u_ref, _, s_scr, _, _, n, pitch in segs:
        for g in range(ng):
            s = jnp.dot(u_ref[g], vin_ref[g], preferred_element_type=F32)
            for b in range(bsz):
                for k in range(2):
                    s_scr[g, k, pl.ds(b * pitch, n), :] = s[b * n:(b + 1) * n, k * half:(k + 1) * half]
    lane = lax.broadcasted_iota(jnp.int32, (bsz, half), 1)
    fwd_lane = lane < S5_STATE
    a_re = [a_ref[g, :, :half] for g in range(ng)]
    a_im = [a_ref[g, :, half:] for g in range(ng)]

    def make_step(s_scr, xa_scr, xb_scr, pitch):
        def step(fc, bc, xs):
            rf = pl.ds(fc, bsz, stride=pitch)
            rb = pl.ds(bc, bsz, stride=pitch)
            out = []
            for g in range(ng):
                xr, xi = xs[2 * g], xs[2 * g + 1]
                xa_scr[g, 0, rf, :] = xr
                xa_scr[g, 1, rf, :] = xi
                xb_scr[g, 0, rb, :] = xr
                xb_scr[g, 1, rb, :] = xi
                sr = jnp.where(fwd_lane, s_scr[g, 0, rf, :], s_scr[g, 0, rb, :])
                si = jnp.where(fwd_lane, s_scr[g, 1, rf, :], s_scr[g, 1, rb, :])
                out.append(a_re[g] * xr - a_im[g] * xi + sr)
                out.append(a_re[g] * xi + a_im[g] * xr + si)
            return tuple(out)
        return step

    step_c = make_step(sc_scr, xac_scr, xbc_scr, pc)
    step_l = make_step(sl_scr, xal_scr, xbl_scr, plat)
    xs = tuple(jnp.zeros((bsz, half), F32) for _ in range(2 * ng))
    xs = lax.fori_loop(0, ncc, lambda i, c: step_c(i, ncc - 1 - i, c), xs, unroll=4)
    xs = lax.fori_loop(0, ncl, lambda i, c: step_l(i, ncl - 1 - i, c), xs, unroll=4)
    for u_ref, y_ref, _, xa_scr, xb_scr, n, pitch in segs:
        fwd_r = lax.broadcasted_iota(jnp.int32, (n, half), 1) < S5_STATE
        for g in range(ng):
            rows = []
            for b in range(bsz):
                r = pl.ds(b * pitch, n)
                rows.append(jnp.concatenate([jnp.where(fwd_r, xa_scr[g, k, r, :], xb_scr[g, k, r, :]) for k in range(2)],
                                            axis=1))
            x_in = jnp.concatenate(rows, axis=0).astype(BF16)
            y = (jnp.dot(u_ref[g], kin_ref[g], preferred_element_type=F32)
                 + jnp.dot(x_in, win_ref[g], preferred_element_type=F32))
            y_ref[g] = y.astype(y_ref.dtype)


def _s5(u_ctx, u_lat, kin, vin, win, a_t, l, bsz, ng=4):
    g_n, rc, w = u_ctx.shape
    rl = u_lat.shape[1]
    ncc, ncl = rc // bsz, rl // bsz
    nblk = g_n // ng
    wspec = pl.BlockSpec((ng, w, w), lambda i: (l * nblk + i, 0, 0))
    uspec = lambda r: pl.BlockSpec((ng, r, w), lambda i: (i, 0, 0))
    scr = lambda n: pltpu.VMEM((ng, 2, bsz * _s5_pitch(n), LANES), F32)
    return pl.pallas_call(
        functools.partial(_s5_kernel, ng=ng, ncc=ncc, ncl=ncl, bsz=bsz),
        grid=(nblk,),
        in_specs=[uspec(rc), uspec(rl), wspec, wspec, wspec,
                  pl.BlockSpec((ng, SUBLANES, w), lambda i: (l * nblk + i, 0, 0))],
        out_specs=[uspec(rc), uspec(rl)],
        out_shape=[jax.ShapeDtypeStruct(u_ctx.shape, BF16), jax.ShapeDtypeStruct(u_lat.shape, BF16)],
        scratch_shapes=[scr(ncc), scr(ncl)] * 3,
        compiler_params=_cparams(("parallel",)),
        name="s5_mixer",
    )(u_ctx, u_lat, kin, vin, win, a_t)


def _rotary_tables(seq):
    quarter = RET_DK // 4
    pos = jnp.arange(seq)
    inv_freq = ROPE_BASE ** (-jnp.arange(quarter, dtype=F32) / quarter)
    ang_r = (pos // GRID_W).astype(F32)[:, None] * inv_freq[None, :]
    ang_c = (pos % GRID_W).astype(F32)[:, None] * inv_freq[None, :]
    cos = jnp.concatenate([jnp.cos(ang_r)] * 2 + [jnp.cos(ang_c)] * 2, axis=-1)
    sin = jnp.concatenate([jnp.sin(ang_r)] * 2 + [jnp.sin(ang_c)] * 2, axis=-1)
    cos = jnp.tile(cos, (1, RET_HEADS))
    sin = jnp.tile(sin, (1, RET_HEADS))
    p = np.zeros((RET_QK_WIDTH, RET_QK_WIDTH), np.float32)
    for d in range(RET_QK_WIDTH):
        if d % (2 * quarter) < quarter:
            p[d + quarter, d] = -1.0
        else:
            p[d - quarter, d] = 1.0
    return cos, sin, jnp.asarray(p, BF16)


def _ret_kernel(lg_ref, q_ref, k_ref, v_ref, g_ref, cq_ref, ck_ref, cv_ref, cg_ref,
                cos_ref, sin_ref, p_ref, o_ref, co_ref, krot_scr, sin_scr, dm_scr, *, layer, need_ctx, ncl):
    t = RET_T
    qkw, vw = RET_QK_WIDTH, RET_WIDTH

    def per_head(shape, axis, width, d):
        head = lax.broadcasted_iota(jnp.int32, shape, axis) // width
        out = jnp.zeros(shape, F32)
        for h in range(RET_HEADS):
            out = jnp.where(head == h, lg_ref[layer, d * RET_HEADS + h], out)
        return out

    row = lax.broadcasted_iota(jnp.int32, (t, qkw), 0).astype(F32)
    lgf = per_head((t, qkw), 1, RET_DK, 0)
    lgb = per_head((t, qkw), 1, RET_DK, 1)
    qdec_f = jnp.exp((row + 1.0) * lgf)
    qdec_b = jnp.exp((t - row) * lgb)
    kdec_f = jnp.exp((t - 1.0 - row) * lgf)
    kdec_b = jnp.exp(row * lgb)
    cdec_f = jnp.exp(float(t) * per_head((qkw, vw), 0, RET_DK, 0))
    cdec_b = jnp.exp(float(t) * per_head((qkw, vw), 0, RET_DK, 1))
    blk = (lax.broadcasted_iota(jnp.int32, (qkw, vw), 0) // RET_DK
           == lax.broadcasted_iota(jnp.int32, (qkw, vw), 1) // RET_DV)
    head_lane = lax.broadcasted_iota(jnp.int32, (t, qkw), 1) // RET_DK

    ii = lax.broadcasted_iota(jnp.int32, (t, t), 0)
    jj = lax.broadcasted_iota(jnp.int32, (t, t), 1)
    dif = (ii - jj).astype(F32)
    for h in range(RET_HEADS):
        df = jnp.where(dif >= 0, jnp.exp(jnp.where(dif >= 0, dif, 0.0) * lg_ref[layer, h]), 0.0)
        db = jnp.where(dif < 0, jnp.exp(jnp.where(dif < 0, -dif, 0.0) * lg_ref[layer, RET_HEADS + h]), 0.0)
        dm_scr[h] = df + db

    def rotary(x_bf, c):
        rows = pl.ds(c * t, t)
        swapped = jnp.dot(x_bf, p_ref[...], preferred_element_type=F32)
        return x_bf.astype(F32) * cos_ref[rows, :] + swapped * sin_ref[rows, :]

    krot_scr[pl.ds(0, t), :] = ck_ref[...]
    for c in range(ncl):
        krot_scr[pl.ds((c + 1) * t, t), :] = rotary(k_ref[pl.ds(c * t, t), :], c).astype(BF16)

    def v_chunk(c):
        return cv_ref[...] if c == 0 else v_ref[pl.ds((c - 1) * t, t), :]

    def kv(c, kdec):
        kd = (krot_scr[pl.ds(c * t, t), :].astype(F32) * kdec).astype(BF16)
        return lax.dot_general(kd, v_chunk(c), (((0,), (0,)), ((), ())), preferred_element_type=F32)

    s = jnp.zeros((qkw, vw), F32)
    for c in range(ncl + 1):
        sin_scr[c, pl.ds(0, qkw), :] = jnp.where(blk, s, 0.0).astype(BF16)
        if c < ncl:
            s = cdec_f * s + kv(c, kdec_f)
    sin_scr[0, pl.ds(qkw, qkw), :] = jnp.zeros((qkw, vw), BF16)
    s = kv(0, kdec_b)
    for c in range(ncl, 0, -1):
        sin_scr[c, pl.ds(qkw, qkw), :] = jnp.where(blk, s, 0.0).astype(BF16)
        if c > 1:
            s = cdec_b * s + kv(c, kdec_b)

    for c in range(0 if need_ctx else 1, ncl + 1):
        if c == 0:
            q = cq_ref[...].astype(F32)
            gate = cg_ref[...].astype(F32)
        else:
            q = rotary(q_ref[pl.ds((c - 1) * t, t), :], c - 1)
            gate = g_ref[pl.ds((c - 1) * t, t), :].astype(F32)
        q_bf = q.astype(BF16)
        cross = (jnp.dot((q * qdec_f).astype(BF16), sin_scr[c, pl.ds(0, qkw), :], preferred_element_type=F32)
                 + jnp.dot((q * qdec_b).astype(BF16), sin_scr[c, pl.ds(qkw, qkw), :], preferred_element_type=F32))
        k_c = krot_scr[pl.ds(c * t, t), :]
        v_c = v_chunk(c)
        outs = []
        q_heads = jnp.concatenate([jnp.where(head_lane == h, q_bf, jnp.zeros_like(q_bf)) for h in range(RET_HEADS)],
                                  axis=0)
        sc_heads = lax.dot_general(q_heads, k_c, (((1,), (1,)), ((), ())), preferred_element_type=F32)
        for h in range(RET_HEADS):
            sc = (sc_heads[h * t:(h + 1) * t] * dm_scr[h]).astype(BF16)
            o = (jnp.dot(sc, v_c[:, h * RET_DV:(h + 1) * RET_DV], preferred_element_type=F32)
                 + cross[:, h * RET_DV:(h + 1) * RET_DV])
            mu = jnp.mean(o, axis=-1, keepdims=True)
            var = jnp.mean(jnp.square(o - mu), axis=-1, keepdims=True)
            outs.append((o - mu) * lax.rsqrt(var + GN_EPS))
        y = gate * _sigmoid(gate) * jnp.concatenate(outs, axis=1)
        if c == 0:
            co_ref[...] = y.astype(co_ref.dtype)
        else:
            o_ref[pl.ds((c - 1) * t, t), :] = y.astype(o_ref.dtype)
    if not need_ctx:
        co_ref[...] = jnp.zeros(co_ref.shape, co_ref.dtype)


def _retention(proj_lat, proj_ctx, lg, cos, sin, pmat, l, bsz, seq, n_ctx, need_ctx):
    t = RET_T
    ncl = seq // t
    assert n_ctx == t
    qb, kb = _MY_OFF['rq'] // RET_QK_WIDTH, _MY_OFF['rk'] // RET_QK_WIDTH
    vb, gb = _MY_OFF['rv'] // RET_WIDTH, _MY_OFF['rg'] // RET_WIDTH

    def col(n, w, j):
        return pl.BlockSpec((n, w), lambda b, j=j: (b, j))

    const = lambda shape: pl.BlockSpec(shape, lambda b: (0,) * len(shape))
    return pl.pallas_call(
        functools.partial(_ret_kernel, layer=l, need_ctx=need_ctx, ncl=ncl),
        grid=(bsz,),
        in_specs=[pl.BlockSpec(memory_space=pltpu.SMEM),
                  col(seq, RET_QK_WIDTH, qb), col(seq, RET_QK_WIDTH, kb), col(seq, RET_WIDTH, vb), col(seq, RET_WIDTH, gb),
                  col(n_ctx, RET_QK_WIDTH, qb), col(n_ctx, RET_QK_WIDTH, kb), col(n_ctx, RET_WIDTH, vb), col(n_ctx, RET_WIDTH, gb),
                  const((seq, RET_QK_WIDTH)), const((seq, RET_QK_WIDTH)), const((RET_QK_WIDTH, RET_QK_WIDTH))],
        out_specs=[pl.BlockSpec((seq, RET_WIDTH), lambda b: (b, 0)),
                   pl.BlockSpec((n_ctx, RET_WIDTH), lambda b: (b, 0))],
        out_shape=[jax.ShapeDtypeStruct((bsz * seq, RET_WIDTH), BF16),
                   jax.ShapeDtypeStruct((bsz * n_ctx, RET_WIDTH), BF16)],
        scratch_shapes=[pltpu.VMEM((seq + n_ctx, RET_QK_WIDTH), BF16),
                        pltpu.VMEM((ncl + 1, 2 * RET_QK_WIDTH, RET_WIDTH), BF16),
                        pltpu.VMEM((RET_HEADS, t, t), F32)],
        compiler_params=_cparams(("parallel",)),
        name="retention",
    )(lg, proj_lat, proj_lat, proj_lat, proj_lat, proj_ctx, proj_ctx, proj_ctx, proj_ctx, cos, sin, pmat)


def _na_block_start(kblk, rows):
    return jnp.clip(kblk * NA_QROWS - NA_ROWS // 2, 0, rows - NA_KROWS)


def _na_bias_table(rpb):
    depth, heads, nlag, ncol = rpb.shape
    cols = np.arange(GRID_W)
    cs = np.clip(cols - NA_COLS // 2, 0, GRID_W - NA_COLS)
    valid_c = (cols[None, :] >= cs[:, None]) & (cols[None, :] < cs[:, None] + NA_COLS)
    pad = GRID_W - 1
    padded = jnp.pad(rpb.astype(F32) * LOG2E, ((0, 0), (0, 0), (0, 0), (pad, pad)))
    toe = jnp.stack([padded[..., pad + NA_COLS - 1 - qc: pad + NA_COLS - 1 - qc + GRID_W] for qc in range(GRID_W)],
                    axis=-2)
    toe = jnp.where(jnp.asarray(valid_c), toe, NEG_INF)
    toe = jnp.pad(toe, ((0, 0), (0, 0), (1, NA_NLAG + 1 - nlag - 1), (0, 0), (0, 0)))
    table = jnp.concatenate([toe[:, :, :NA_NLAG], toe[:, :, 1:NA_NLAG + 1]], axis=-1)
    return table.reshape(depth, heads * NA_NLAG, GRID_W, 2 * GRID_W)


def _attend(q_pair, k_list, v_list, bias_list):
    nq = q_pair.shape[0]
    lane = lax.broadcasted_iota(jnp.int32, q_pair.shape, 1) // NA_HEAD_DIM
    zero = jnp.zeros_like(q_pair)
    q2 = jnp.concatenate([jnp.where(lane == 0, q_pair, zero), jnp.where(lane == 1, q_pair, zero)], axis=0)
    ss = []
    for k_i, b_i in zip(k_list, bias_list):
        s = lax.dot_general(q2, k_i, (((1,), (1,)), ((), ())), preferred_element_type=F32)
        if b_i is not None:
            s = s + jnp.concatenate([b_i(0), b_i(1)], axis=0)
        ss.append(s)
    m = ss[0].max(axis=-1, keepdims=True)
    for s in ss[1:]:
        m = jnp.maximum(m, s.max(axis=-1, keepdims=True))
    ps = [jnp.exp2(s - m) for s in ss]
    den = ps[0].sum(axis=-1, keepdims=True)
    for p in ps[1:]:
        den = den + p.sum(axis=-1, keepdims=True)
    acc = jnp.dot(ps[0].astype(BF16), v_list[0], preferred_element_type=F32)
    for p, v_i in zip(ps[1:], v_list[1:]):
        acc = acc + jnp.dot(p.astype(BF16), v_i, preferred_element_type=F32)
    out = acc / den
    return jnp.where(lane == 0, out[:nq], out[nq:])


def _attend_window(q_pair, k_win, k_ctx, v_win, v_ctx, tab_ref, hp, rel, starts):
    nq = q_pair.shape[0]
    npair = k_win.shape[0] // LANES
    lane = lax.broadcasted_iota(jnp.int32, q_pair.shape, 1) // NA_HEAD_DIM
    zero = jnp.zeros_like(q_pair)
    dn = (((1,), (1,)), ((), ()))
    q2 = jnp.concatenate([jnp.where(lane == 0, q_pair, zero), jnp.where(lane == 1, q_pair, zero)], axis=0)
    s_win = lax.dot_general(q2, k_win, dn, preferred_element_type=F32)
    s_ctx = lax.dot_general(q2, k_ctx, dn, preferred_element_type=F32)
    left = lax.broadcasted_iota(jnp.int32, (GRID_W, LANES), 1) < GRID_W
    neg_left = jnp.where(left, NEG_INF, 0.0)
    neg_right = jnp.where(left, 0.0, NEG_INF)
    p_win, p_ctx, inv = [], [], []
    for hh in range(2):
        h = 2 * hp + hh
        for qrl in range(NA_QROWS):
            r = slice(hh * nq + qrl * GRID_W, hh * nq + (qrl + 1) * GRID_W)
            st = starts[qrl]
            lo, hi = st // 2, (st + NA_ROWS + 1) // 2
            tiles = []
            for kp in range(lo, hi):
                lag = rel + 2 * kp - qrl + NA_ROWS - 1
                assert -1 <= lag <= NA_NLAG - 2
                t = tab_ref[0, h * NA_NLAG + lag + 1]
                if 2 * kp < st:
                    t = t + neg_left
                if 2 * kp + 1 >= st + NA_ROWS:
                    t = t + neg_right
                tiles.append(t)
            sw = s_win[r, lo * LANES:hi * LANES] + jnp.concatenate(tiles, axis=1)
            sc = s_ctx[r]
            m = jnp.maximum(sw.max(axis=-1, keepdims=True), sc.max(axis=-1, keepdims=True))
            pw, pc = jnp.exp2(sw - m), jnp.exp2(sc - m)
            inv.append(1.0 / (pw.sum(axis=-1, keepdims=True) + pc.sum(axis=-1, keepdims=True)))
            parts = [pw.astype(BF16)]
            if lo > 0:
                parts.insert(0, jnp.zeros((GRID_W, lo * LANES), BF16))
            if hi < npair:
                parts.append(jnp.zeros((GRID_W, (npair - hi) * LANES), BF16))
            p_win.append(jnp.concatenate(parts, axis=1) if len(parts) > 1 else parts[0])
            p_ctx.append(pc.astype(BF16))
    acc = (jnp.dot(jnp.concatenate(p_win, axis=0), v_win, preferred_element_type=F32)
           + jnp.dot(jnp.concatenate(p_ctx, axis=0), v_ctx, preferred_element_type=F32))
    out = jnp.concatenate([acc[i * GRID_W:(i + 1) * GRID_W] * inv[i] for i in range(len(inv))], axis=0)
    return jnp.where(lane == 0, out[:nq], out[nq:])


def _na_kernel(q_ref, k_ref, v_ref, ck_ref, cv_ref, tab_ref, o_ref, *, rows):
    nk = NA_KROWS * GRID_W
    nblk = rows // NA_QROWS
    kblk = pl.program_id(1)
    ks = pl.multiple_of(_na_block_start(kblk, rows) * GRID_W, NA_QROWS * GRID_W)

    def run(kb):
        r0 = kb * NA_QROWS
        ks_row = min(max(r0 - NA_ROWS // 2, 0), rows - NA_KROWS)
        starts = [min(max(r0 + qrl - NA_ROWS // 2, 0), rows - NA_ROWS) - ks_row for qrl in range(NA_QROWS)]
        for hp in range(NA_HEADS // 2):
            ln = pl.ds(hp * LANES, LANES)
            y = _attend_window(q_ref[:, ln], k_ref[pl.ds(ks, nk), ln], ck_ref[:, ln],
                               v_ref[pl.ds(ks, nk), ln], cv_ref[:, ln], tab_ref, hp, ks_row - r0, starts)
            o_ref[:, ln] = y.astype(o_ref.dtype)

    assert nblk >= 3 and all(_na_geometry(kb, rows) == _na_geometry(1, rows) for kb in range(1, nblk - 1))
    pl.when(kblk == 0)(lambda: run(0))
    pl.when((kblk > 0) & (kblk < nblk - 1))(lambda: run(1))
    pl.when(kblk == nblk - 1)(lambda: run(nblk - 1))


def _na_geometry(kb, rows):
    r0 = kb * NA_QROWS
    ks_row = min(max(r0 - NA_ROWS // 2, 0), rows - NA_KROWS)
    return (ks_row - r0,) + tuple(min(max(r0 + qrl - NA_ROWS // 2, 0), rows - NA_ROWS) - ks_row
                                  for qrl in range(NA_QROWS))


def _na(proj_lat, proj_ctx, table, l, bsz, seq, n_ctx):
    rows = seq // GRID_W
    nq = NA_QROWS * GRID_W
    nblk = seq // nq
    qb, kb, vb = (_MY_OFF[n] // NA_WIDTH for n in ('nq', 'nk', 'nv'))
    return pl.pallas_call(
        functools.partial(_na_kernel, rows=rows),
        grid=(bsz, nblk),
        in_specs=[pl.BlockSpec((nq, NA_WIDTH), lambda b, k: (b * nblk + k, qb)),
                  pl.BlockSpec((seq, NA_WIDTH), lambda b, k: (b, kb)),
                  pl.BlockSpec((seq, NA_WIDTH), lambda b, k: (b, vb)),
                  pl.BlockSpec((n_ctx, NA_WIDTH), lambda b, k: (b, kb)),
                  pl.BlockSpec((n_ctx, NA_WIDTH), lambda b, k: (b, vb)),
                  pl.BlockSpec((1,) + table.shape[1:], lambda b, k: (l, 0, 0, 0))],
        out_specs=pl.BlockSpec((nq, NA_WIDTH), lambda b, k: (b * nblk + k, 0)),
        out_shape=jax.ShapeDtypeStruct((bsz * seq, NA_WIDTH), BF16),
        compiler_params=_cparams(("parallel", "arbitrary")),
        name="neighborhood_attention",
    )(proj_lat, proj_lat, proj_lat, proj_ctx, proj_ctx, table)


def _ctx_attn_kernel(q_ref, k_ref, v_ref, o_ref):
    for hp in range(NA_HEADS // 2):
        ln = pl.ds(hp * LANES, LANES)
        y = _attend(q_ref[:, ln], [k_ref[:, ln]], [v_ref[:, ln]], [None])
        o_ref[:, ln] = y.astype(o_ref.dtype)


def _ctx_attn(proj_ctx, bsz, n_ctx):
    qb, kb, vb = (_MY_OFF[n] // NA_WIDTH for n in ('nq', 'nk', 'nv'))
    spec = lambda j: pl.BlockSpec((n_ctx, NA_WIDTH), lambda b: (b, j))
    return pl.pallas_call(
        _ctx_attn_kernel,
        grid=(bsz,),
        in_specs=[spec(qb), spec(kb), spec(vb)],
        out_specs=pl.BlockSpec((n_ctx, NA_WIDTH), lambda b: (b, 0)),
        out_shape=jax.ShapeDtypeStruct((bsz * n_ctx, NA_WIDTH), BF16),
        compiler_params=_cparams(("parallel",)),
        name="context_attention",
    )(proj_ctx, proj_ctx, proj_ctx)


def _gelu_tanh(x):
    return 0.5 * x * (1.0 + jnp.tanh(math.sqrt(2.0 / math.pi) * (x + 0.044715 * (x * x * x))))


def _merge_kernel(x_ref, g0_ref, g1_ref, g2_ref, yt_ref, yret_ref, yna_ref, wglu_ref, bglu_ref,
                  wbs5_ref, wbret_ref, wbna_ref, wout_ref, gate_ref, o_ref, w_scr):
    nrow = yt_ref.shape[1]
    for j in range(_NSLAB):
        for half in range(2):
            o = [yt_ref[j * _GPL + q, :, half * LANES:(half + 1) * LANES].astype(F32) for q in range(_GPL)]
            for tl, out in enumerate(_block_transpose(o)):
                w_scr[j, pl.ds(half * _GPL + tl, nrow, stride=S5_CHUNK), :] = out
    ge = _gelu_tanh(jnp.concatenate([w_scr[j] for j in range(_NSLAB)], axis=1))
    z = jnp.dot(ge.astype(BF16), wglu_ref[0], preferred_element_type=F32) + bglu_ref[0]
    s5 = (ge * _sigmoid(z)).astype(BF16)
    m = (_sigmoid(g0_ref[...].astype(F32)) * jnp.dot(s5, wbs5_ref[0], preferred_element_type=F32)
         + _sigmoid(g1_ref[...].astype(F32)) * jnp.dot(yret_ref[...], wbret_ref[0], preferred_element_type=F32)
         + _sigmoid(g2_ref[...].astype(F32)) * jnp.dot(yna_ref[...], wbna_ref[0], preferred_element_type=F32))
    o_ref[...] = x_ref[...] + gate_ref[0] * jnp.dot(m.astype(BF16), wout_ref[0], preferred_element_type=F32)


def _merge(x2, proj, y_t, yret, yna, mods, mod_row, l, wts, tm):
    m, d = x2.shape
    base = l * MOD_ROWS * 6
    rowblk = lambda w, j=0: pl.BlockSpec((tm, w), lambda i, j=j: (i, j))
    return pl.pallas_call(
        _merge_kernel,
        grid=(m // tm,),
        in_specs=[rowblk(d), rowblk(d, 0), rowblk(d, 1), rowblk(d, 2),
                  pl.BlockSpec((S5_GROUPS, tm // S5_CHUNK, S5_CHUNK * S5_GROUP), lambda i: (0, i, 0)),
                  rowblk(RET_WIDTH), rowblk(NA_WIDTH)]
                 + [_layer_spec(w, l) for w in wts]
                 + [pl.BlockSpec((1, 1, d), lambda i: (base + mod_row(i) * 6 + 2, 0, 0))],
        out_specs=rowblk(d),
        out_shape=jax.ShapeDtypeStruct((m, d), F32),
        scratch_shapes=[pltpu.VMEM((_NSLAB, tm, LANES), F32)],
        compiler_params=_cparams(("parallel",)),
        name="merge_residual",
    )(x2, proj, proj, proj, y_t, yret, yna, *wts, mods)


def _ffn_kernel(x_ref, sh_ref, sc_ref, gate_ref, wg_ref, wu_ref, wd_ref, fn_ref, o_ref, *, final, th):
    x = x_ref[...]
    h = (_rms(x) * (1.0 + sc_ref[0]) + sh_ref[0]).astype(BF16)
    hidden = wg_ref.shape[2]
    acc = jnp.zeros(x.shape, F32)
    for j in range(hidden // th):
        a = jnp.dot(h, wg_ref[0, :, j * th:(j + 1) * th], preferred_element_type=F32)
        b = jnp.dot(h, wu_ref[0, :, j * th:(j + 1) * th], preferred_element_type=F32)
        act = (a * _sigmoid(a) * b).astype(BF16)
        acc = acc + jnp.dot(act, wd_ref[0, j * th:(j + 1) * th, :], preferred_element_type=F32)
    y = x + gate_ref[0] * acc
    if final:
        y = _rms(y) * fn_ref[...]
    o_ref[...] = y


def _ffn(x2, mods, mod_row, l, wg, wu, wd, fn, tm, final):
    m, d = x2.shape
    base = l * MOD_ROWS * 6
    modspec = lambda k: pl.BlockSpec((1, 1, d), lambda i, k=k: (base + mod_row(i) * 6 + k, 0, 0))
    return pl.pallas_call(
        functools.partial(_ffn_kernel, final=final, th=256),
        grid=(m // tm,),
        in_specs=[pl.BlockSpec((tm, d), lambda i: (i, 0)), modspec(3), modspec(4), modspec(5),
                  _layer_spec(wg, l), _layer_spec(wu, l), _layer_spec(wd, l),
                  pl.BlockSpec(fn.shape, lambda i: (0, 0))],
        out_specs=pl.BlockSpec((tm, d), lambda i: (i, 0)),
        out_shape=jax.ShapeDtypeStruct((m, d), F32),
        compiler_params=_cparams(("parallel",)),
        name="swiglu_residual",
    )(x2, mods, mods, mods, wg, wu, wd, fn)


def kernel(x, c, ctx, c_ctx, w_ada, b_ada, w_in, s5_lam_re, s5_lam_im, s5_log_dt, s5_b_re, s5_b_im,
           s5_c_re, s5_c_im, s5_d, s5_w_glu, s5_b_glu, ret_theta, na_rpb, w_branch_s5, w_branch_ret,
           w_branch_na, w_out, w_ffn_gate, w_ffn_up, w_ffn_down, final_norm):
    bsz, seq, d = x.shape
    n_ctx = ctx.shape[1]
    depth = w_ada.shape[0]
    ctx_row = bsz
    assert bsz == SUBLANES and bsz + 1 <= MOD_ROWS

    cvec = jnp.zeros((MOD_ROWS, d), F32).at[:bsz].set(c).at[ctx_row].set(c_ctx)
    mods = _ada(cvec, w_ada, b_ada).reshape(depth * MOD_ROWS * 6, 1, d)

    w_in_k = jnp.concatenate(
        [w_in[:, :, _REF_OFF[n]:_REF_OFF[n] + _REF_W[n]] * _COL_SCALE.get(n, 1.0) for n in _MY_ORDER],
        axis=2).astype(BF16)

    cos, sin, pmat = _rotary_tables(seq)
    log_gamma = jax.nn.log_sigmoid(ret_theta.astype(F32)).reshape(depth, 2 * RET_HEADS)
    kin, vin, win, a_t = _s5_weights(s5_lam_re, s5_lam_im, s5_log_dt, s5_b_re, s5_b_im, s5_c_re, s5_c_im, s5_d)
    na_table = _na_bias_table(na_rpb)
    merge_w = (s5_w_glu.astype(BF16), s5_b_glu.reshape(depth, 1, -1).astype(F32), w_branch_s5.astype(BF16),
               w_branch_ret.astype(BF16), w_branch_na.astype(BF16), w_out.astype(BF16))
    ffn_w = (w_ffn_gate.astype(BF16), w_ffn_up.astype(BF16), w_ffn_down.astype(BF16),
             final_norm.reshape(1, d).astype(F32))

    tm_proj, tn_proj, tm = 1024, N_IN // 2, 512
    lat_row = lambda t: (lambda i: i // (seq // t))
    ctx_mod_row = lambda i: ctx_row

    x2 = x.reshape(bsz * seq, d)
    c2 = ctx.reshape(bsz * n_ctx, d)
    for l in range(depth):
        need_ctx = l < depth - 1
        proj_lat, u_lat = _inproj(x2, mods, lat_row(tm_proj), w_in_k, l, tm_proj, tn_proj)
        proj_ctx, u_ctx = _inproj(c2, mods, ctx_mod_row, w_in_k, l, tm_proj, tn_proj)

        ys5_ctx, ys5_lat = _s5(u_ctx, u_lat, kin, vin, win, a_t, l, bsz)
        yret_lat, yret_ctx = _retention(proj_lat, proj_ctx, log_gamma, cos, sin, pmat, l, bsz, seq, n_ctx, need_ctx)
        yna_lat = _na(proj_lat, proj_ctx, na_table, l, bsz, seq, n_ctx)

        x2 = _merge(x2, proj_lat, ys5_lat, yret_lat, yna_lat, mods, lat_row(tm), l, merge_w, tm)
        x2 = _ffn(x2, mods, lat_row(tm), l, *ffn_w, tm, final=not need_ctx)
        if need_ctx:
            yna_ctx = _ctx_attn(proj_ctx, bsz, n_ctx)
            c2 = _merge(c2, proj_ctx, ys5_ctx, yret_ctx, yna_ctx, mods, ctx_mod_row, l, merge_w, tm)
            c2 = _ffn(c2, mods, ctx_mod_row, l, *ffn_w, tm, final=False)
    return x2.reshape(bsz, seq, d)
```

```python
import functools
import math

import numpy as np
import jax
import jax.numpy as jnp
from jax import lax
from jax.experimental import pallas as pl
from jax.experimental.pallas import tpu as pltpu

F32 = jnp.float32
BF16 = jnp.bfloat16
HIGHEST = lax.Precision.HIGHEST

D_MODEL = 1024
GRID_W = 64
S5_WIDTH = 512
S5_GROUP = 16
S5_GROUPS = S5_WIDTH // S5_GROUP
S5_STATE = 64
S5_CHUNK = 16
RET_HEADS = 4
RET_DK = 64
RET_DV = 128
RET_QK_WIDTH = RET_HEADS * RET_DK
RET_WIDTH = RET_HEADS * RET_DV
RET_T = 256
NA_HEADS = 8
NA_HEAD_DIM = 64
NA_WIDTH = NA_HEADS * NA_HEAD_DIM
NA_ROWS = 8
NA_COLS = 16
NA_QROWS = 4
NA_KROWS = 12
NA_NLAG = 2 * NA_ROWS
N_BRANCH = 3
ROPE_BASE = 10000.0
RMS_EPS = 1e-6
GN_EPS = 1e-5
NEG_INF = -1e30
LANES = 128
SUBLANES = 8
MOD_ROWS = 16

_REF_SPLIT = (S5_WIDTH, RET_QK_WIDTH, RET_WIDTH, NA_WIDTH, NA_WIDTH,
              RET_QK_WIDTH, RET_WIDTH, NA_WIDTH, N_BRANCH * D_MODEL)
_REF_NAMES = ('u', 'rk', 'rv', 'nk', 'nv', 'rq', 'rg', 'nq', 'gates')
_REF_OFF = dict(zip(_REF_NAMES, np.concatenate([[0], np.cumsum(_REF_SPLIT)[:-1]]).tolist()))
_REF_W = dict(zip(_REF_NAMES, _REF_SPLIT))
_MY_ORDER = ('gates', 'rq', 'rk', 'u', 'rv', 'nk', 'nv', 'rg', 'nq')
_CTX_KV_BLOCK = 1
_MY_OFF = {}
_o = 0
for _n in _MY_ORDER:
    _MY_OFF[_n] = _o
    _o += _REF_W[_n]
N_IN = _o
LOG2E = math.log2(math.e)
_COL_SCALE = {'nq': NA_HEAD_DIM ** -0.5 * LOG2E, 'rk': RET_DK ** -0.5}

VMEM_LIMIT = 56 * 1024 * 1024


def _cparams(sem):
    return pltpu.CompilerParams(dimension_semantics=sem, vmem_limit_bytes=VMEM_LIMIT)


def _sigmoid(x):
    return 1.0 / (1.0 + jnp.exp(-x))


def _rms(x):
    return x * lax.rsqrt(jnp.mean(x * x, axis=-1, keepdims=True) + RMS_EPS)


def _layer_spec(arr, l):
    nd = arr.ndim
    return pl.BlockSpec((1,) + arr.shape[1:], lambda *_: (l,) + (0,) * (nd - 1))


def _ada_kernel(c_ref, w_ref, b_ref, o_ref):
    c = c_ref[...]
    s = c * _sigmoid(c)
    o_ref[0] = jnp.dot(s, w_ref[0], preferred_element_type=F32, precision=HIGHEST) + b_ref[0]


def _ada(cvec, w_ada, b_ada):
    depth, d, n = w_ada.shape
    tn = 1536
    rows = cvec.shape[0]
    return pl.pallas_call(
        _ada_kernel,
        grid=(depth, n // tn),
        in_specs=[pl.BlockSpec((rows, d), lambda l, j: (0, 0)),
                  pl.BlockSpec((1, d, tn), lambda l, j: (l, 0, j)),
                  pl.BlockSpec((1, 1, tn), lambda l, j: (l, 0, j))],
        out_specs=pl.BlockSpec((1, rows, tn), lambda l, j: (l, 0, j)),
        out_shape=jax.ShapeDtypeStruct((depth, rows, n), F32),
        compiler_params=_cparams(("parallel", "parallel")),
        name="ada_mod",
    )(cvec, w_ada, b_ada.reshape(depth, 1, n))


_GPL = LANES // S5_GROUP
_NSLAB = S5_WIDTH // LANES


def _block_transpose(tiles):
    blk = lax.broadcasted_iota(jnp.int32, tiles[0].shape, 1) // S5_GROUP
    tiles = list(tiles)
    s = _GPL // 2
    while s >= 1:
        hi = (blk & s) != 0
        for a in range(_GPL):
            if a & s:
                continue
            b = a + s
            ta, tb = tiles[a], tiles[b]
            tiles[a] = jnp.where(hi, pltpu.roll(tb, s * S5_GROUP, 1), ta)
            tiles[b] = jnp.where(hi, tb, pltpu.roll(ta, LANES - s * S5_GROUP, 1))
        s //= 2
    return tiles


def _inproj_kernel(x_ref, sh_ref, sc_ref, w_ref, o_ref, ut_ref, h_ref, u_scr, *, u_off):
    @pl.when(pl.program_id(1) == 0)
    def _():
        h = _rms(x_ref[...]) * (1.0 + sc_ref[0]) + sh_ref[0]
        h_ref[...] = h.astype(BF16)

    res = jnp.dot(h_ref[...], w_ref[0], preferred_element_type=F32)
    o_ref[...] = res.astype(o_ref.dtype)
    for j in range(_NSLAB):
        u_scr[j] = res[:, u_off + j * LANES:u_off + (j + 1) * LANES]
    nrow = u_scr.shape[1] // S5_CHUNK
    for j in range(_NSLAB):
        for half in range(2):
            v = [u_scr[j, pl.ds(half * _GPL + tl, nrow, stride=S5_CHUNK), :] for tl in range(_GPL)]
            for q, out in enumerate(_block_transpose(v)):
                ut_ref[j * _GPL + q, :, half * LANES:(half + 1) * LANES] = out.astype(ut_ref.dtype)


def _inproj(x2, mods, mod_row, w, l, tm, tn, first_block=0):
    m, d = x2.shape
    n = w.shape[2]
    base = l * MOD_ROWS * 6
    nj = n // tn
    u_off = _MY_OFF['u'] - (nj - 1) * tn
    assert 0 <= u_off and u_off + S5_WIDTH <= tn and 0 <= first_block < nj
    return pl.pallas_call(
        functools.partial(_inproj_kernel, u_off=u_off),
        grid=(m // tm, nj - first_block),
        in_specs=[pl.BlockSpec((tm, d), lambda i, j: (i, 0)),
                  pl.BlockSpec((1, 1, d), lambda i, j: (base + mod_row(i) * 6 + 0, 0, 0)),
                  pl.BlockSpec((1, 1, d), lambda i, j: (base + mod_row(i) * 6 + 1, 0, 0)),
                  pl.BlockSpec((1, d, tn), lambda i, j: (l, 0, first_block + j))],
        out_specs=[pl.BlockSpec((tm, tn), lambda i, j: (i, first_block + j)),
                   pl.BlockSpec((S5_GROUPS, tm // S5_CHUNK, S5_CHUNK * S5_GROUP), lambda i, j: (0, i, 0))],
        out_shape=[jax.ShapeDtypeStruct((m, n), BF16),
                   jax.ShapeDtypeStruct((S5_GROUPS, m // S5_CHUNK, S5_CHUNK * S5_GROUP), BF16)],
        scratch_shapes=[pltpu.VMEM((tm, d), BF16), pltpu.VMEM((_NSLAB, tm, LANES), F32)],
        compiler_params=_cparams(("parallel", "arbitrary")),
        name="in_proj",
    )(x2, mods, mods, w)


def _s5w_kernel(lam_ref, btr_ref, bti_ref, ctr_ref, cti_ref, dd_ref, kin_ref, vin_ref, win_ref, at_ref, *, ng):
    t_n, h_n = S5_CHUNK, S5_GROUP
    width = t_n * h_n
    lane = lax.broadcasted_iota(jnp.int32, (1, LANES), 1)
    f_lane = lane < S5_STATE
    tau = lax.broadcasted_iota(jnp.int32, (3 * SUBLANES, LANES), 0).astype(F32)
    lane_w = lax.broadcasted_iota(jnp.int32, (h_n, width), 1)
    for g in range(ng):
        lam_re, lam_im, dt = lam_ref[g, 0:1, :], lam_ref[g, 1:2, :], lam_ref[g, 2:3, :]
        mag = jnp.exp(tau * (lam_re * dt))
        ang = tau * (lam_im * dt)
        pr, pi = mag * jnp.cos(ang), mag * jnp.sin(ang)
        ab_re, ab_im = pr[1:2], pi[1:2]
        den = lam_re * lam_re + lam_im * lam_im
        f_re = ((ab_re - 1.0) * lam_re + ab_im * lam_im) / den
        f_im = (ab_im * lam_re - (ab_re - 1.0) * lam_im) / den
        btr, bti = btr_ref[g], bti_ref[g]
        bbr = f_re * btr - f_im * bti
        bbi = f_re * bti + f_im * btr
        ctr, cti = ctr_ref[g], cti_ref[g]

        def powers(pf, pb):
            rr = [jnp.broadcast_to(jnp.where(f_lane, pr[pf[t]:pf[t] + 1], pr[pb[t]:pb[t] + 1]), (h_n, LANES))
                  for t in range(t_n)]
            ri = [jnp.broadcast_to(jnp.where(f_lane, pi[pf[t]:pf[t] + 1], pi[pb[t]:pb[t] + 1]), (h_n, LANES))
                  for t in range(t_n)]
            return jnp.concatenate(rr, axis=0), jnp.concatenate(ri, axis=0)

        tile = lambda a: jnp.concatenate([a] * t_n, axis=0)
        bbr_t, bbi_t, ctr_t, cti_t = tile(bbr), tile(bbi), tile(ctr), tile(cti)

        xr, xi = powers([t_n - 1 - t for t in range(t_n)], list(range(t_n)))
        vin = jnp.concatenate([xr * bbr_t - xi * bbi_t, xr * bbi_t + xi * bbr_t], axis=1)
        vin_ref[g] = vin.astype(vin_ref.dtype)

        yr, yi = powers([t + 1 for t in range(t_n)], [t_n - t for t in range(t_n)])
        win_t = jnp.concatenate([ctr_t * yr - cti_t * yi, -(ctr_t * yi + cti_t * yr)], axis=1)
        win_ref[g] = win_t.T.astype(win_ref.dtype)

        zr, zi = powers(list(range(t_n)), [t_n - 1 - t for t in range(t_n)])
        fmat = jnp.concatenate([ctr_t * zr - cti_t * zi, ctr_t * zi + cti_t * zr], axis=1)
        dn = (((1,), (1,)), ((), ()))
        lhs_f = jnp.concatenate([jnp.where(f_lane, bbr, 0.0), jnp.where(f_lane, -bbi, 0.0)], axis=1)
        lhs_b = jnp.concatenate([jnp.where(f_lane, 0.0, bbr), jnp.where(f_lane, 0.0, -bbi)], axis=1)
        w_f = lax.dot_general(lhs_f, fmat, dn, preferred_element_type=F32, precision=HIGHEST)
        w_b = lax.dot_general(lhs_b, fmat, dn, preferred_element_type=F32, precision=HIGHEST)
        w_f = w_f + jnp.concatenate([dd_ref[g], jnp.zeros((h_n, width - LANES), F32)], axis=1)
        blocks = []
        for t in range(t_n):
            sh_f = t * h_n
            sh_b = (width - (t_n - 1 - t) * h_n) % width
            fw = w_f if sh_f == 0 else pltpu.roll(w_f, sh_f, 1)
            bw = w_b if sh_b == 0 else pltpu.roll(w_b, sh_b, 1)
            blocks.append(jnp.where(lane_w >= t * h_n, fw, 0.0) + jnp.where(lane_w < (t + 1) * h_n, bw, 0.0))
        kin_ref[g] = jnp.concatenate(blocks, axis=0).astype(kin_ref.dtype)
        at_ref[g] = jnp.broadcast_to(jnp.concatenate([pr[t_n:t_n + 1], pi[t_n:t_n + 1]], axis=1),
                                     (SUBLANES, 2 * LANES))


def _s5_weights(lam_re, lam_im, log_dt, b_re, b_im, c_re, c_im, d_skip, ng=4):
    depth, _, g_n, p_n = lam_re.shape
    h_n = S5_GROUP
    n = depth * g_n
    width = S5_CHUNK * h_n
    pair = lambda a: a.astype(F32).transpose(0, 2, 1, 3).reshape(n, 1, 2 * p_n)
    dt = jnp.broadcast_to(jnp.exp(log_dt.astype(F32))[..., None], lam_re.shape)
    lam = jnp.concatenate([pair(lam_re), pair(lam_im), pair(dt), jnp.zeros((n, SUBLANES - 3, 2 * p_n), F32)], axis=1)
    bt = lambda a: a.astype(F32).transpose(0, 2, 4, 1, 3).reshape(n, h_n, 2 * p_n)
    ct = lambda a: a.astype(F32).transpose(0, 2, 3, 1, 4).reshape(n, h_n, 2 * p_n)
    dd = jnp.eye(h_n, LANES, dtype=F32)[None] * d_skip.astype(F32).reshape(n, h_n, 1)
    small = pl.BlockSpec((ng, h_n, LANES), lambda i: (i, 0, 0))
    big = pl.BlockSpec((ng, width, width), lambda i: (i, 0, 0))
    return pl.pallas_call(
        functools.partial(_s5w_kernel, ng=ng),
        grid=(n // ng,),
        in_specs=[pl.BlockSpec((ng, SUBLANES, LANES), lambda i: (i, 0, 0)), small, small, small, small, small],
        out_specs=[big, big, big, pl.BlockSpec((ng, SUBLANES, width), lambda i: (i, 0, 0))],
        out_shape=[jax.ShapeDtypeStruct((n, width, width), BF16)] * 3
                  + [jax.ShapeDtypeStruct((n, SUBLANES, width), F32)],
        compiler_params=_cparams(("parallel",)),
        name="s5_weights",
    )(lam, bt(b_re), bt(b_im), ct(c_re), ct(c_im), dd)


def _s5_pitch(n):
    p = -(-n // SUBLANES)
    return (p | 1) * SUBLANES


def _s5_kernel(uc_ref, ul_ref, kin_ref, vin_ref, win_ref, a_ref, yc_ref, yl_ref,
               sc_scr, sl_scr, xac_scr, xal_scr, xbc_scr, xbl_scr, *, ng, ncc, ncl, bsz):
    half = 2 * S5_STATE
    pc, plat = _s5_pitch(ncc), _s5_pitch(ncl)
    segs = ((uc_ref, yc_ref, sc_scr, xac_scr, xbc_scr, ncc, pc), (ul_ref, yl_ref, sl_scr, xal_scr, xbl_scr, ncl, plat))
    for u_ref, _, s_scr, _, _, n, pitch in segs:
        for g in range(ng):
            s = jnp.dot(u_ref[g], vin_ref[g], preferred_element_type=F32)
            for b in range(bsz):
                for k in range(2):
                    s_scr[g, k, pl.ds(b * pitch, n), :] = s[b * n:(b + 1) * n, k * half:(k + 1) * half]
    lane = lax.broadcasted_iota(jnp.int32, (bsz, half), 1)
    fwd_lane = lane < S5_STATE
    a_re = [a_ref[g, :, :half] for g in range(ng)]
    a_im = [a_ref[g, :, half:] for g in range(ng)]

    def make_step(s_scr, xa_scr, xb_scr, pitch):
        def step(fc, bc, xs):
            rf = pl.ds(fc, bsz, stride=pitch)
            rb = pl.ds(bc, bsz, stride=pitch)
            out = []
            for g in range(ng):
                xr, xi = xs[2 * g], xs[2 * g + 1]
                xa_scr[g, 0, rf, :] = xr
                xa_scr[g, 1, rf, :] = xi
                xb_scr[g, 0, rb, :] = xr
                xb_scr[g, 1, rb, :] = xi
                sr = jnp.where(fwd_lane, s_scr[g, 0, rf, :], s_scr[g, 0, rb, :])
                si = jnp.where(fwd_lane, s_scr[g, 1, rf, :], s_scr[g, 1, rb, :])
                out.append(a_re[g] * xr - a_im[g] * xi + sr)
                out.append(a_re[g] * xi + a_im[g] * xr + si)
            return tuple(out)
        return step

    step_c = make_step(sc_scr, xac_scr, xbc_scr, pc)
    step_l = make_step(sl_scr, xal_scr, xbl_scr, plat)
    xs = tuple(jnp.zeros((bsz, half), F32) for _ in range(2 * ng))
    xs = lax.fori_loop(0, ncc, lambda i, c: step_c(i, ncc - 1 - i, c), xs, unroll=4)
    xs = lax.fori_loop(0, ncl, lambda i, c: step_l(i, ncl - 1 - i, c), xs, unroll=4)
    for u_ref, y_ref, _, xa_scr, xb_scr, n, pitch in segs:
        fwd_r = lax.broadcasted_iota(jnp.int32, (n, half), 1) < S5_STATE
        for g in range(ng):
            rows = []
            for b in range(bsz):
                r = pl.ds(b * pitch, n)
                rows.append(jnp.concatenate([jnp.where(fwd_r, xa_scr[g, k, r, :], xb_scr[g, k, r, :]) for k in range(2)],
                                            axis=1))
            x_in = jnp.concatenate(rows, axis=0).astype(BF16)
            y = (jnp.dot(u_ref[g], kin_ref[g], preferred_element_type=F32)
                 + jnp.dot(x_in, win_ref[g], preferred_element_type=F32))
            y_ref[g] = y.astype(y_ref.dtype)


def _s5(u_ctx, u_lat, kin, vin, win, a_t, l, bsz, ng=4):
    g_n, rc, w = u_ctx.shape
    rl = u_lat.shape[1]
    ncc, ncl = rc // bsz, rl // bsz
    nblk = g_n // ng
    wspec = pl.BlockSpec((ng, w, w), lambda i: (l * nblk + i, 0, 0))
    uspec = lambda r: pl.BlockSpec((ng, r, w), lambda i: (i, 0, 0))
    scr = lambda n: pltpu.VMEM((ng, 2, bsz * _s5_pitch(n), LANES), F32)
    return pl.pallas_call(
        functools.partial(_s5_kernel, ng=ng, ncc=ncc, ncl=ncl, bsz=bsz),
        grid=(nblk,),
        in_specs=[uspec(rc), uspec(rl), wspec, wspec, wspec,
                  pl.BlockSpec((ng, SUBLANES, w), lambda i: (l * nblk + i, 0, 0))],
        out_specs=[uspec(rc), uspec(rl)],
        out_shape=[jax.ShapeDtypeStruct(u_ctx.shape, BF16), jax.ShapeDtypeStruct(u_lat.shape, BF16)],
        scratch_shapes=[scr(ncc), scr(ncl)] * 3,
        compiler_params=_cparams(("parallel",)),
        name="s5_mixer",
    )(u_ctx, u_lat, kin, vin, win, a_t)


def _rotary_tables(seq):
    quarter = RET_DK // 4
    pos = jnp.arange(seq)
    inv_freq = ROPE_BASE ** (-jnp.arange(quarter, dtype=F32) / quarter)
    ang_r = (pos // GRID_W).astype(F32)[:, None] * inv_freq[None, :]
    ang_c = (pos % GRID_W).astype(F32)[:, None] * inv_freq[None, :]
    cos = jnp.concatenate([jnp.cos(ang_r)] * 2 + [jnp.cos(ang_c)] * 2, axis=-1)
    sin = jnp.concatenate([jnp.sin(ang_r)] * 2 + [jnp.sin(ang_c)] * 2, axis=-1)
    cos = jnp.tile(cos, (1, RET_HEADS))
    sin = jnp.tile(sin, (1, RET_HEADS))
    p = np.zeros((RET_QK_WIDTH, RET_QK_WIDTH), np.float32)
    for d in range(RET_QK_WIDTH):
        if d % (2 * quarter) < quarter:
            p[d + quarter, d] = -1.0
        else:
            p[d - quarter, d] = 1.0
    return cos, sin, jnp.asarray(p, BF16)


def _ret_kernel(lg_ref, q_ref, k_ref, v_ref, g_ref, cq_ref, ck_ref, cv_ref, cg_ref,
                cos_ref, sin_ref, p_ref, o_ref, co_ref, krot_scr, sin_scr, dm_scr, *, layer, need_ctx, ncl):
    t = RET_T
    qkw, vw = RET_QK_WIDTH, RET_WIDTH

    def per_head(shape, axis, width, d):
        head = lax.broadcasted_iota(jnp.int32, shape, axis) // width
        out = jnp.zeros(shape, F32)
        for h in range(RET_HEADS):
            out = jnp.where(head == h, lg_ref[layer, d * RET_HEADS + h], out)
        return out

    row = lax.broadcasted_iota(jnp.int32, (t, qkw), 0).astype(F32)
    lgf = per_head((t, qkw), 1, RET_DK, 0)
    lgb = per_head((t, qkw), 1, RET_DK, 1)
    qdec_f = jnp.exp((row + 1.0) * lgf)
    qdec_b = jnp.exp((t - row) * lgb)
    kdec_f = jnp.exp((t - 1.0 - row) * lgf)
    kdec_b = jnp.exp(row * lgb)
    cdec_f = jnp.exp(float(t) * per_head((qkw, vw), 0, RET_DK, 0))
    cdec_b = jnp.exp(float(t) * per_head((qkw, vw), 0, RET_DK, 1))
    blk = (lax.broadcasted_iota(jnp.int32, (qkw, vw), 0) // RET_DK
           == lax.broadcasted_iota(jnp.int32, (qkw, vw), 1) // RET_DV)
    head_lane = lax.broadcasted_iota(jnp.int32, (t, qkw), 1) // RET_DK

    ii = lax.broadcasted_iota(jnp.int32, (t, t), 0)
    jj = lax.broadcasted_iota(jnp.int32, (t, t), 1)
    dif = (ii - jj).astype(F32)
    for h in range(RET_HEADS):
        df = jnp.where(dif >= 0, jnp.exp(jnp.where(dif >= 0, dif, 0.0) * lg_ref[layer, h]), 0.0)
        db = jnp.where(dif < 0, jnp.exp(jnp.where(dif < 0, -dif, 0.0) * lg_ref[layer, RET_HEADS + h]), 0.0)
        dm_scr[h] = df + db

    def rotary(x_bf, c):
        rows = pl.ds(c * t, t)
        swapped = jnp.dot(x_bf, p_ref[...], preferred_element_type=F32)
        return x_bf.astype(F32) * cos_ref[rows, :] + swapped * sin_ref[rows, :]

    krot_scr[pl.ds(0, t), :] = ck_ref[...]
    for c in range(ncl):
        krot_scr[pl.ds((c + 1) * t, t), :] = rotary(k_ref[pl.ds(c * t, t), :], c).astype(BF16)

    def v_chunk(c):
        return cv_ref[...] if c == 0 else v_ref[pl.ds((c - 1) * t, t), :]

    def kv(c, kdec):
        kd = (krot_scr[pl.ds(c * t, t), :].astype(F32) * kdec).astype(BF16)
        return lax.dot_general(kd, v_chunk(c), (((0,), (0,)), ((), ())), preferred_element_type=F32)

    s = jnp.zeros((qkw, vw), F32)
    for c in range(ncl + 1):
        sin_scr[c, pl.ds(0, qkw), :] = jnp.where(blk, s, 0.0).astype(BF16)
        if c < ncl:
            s = cdec_f * s + kv(c, kdec_f)
    sin_scr[0, pl.ds(qkw, qkw), :] = jnp.zeros((qkw, vw), BF16)
    s = kv(0, kdec_b)
    for c in range(ncl, 0, -1):
        sin_scr[c, pl.ds(qkw, qkw), :] = jnp.where(blk, s, 0.0).astype(BF16)
        if c > 1:
            s = cdec_b * s + kv(c, kdec_b)

    for c in range(0 if need_ctx else 1, ncl + 1):
        if c == 0:
            q = cq_ref[...].astype(F32)
            gate = cg_ref[...].astype(F32)
        else:
            q = rotary(q_ref[pl.ds((c - 1) * t, t), :], c - 1)
            gate = g_ref[pl.ds((c - 1) * t, t), :].astype(F32)
        q_bf = q.astype(BF16)
        cross = (jnp.dot((q * qdec_f).astype(BF16), sin_scr[c, pl.ds(0, qkw), :], preferred_element_type=F32)
                 + jnp.dot((q * qdec_b).astype(BF16), sin_scr[c, pl.ds(qkw, qkw), :], preferred_element_type=F32))
        k_c = krot_scr[pl.ds(c * t, t), :]
        v_c = v_chunk(c)
        outs = []
        q_heads = jnp.concatenate([jnp.where(head_lane == h, q_bf, jnp.zeros_like(q_bf)) for h in range(RET_HEADS)],
                                  axis=0)
        sc_heads = lax.dot_general(q_heads, k_c, (((1,), (1,)), ((), ())), preferred_element_type=F32)
        for h in range(RET_HEADS):
            sc = (sc_heads[h * t:(h + 1) * t] * dm_scr[h]).astype(BF16)
            o = (jnp.dot(sc, v_c[:, h * RET_DV:(h + 1) * RET_DV], preferred_element_type=F32)
                 + cross[:, h * RET_DV:(h + 1) * RET_DV])
            mu = jnp.mean(o, axis=-1, keepdims=True)
            var = jnp.mean(jnp.square(o - mu), axis=-1, keepdims=True)
            outs.append((o - mu) * lax.rsqrt(var + GN_EPS))
        y = gate * _sigmoid(gate) * jnp.concatenate(outs, axis=1)
        if c == 0:
            co_ref[...] = y.astype(co_ref.dtype)
        else:
            o_ref[pl.ds((c - 1) * t, t), :] = y.astype(o_ref.dtype)
    if not need_ctx:
        co_ref[...] = jnp.zeros(co_ref.shape, co_ref.dtype)


def _retention(proj_lat, proj_ctx, lg, cos, sin, pmat, l, bsz, seq, n_ctx, need_ctx):
    t = RET_T
    ncl = seq // t
    assert n_ctx == t
    qb, kb = _MY_OFF['rq'] // RET_QK_WIDTH, _MY_OFF['rk'] // RET_QK_WIDTH
    vb, gb = _MY_OFF['rv'] // RET_WIDTH, _MY_OFF['rg'] // RET_WIDTH

    def col(n, w, j):
        return pl.BlockSpec((n, w), lambda b, j=j: (b, j))

    const = lambda shape: pl.BlockSpec(shape, lambda b: (0,) * len(shape))
    return pl.pallas_call(
        functools.partial(_ret_kernel, layer=l, need_ctx=need_ctx, ncl=ncl),
        grid=(bsz,),
        in_specs=[pl.BlockSpec(memory_space=pltpu.SMEM),
                  col(seq, RET_QK_WIDTH, qb), col(seq, RET_QK_WIDTH, kb), col(seq, RET_WIDTH, vb), col(seq, RET_WIDTH, gb),
                  col(n_ctx, RET_QK_WIDTH, qb if need_ctx else kb), col(n_ctx, RET_QK_WIDTH, kb),
                  col(n_ctx, RET_WIDTH, vb), col(n_ctx, RET_WIDTH, gb),
                  const((seq, RET_QK_WIDTH)), const((seq, RET_QK_WIDTH)), const((RET_QK_WIDTH, RET_QK_WIDTH))],
        out_specs=[pl.BlockSpec((seq, RET_WIDTH), lambda b: (b, 0)),
                   pl.BlockSpec((n_ctx, RET_WIDTH), lambda b: (b, 0))],
        out_shape=[jax.ShapeDtypeStruct((bsz * seq, RET_WIDTH), BF16),
                   jax.ShapeDtypeStruct((bsz * n_ctx, RET_WIDTH), BF16)],
        scratch_shapes=[pltpu.VMEM((seq + n_ctx, RET_QK_WIDTH), BF16),
                        pltpu.VMEM((ncl + 1, 2 * RET_QK_WIDTH, RET_WIDTH), BF16),
                        pltpu.VMEM((RET_HEADS, t, t), F32)],
        compiler_params=_cparams(("parallel",)),
        name="retention",
    )(lg, proj_lat, proj_lat, proj_lat, proj_lat, proj_ctx, proj_ctx, proj_ctx, proj_ctx, cos, sin, pmat)


def _na_block_start(kblk, rows):
    return jnp.clip(kblk * NA_QROWS - NA_ROWS // 2, 0, rows - NA_KROWS)


def _na_bias_table(rpb):
    depth, heads, nlag, ncol = rpb.shape
    cols = np.arange(GRID_W)
    cs = np.clip(cols - NA_COLS // 2, 0, GRID_W - NA_COLS)
    valid_c = (cols[None, :] >= cs[:, None]) & (cols[None, :] < cs[:, None] + NA_COLS)
    pad = GRID_W - 1
    padded = jnp.pad(rpb.astype(F32) * LOG2E, ((0, 0), (0, 0), (0, 0), (pad, pad)))
    toe = jnp.stack([padded[..., pad + NA_COLS - 1 - qc: pad + NA_COLS - 1 - qc + GRID_W] for qc in range(GRID_W)],
                    axis=-2)
    toe = jnp.where(jnp.asarray(valid_c), toe, NEG_INF)
    toe = jnp.pad(toe, ((0, 0), (0, 0), (1, NA_NLAG + 1 - nlag - 1), (0, 0), (0, 0)))
    table = jnp.concatenate([toe[:, :, :NA_NLAG], toe[:, :, 1:NA_NLAG + 1]], axis=-1)
    return table.reshape(depth, heads * NA_NLAG, GRID_W, 2 * GRID_W)


def _attend(q_pair, k_list, v_list, bias_list):
    nq = q_pair.shape[0]
    lane = lax.broadcasted_iota(jnp.int32, q_pair.shape, 1) // NA_HEAD_DIM
    zero = jnp.zeros_like(q_pair)
    q2 = jnp.concatenate([jnp.where(lane == 0, q_pair, zero), jnp.where(lane == 1, q_pair, zero)], axis=0)
    ss = []
    for k_i, b_i in zip(k_list, bias_list):
        s = lax.dot_general(q2, k_i, (((1,), (1,)), ((), ())), preferred_element_type=F32)
        if b_i is not None:
            s = s + jnp.concatenate([b_i(0), b_i(1)], axis=0)
        ss.append(s)
    m = ss[0].max(axis=-1, keepdims=True)
    for s in ss[1:]:
        m = jnp.maximum(m, s.max(axis=-1, keepdims=True))
    ps = [jnp.exp2(s - m) for s in ss]
    den = ps[0].sum(axis=-1, keepdims=True)
    for p in ps[1:]:
        den = den + p.sum(axis=-1, keepdims=True)
    acc = jnp.dot(ps[0].astype(BF16), v_list[0], preferred_element_type=F32)
    for p, v_i in zip(ps[1:], v_list[1:]):
        acc = acc + jnp.dot(p.astype(BF16), v_i, preferred_element_type=F32)
    out = acc / den
    return jnp.where(lane == 0, out[:nq], out[nq:])


def _attend_window(q_pair, k_win, k_ctx, v_win, v_ctx, tab_ref, hp, rel, starts):
    nq = q_pair.shape[0]
    npair = k_win.shape[0] // LANES
    lane = lax.broadcasted_iota(jnp.int32, q_pair.shape, 1) // NA_HEAD_DIM
    zero = jnp.zeros_like(q_pair)
    dn = (((1,), (1,)), ((), ()))
    q2 = jnp.concatenate([jnp.where(lane == 0, q_pair, zero), jnp.where(lane == 1, q_pair, zero)], axis=0)
    s_win = lax.dot_general(q2, k_win, dn, preferred_element_type=F32)
    s_ctx = lax.dot_general(q2, k_ctx, dn, preferred_element_type=F32)
    left = lax.broadcasted_iota(jnp.int32, (GRID_W, LANES), 1) < GRID_W
    neg_left = jnp.where(left, NEG_INF, 0.0)
    neg_right = jnp.where(left, 0.0, NEG_INF)
    p_win, p_ctx, inv = [], [], []
    for hh in range(2):
        h = 2 * hp + hh
        for qrl in range(NA_QROWS):
            r = slice(hh * nq + qrl * GRID_W, hh * nq + (qrl + 1) * GRID_W)
            st = starts[qrl]
            lo, hi = st // 2, (st + NA_ROWS + 1) // 2
            tiles = []
            for kp in range(lo, hi):
                lag = rel + 2 * kp - qrl + NA_ROWS - 1
                assert -1 <= lag <= NA_NLAG - 2
                t = tab_ref[0, h * NA_NLAG + lag + 1]
                if 2 * kp < st:
                    t = t + neg_left
                if 2 * kp + 1 >= st + NA_ROWS:
                    t = t + neg_right
                tiles.append(t)
            sw = s_win[r, lo * LANES:hi * LANES] + jnp.concatenate(tiles, axis=1)
            sc = s_ctx[r]
            m = jnp.maximum(sw.max(axis=-1, keepdims=True), sc.max(axis=-1, keepdims=True))
            pw, pc = jnp.exp2(sw - m), jnp.exp2(sc - m)
            inv.append(1.0 / (pw.sum(axis=-1, keepdims=True) + pc.sum(axis=-1, keepdims=True)))
            parts = [pw.astype(BF16)]
            if lo > 0:
                parts.insert(0, jnp.zeros((GRID_W, lo * LANES), BF16))
            if hi < npair:
                parts.append(jnp.zeros((GRID_W, (npair - hi) * LANES), BF16))
            p_win.append(jnp.concatenate(parts, axis=1) if len(parts) > 1 else parts[0])
            p_ctx.append(pc.astype(BF16))
    acc = (jnp.dot(jnp.concatenate(p_win, axis=0), v_win, preferred_element_type=F32)
           + jnp.dot(jnp.concatenate(p_ctx, axis=0), v_ctx, preferred_element_type=F32))
    out = jnp.concatenate([acc[i * GRID_W:(i + 1) * GRID_W] * inv[i] for i in range(len(inv))], axis=0)
    return jnp.where(lane == 0, out[:nq], out[nq:])


def _na_kernel(q_ref, k_ref, v_ref, ck_ref, cv_ref, tab_ref, o_ref, *, rows):
    nk = NA_KROWS * GRID_W
    nblk = rows // NA_QROWS
    kblk = pl.program_id(1)
    ks = pl.multiple_of(_na_block_start(kblk, rows) * GRID_W, NA_QROWS * GRID_W)

    def run(kb):
        r0 = kb * NA_QROWS
        ks_row = min(max(r0 - NA_ROWS // 2, 0), rows - NA_KROWS)
        starts = [min(max(r0 + qrl - NA_ROWS // 2, 0), rows - NA_ROWS) - ks_row for qrl in range(NA_QROWS)]
        for hp in range(NA_HEADS // 2):
            ln = pl.ds(hp * LANES, LANES)
            y = _attend_window(q_ref[:, ln], k_ref[pl.ds(ks, nk), ln], ck_ref[:, ln],
                               v_ref[pl.ds(ks, nk), ln], cv_ref[:, ln], tab_ref, hp, ks_row - r0, starts)
            o_ref[:, ln] = y.astype(o_ref.dtype)

    assert nblk >= 3 and all(_na_geometry(kb, rows) == _na_geometry(1, rows) for kb in range(1, nblk - 1))
    pl.when(kblk == 0)(lambda: run(0))
    pl.when((kblk > 0) & (kblk < nblk - 1))(lambda: run(1))
    pl.when(kblk == nblk - 1)(lambda: run(nblk - 1))


def _na_geometry(kb, rows):
    r0 = kb * NA_QROWS
    ks_row = min(max(r0 - NA_ROWS // 2, 0), rows - NA_KROWS)
    return (ks_row - r0,) + tuple(min(max(r0 + qrl - NA_ROWS // 2, 0), rows - NA_ROWS) - ks_row
                                  for qrl in range(NA_QROWS))


def _na(proj_lat, proj_ctx, table, l, bsz, seq, n_ctx):
    rows = seq // GRID_W
    nq = NA_QROWS * GRID_W
    nblk = seq // nq
    qb, kb, vb = (_MY_OFF[n] // NA_WIDTH for n in ('nq', 'nk', 'nv'))
    return pl.pallas_call(
        functools.partial(_na_kernel, rows=rows),
        grid=(bsz, nblk),
        in_specs=[pl.BlockSpec((nq, NA_WIDTH), lambda b, k: (b * nblk + k, qb)),
                  pl.BlockSpec((seq, NA_WIDTH), lambda b, k: (b, kb)),
                  pl.BlockSpec((seq, NA_WIDTH), lambda b, k: (b, vb)),
                  pl.BlockSpec((n_ctx, NA_WIDTH), lambda b, k: (b, kb)),
                  pl.BlockSpec((n_ctx, NA_WIDTH), lambda b, k: (b, vb)),
                  pl.BlockSpec((1,) + table.shape[1:], lambda b, k: (l, 0, 0, 0))],
        out_specs=pl.BlockSpec((nq, NA_WIDTH), lambda b, k: (b * nblk + k, 0)),
        out_shape=jax.ShapeDtypeStruct((bsz * seq, NA_WIDTH), BF16),
        compiler_params=_cparams(("parallel", "arbitrary")),
        name="neighborhood_attention",
    )(proj_lat, proj_lat, proj_lat, proj_ctx, proj_ctx, table)


def _ctx_attn_kernel(q_ref, k_ref, v_ref, o_ref):
    for hp in range(NA_HEADS // 2):
        ln = pl.ds(hp * LANES, LANES)
        y = _attend(q_ref[:, ln], [k_ref[:, ln]], [v_ref[:, ln]], [None])
        o_ref[:, ln] = y.astype(o_ref.dtype)


def _ctx_attn(proj_ctx, bsz, n_ctx):
    qb, kb, vb = (_MY_OFF[n] // NA_WIDTH for n in ('nq', 'nk', 'nv'))
    spec = lambda j: pl.BlockSpec((n_ctx, NA_WIDTH), lambda b: (b, j))
    return pl.pallas_call(
        _ctx_attn_kernel,
        grid=(bsz,),
        in_specs=[spec(qb), spec(kb), spec(vb)],
        out_specs=pl.BlockSpec((n_ctx, NA_WIDTH), lambda b: (b, 0)),
        out_shape=jax.ShapeDtypeStruct((bsz * n_ctx, NA_WIDTH), BF16),
        compiler_params=_cparams(("parallel",)),
        name="context_attention",
    )(proj_ctx, proj_ctx, proj_ctx)


def _gelu_tanh(x):
    return 0.5 * x * (1.0 + jnp.tanh(math.sqrt(2.0 / math.pi) * (x + 0.044715 * (x * x * x))))


def _merge_kernel(x_ref, g0_ref, g1_ref, g2_ref, yt_ref, yret_ref, yna_ref, wglu_ref, bglu_ref,
                  wbs5_ref, wbret_ref, wbna_ref, wout_ref, gate_ref, o_ref, w_scr):
    nrow = yt_ref.shape[1]
    for j in range(_NSLAB):
        for half in range(2):
            o = [yt_ref[j * _GPL + q, :, half * LANES:(half + 1) * LANES].astype(F32) for q in range(_GPL)]
            for tl, out in enumerate(_block_transpose(o)):
                w_scr[j, pl.ds(half * _GPL + tl, nrow, stride=S5_CHUNK), :] = out
    ge = _gelu_tanh(jnp.concatenate([w_scr[j] for j in range(_NSLAB)], axis=1))
    z = jnp.dot(ge.astype(BF16), wglu_ref[0], preferred_element_type=F32) + bglu_ref[0]
    s5 = (ge * _sigmoid(z)).astype(BF16)
    m = (_sigmoid(g0_ref[...].astype(F32)) * jnp.dot(s5, wbs5_ref[0], preferred_element_type=F32)
         + _sigmoid(g1_ref[...].astype(F32)) * jnp.dot(yret_ref[...], wbret_ref[0], preferred_element_type=F32)
         + _sigmoid(g2_ref[...].astype(F32)) * jnp.dot(yna_ref[...], wbna_ref[0], preferred_element_type=F32))
    o_ref[...] = x_ref[...] + gate_ref[0] * jnp.dot(m.astype(BF16), wout_ref[0], preferred_element_type=F32)


def _merge(x2, proj, y_t, yret, yna, mods, mod_row, l, wts, tm):
    m, d = x2.shape
    base = l * MOD_ROWS * 6
    rowblk = lambda w, j=0: pl.BlockSpec((tm, w), lambda i, j=j: (i, j))
    return pl.pallas_call(
        _merge_kernel,
        grid=(m // tm,),
        in_specs=[rowblk(d), rowblk(d, 0), rowblk(d, 1), rowblk(d, 2),
                  pl.BlockSpec((S5_GROUPS, tm // S5_CHUNK, S5_CHUNK * S5_GROUP), lambda i: (0, i, 0)),
                  rowblk(RET_WIDTH), rowblk(NA_WIDTH)]
                 + [_layer_spec(w, l) for w in wts]
                 + [pl.BlockSpec((1, 1, d), lambda i: (base + mod_row(i) * 6 + 2, 0, 0))],
        out_specs=rowblk(d),
        out_shape=jax.ShapeDtypeStruct((m, d), F32),
        scratch_shapes=[pltpu.VMEM((_NSLAB, tm, LANES), F32)],
        compiler_params=_cparams(("parallel",)),
        name="merge_residual",
    )(x2, proj, proj, proj, y_t, yret, yna, *wts, mods)


def _ffn_kernel(x_ref, sh_ref, sc_ref, gate_ref, wg_ref, wu_ref, wd_ref, fn_ref, o_ref, *, final, th):
    x = x_ref[...]
    h = (_rms(x) * (1.0 + sc_ref[0]) + sh_ref[0]).astype(BF16)
    hidden = wg_ref.shape[2]
    acc = jnp.zeros(x.shape, F32)
    for j in range(hidden // th):
        a = jnp.dot(h, wg_ref[0, :, j * th:(j + 1) * th], preferred_element_type=F32)
        b = jnp.dot(h, wu_ref[0, :, j * th:(j + 1) * th], preferred_element_type=F32)
        act = (a * _sigmoid(a) * b).astype(BF16)
        acc = acc + jnp.dot(act, wd_ref[0, j * th:(j + 1) * th, :], preferred_element_type=F32)
    y = x + gate_ref[0] * acc
    if final:
        y = _rms(y) * fn_ref[...]
    o_ref[...] = y


def _ffn(x2, mods, mod_row, l, wg, wu, wd, fn, tm, final):
    m, d = x2.shape
    base = l * MOD_ROWS * 6
    modspec = lambda k: pl.BlockSpec((1, 1, d), lambda i, k=k: (base + mod_row(i) * 6 + k, 0, 0))
    return pl.pallas_call(
        functools.partial(_ffn_kernel, final=final, th=256),
        grid=(m // tm,),
        in_specs=[pl.BlockSpec((tm, d), lambda i: (i, 0)), modspec(3), modspec(4), modspec(5),
                  _layer_spec(wg, l), _layer_spec(wu, l), _layer_spec(wd, l),
                  pl.BlockSpec(fn.shape, lambda i: (0, 0))],
        out_specs=pl.BlockSpec((tm, d), lambda i: (i, 0)),
        out_shape=jax.ShapeDtypeStruct((m, d), F32),
        compiler_params=_cparams(("parallel",)),
        name="swiglu_residual",
    )(x2, mods, mods, mods, wg, wu, wd, fn)


def kernel(x, c, ctx, c_ctx, w_ada, b_ada, w_in, s5_lam_re, s5_lam_im, s5_log_dt, s5_b_re, s5_b_im,
           s5_c_re, s5_c_im, s5_d, s5_w_glu, s5_b_glu, ret_theta, na_rpb, w_branch_s5, w_branch_ret,
           w_branch_na, w_out, w_ffn_gate, w_ffn_up, w_ffn_down, final_norm):
    bsz, seq, d = x.shape
    n_ctx = ctx.shape[1]
    depth = w_ada.shape[0]
    ctx_row = bsz
    assert bsz == SUBLANES and bsz + 1 <= MOD_ROWS

    cvec = jnp.zeros((MOD_ROWS, d), F32).at[:bsz].set(c).at[ctx_row].set(c_ctx)
    mods = _ada(cvec, w_ada, b_ada).reshape(depth * MOD_ROWS * 6, 1, d)

    w_in_k = jnp.concatenate(
        [(w_in[:, :, _REF_OFF[n]:_REF_OFF[n] + _REF_W[n]] * _COL_SCALE.get(n, 1.0)).astype(BF16) for n in _MY_ORDER],
        axis=2)

    cos, sin, pmat = _rotary_tables(seq)
    log_gamma = jax.nn.log_sigmoid(ret_theta.astype(F32)).reshape(depth, 2 * RET_HEADS)
    kin, vin, win, a_t = _s5_weights(s5_lam_re, s5_lam_im, s5_log_dt, s5_b_re, s5_b_im, s5_c_re, s5_c_im, s5_d)
    na_table = _na_bias_table(na_rpb)
    merge_w = (s5_w_glu.astype(BF16), s5_b_glu.reshape(depth, 1, -1).astype(F32), w_branch_s5.astype(BF16),
               w_branch_ret.astype(BF16), w_branch_na.astype(BF16), w_out.astype(BF16))
    ffn_w = (w_ffn_gate.astype(BF16), w_ffn_up.astype(BF16), w_ffn_down.astype(BF16),
             final_norm.reshape(1, d).astype(F32))

    tm_proj, tn_proj, tm = 1024, N_IN // 2, 512
    assert all(_MY_OFF[n] >= _CTX_KV_BLOCK * tn_proj for n in ('u', 'rk', 'rv', 'nk', 'nv', 'rg'))
    lat_row = lambda t: (lambda i: i // (seq // t))
    ctx_mod_row = lambda i: ctx_row

    x2 = x.reshape(bsz * seq, d)
    c2 = ctx.reshape(bsz * n_ctx, d)
    for l in range(depth):
        need_ctx = l < depth - 1
        proj_lat, u_lat = _inproj(x2, mods, lat_row(tm_proj), w_in_k, l, tm_proj, tn_proj)
        proj_ctx, u_ctx = _inproj(c2, mods, ctx_mod_row, w_in_k, l, tm_proj, tn_proj,
                                  first_block=0 if need_ctx else _CTX_KV_BLOCK)

        ys5_ctx, ys5_lat = _s5(u_ctx, u_lat, kin, vin, win, a_t, l, bsz)
        yret_lat, yret_ctx = _retention(proj_lat, proj_ctx, log_gamma, cos, sin, pmat, l, bsz, seq, n_ctx, need_ctx)
        yna_lat = _na(proj_lat, proj_ctx, na_table, l, bsz, seq, n_ctx)

        x2 = _merge(x2, proj_lat, ys5_lat, yret_lat, yna_lat, mods, lat_row(tm), l, merge_w, tm)
        x2 = _ffn(x2, mods, lat_row(tm), l, *ffn_w, tm, final=not need_ctx)
        if need_ctx:
            yna_ctx = _ctx_attn(proj_ctx, bsz, n_ctx)
            c2 = _merge(c2, proj_ctx, ys5_ctx, yret_ctx, yna_ctx, mods, ctx_mod_row, l, merge_w, tm)
            c2 = _ffn(c2, mods, ctx_mod_row, l, *ffn_w, tm, final=False)
    return x2.reshape(bsz, seq, d)
```

```python
import functools
import math

import numpy as np
import jax
import jax.numpy as jnp
from jax import lax
from jax.experimental import pallas as pl
from jax.experimental.pallas import tpu as pltpu

F32 = jnp.float32
BF16 = jnp.bfloat16
HIGHEST = lax.Precision.HIGHEST

D_MODEL = 1024
GRID_W = 64
S5_WIDTH = 512
S5_GROUP = 16
S5_GROUPS = S5_WIDTH // S5_GROUP
S5_STATE = 64
S5_CHUNK = 16
RET_HEADS = 4
RET_DK = 64
RET_DV = 128
RET_QK_WIDTH = RET_HEADS * RET_DK
RET_WIDTH = RET_HEADS * RET_DV
RET_T = 256
NA_HEADS = 8
NA_HEAD_DIM = 64
NA_WIDTH = NA_HEADS * NA_HEAD_DIM
NA_ROWS = 8
NA_COLS = 16
NA_QROWS = 4
NA_KROWS = 12
NA_NLAG = 2 * NA_ROWS
N_BRANCH = 3
ROPE_BASE = 10000.0
RMS_EPS = 1e-6
GN_EPS = 1e-5
NEG_INF = -1e30
LANES = 128
SUBLANES = 8
MOD_ROWS = 16

_REF_SPLIT = (S5_WIDTH, RET_QK_WIDTH, RET_WIDTH, NA_WIDTH, NA_WIDTH,
              RET_QK_WIDTH, RET_WIDTH, NA_WIDTH, N_BRANCH * D_MODEL)
_REF_NAMES = ('u', 'rk', 'rv', 'nk', 'nv', 'rq', 'rg', 'nq', 'gates')
_REF_OFF = dict(zip(_REF_NAMES, np.concatenate([[0], np.cumsum(_REF_SPLIT)[:-1]]).tolist()))
_REF_W = dict(zip(_REF_NAMES, _REF_SPLIT))
_MY_ORDER = ('gates', 'rk', 'rq', 'u', 'rv', 'nk', 'nv', 'rg', 'nq')
_MY_OFF = {}
_o = 0
for _n in _MY_ORDER:
    _MY_OFF[_n] = _o
    _o += _REF_W[_n]
N_IN = _o
LOG2E = math.log2(math.e)
_COL_SCALE = {'nq': NA_HEAD_DIM ** -0.5 * LOG2E, 'rk': RET_DK ** -0.5}

VMEM_LIMIT = 56 * 1024 * 1024


def _cparams(sem):
    return pltpu.CompilerParams(dimension_semantics=sem, vmem_limit_bytes=VMEM_LIMIT)


def _sigmoid(x):
    return 1.0 / (1.0 + jnp.exp(-x))


def _rms(x):
    return x * lax.rsqrt(jnp.mean(x * x, axis=-1, keepdims=True) + RMS_EPS)


def _layer_spec(arr, l):
    nd = arr.ndim
    return pl.BlockSpec((1,) + arr.shape[1:], lambda *_: (l,) + (0,) * (nd - 1))


def _ada_kernel(c_ref, w_ref, b_ref, o_ref):
    c = c_ref[...]
    s = c * _sigmoid(c)
    o_ref[0] = jnp.dot(s, w_ref[0], preferred_element_type=F32, precision=HIGHEST) + b_ref[0]


def _ada(cvec, w_ada, b_ada):
    depth, d, n = w_ada.shape
    tn = 1536
    rows = cvec.shape[0]
    return pl.pallas_call(
        _ada_kernel,
        grid=(depth, n // tn),
        in_specs=[pl.BlockSpec((rows, d), lambda l, j: (0, 0)),
                  pl.BlockSpec((1, d, tn), lambda l, j: (l, 0, j)),
                  pl.BlockSpec((1, 1, tn), lambda l, j: (l, 0, j))],
        out_specs=pl.BlockSpec((1, rows, tn), lambda l, j: (l, 0, j)),
        out_shape=jax.ShapeDtypeStruct((depth, rows, n), F32),
        compiler_params=_cparams(("parallel", "parallel")),
        name="ada_mod",
    )(cvec, w_ada, b_ada.reshape(depth, 1, n))


_GPL = LANES // S5_GROUP
_NSLAB = S5_WIDTH // LANES


def _block_transpose(tiles):
    blk = lax.broadcasted_iota(jnp.int32, tiles[0].shape, 1) // S5_GROUP
    tiles = list(tiles)
    s = _GPL // 2
    while s >= 1:
        hi = (blk & s) != 0
        for a in range(_GPL):
            if a & s:
                continue
            b = a + s
            ta, tb = tiles[a], tiles[b]
            tiles[a] = jnp.where(hi, pltpu.roll(tb, s * S5_GROUP, 1), ta)
            tiles[b] = jnp.where(hi, tb, pltpu.roll(ta, LANES - s * S5_GROUP, 1))
        s //= 2
    return tiles


def _inproj_kernel(x_ref, sh_ref, sc_ref, w_ref, o_ref, ut_ref, h_ref, u_scr, *, u_off):
    @pl.when(pl.program_id(1) == 0)
    def _():
        h = _rms(x_ref[...]) * (1.0 + sc_ref[0]) + sh_ref[0]
        h_ref[...] = h.astype(BF16)

    res = jnp.dot(h_ref[...], w_ref[0], preferred_element_type=F32)
    o_ref[...] = res.astype(o_ref.dtype)
    for j in range(_NSLAB):
        u_scr[j] = res[:, u_off + j * LANES:u_off + (j + 1) * LANES]
    nrow = u_scr.shape[1] // S5_CHUNK
    for j in range(_NSLAB):
        for half in range(2):
            v = [u_scr[j, pl.ds(half * _GPL + tl, nrow, stride=S5_CHUNK), :] for tl in range(_GPL)]
            for q, out in enumerate(_block_transpose(v)):
                ut_ref[j * _GPL + q, :, half * LANES:(half + 1) * LANES] = out.astype(ut_ref.dtype)


def _inproj(x2, mods, mod_row, w, l, tm, tn):
    m, d = x2.shape
    n = w.shape[2]
    base = l * MOD_ROWS * 6
    nj = n // tn
    u_off = _MY_OFF['u'] - (nj - 1) * tn
    assert 0 <= u_off and u_off + S5_WIDTH <= tn
    return pl.pallas_call(
        functools.partial(_inproj_kernel, u_off=u_off),
        grid=(m // tm, nj),
        in_specs=[pl.BlockSpec((tm, d), lambda i, j: (i, 0)),
                  pl.BlockSpec((1, 1, d), lambda i, j: (base + mod_row(i) * 6 + 0, 0, 0)),
                  pl.BlockSpec((1, 1, d), lambda i, j: (base + mod_row(i) * 6 + 1, 0, 0)),
                  pl.BlockSpec((1, d, tn), lambda i, j: (l, 0, j))],
        out_specs=[pl.BlockSpec((tm, tn), lambda i, j: (i, j)),
                   pl.BlockSpec((S5_GROUPS, tm // S5_CHUNK, S5_CHUNK * S5_GROUP), lambda i, j: (0, i, 0))],
        out_shape=[jax.ShapeDtypeStruct((m, n), BF16),
                   jax.ShapeDtypeStruct((S5_GROUPS, m // S5_CHUNK, S5_CHUNK * S5_GROUP), BF16)],
        scratch_shapes=[pltpu.VMEM((tm, d), BF16), pltpu.VMEM((_NSLAB, tm, LANES), F32)],
        compiler_params=_cparams(("parallel", "arbitrary")),
        name="in_proj",
    )(x2, mods, mods, w)


def _s5w_kernel(lam_ref, btr_ref, bti_ref, ctr_ref, cti_ref, dd_ref, kin_ref, vin_ref, win_ref, at_ref, *, ng):
    t_n, h_n = S5_CHUNK, S5_GROUP
    width = t_n * h_n
    lane = lax.broadcasted_iota(jnp.int32, (1, LANES), 1)
    f_lane = lane < S5_STATE
    tau = lax.broadcasted_iota(jnp.int32, (3 * SUBLANES, LANES), 0).astype(F32)
    lane_w = lax.broadcasted_iota(jnp.int32, (h_n, width), 1)
    for g in range(ng):
        lam_re, lam_im, dt = lam_ref[g, 0:1, :], lam_ref[g, 1:2, :], lam_ref[g, 2:3, :]
        mag = jnp.exp(tau * (lam_re * dt))
        ang = tau * (lam_im * dt)
        pr, pi = mag * jnp.cos(ang), mag * jnp.sin(ang)
        ab_re, ab_im = pr[1:2], pi[1:2]
        den = lam_re * lam_re + lam_im * lam_im
        f_re = ((ab_re - 1.0) * lam_re + ab_im * lam_im) / den
        f_im = (ab_im * lam_re - (ab_re - 1.0) * lam_im) / den
        btr, bti = btr_ref[g], bti_ref[g]
        bbr = f_re * btr - f_im * bti
        bbi = f_re * bti + f_im * btr
        ctr, cti = ctr_ref[g], cti_ref[g]

        def powers(pf, pb):
            rr = [jnp.broadcast_to(jnp.where(f_lane, pr[pf[t]:pf[t] + 1], pr[pb[t]:pb[t] + 1]), (h_n, LANES))
                  for t in range(t_n)]
            ri = [jnp.broadcast_to(jnp.where(f_lane, pi[pf[t]:pf[t] + 1], pi[pb[t]:pb[t] + 1]), (h_n, LANES))
                  for t in range(t_n)]
            return jnp.concatenate(rr, axis=0), jnp.concatenate(ri, axis=0)

        tile = lambda a: jnp.concatenate([a] * t_n, axis=0)
        bbr_t, bbi_t, ctr_t, cti_t = tile(bbr), tile(bbi), tile(ctr), tile(cti)

        xr, xi = powers([t_n - 1 - t for t in range(t_n)], list(range(t_n)))
        vin = jnp.concatenate([xr * bbr_t - xi * bbi_t, xr * bbi_t + xi * bbr_t], axis=1)
        vin_ref[g] = vin.astype(vin_ref.dtype)

        yr, yi = powers([t + 1 for t in range(t_n)], [t_n - t for t in range(t_n)])
        win_t = jnp.concatenate([ctr_t * yr - cti_t * yi, -(ctr_t * yi + cti_t * yr)], axis=1)
        win_ref[g] = win_t.T.astype(win_ref.dtype)

        zr, zi = powers(list(range(t_n)), [t_n - 1 - t for t in range(t_n)])
        fmat = jnp.concatenate([ctr_t * zr - cti_t * zi, ctr_t * zi + cti_t * zr], axis=1)
        dn = (((1,), (1,)), ((), ()))
        lhs_f = jnp.concatenate([jnp.where(f_lane, bbr, 0.0), jnp.where(f_lane, -bbi, 0.0)], axis=1)
        lhs_b = jnp.concatenate([jnp.where(f_lane, 0.0, bbr), jnp.where(f_lane, 0.0, -bbi)], axis=1)
        w_f = lax.dot_general(lhs_f, fmat, dn, preferred_element_type=F32, precision=HIGHEST)
        w_b = lax.dot_general(lhs_b, fmat, dn, preferred_element_type=F32, precision=HIGHEST)
        w_f = w_f + jnp.concatenate([dd_ref[g], jnp.zeros((h_n, width - LANES), F32)], axis=1)
        blocks = []
        for t in range(t_n):
            sh_f = t * h_n
            sh_b = (width - (t_n - 1 - t) * h_n) % width
            fw = w_f if sh_f == 0 else pltpu.roll(w_f, sh_f, 1)
            bw = w_b if sh_b == 0 else pltpu.roll(w_b, sh_b, 1)
            blocks.append(jnp.where(lane_w >= t * h_n, fw, 0.0) + jnp.where(lane_w < (t + 1) * h_n, bw, 0.0))
        kin_ref[g] = jnp.concatenate(blocks, axis=0).astype(kin_ref.dtype)
        at_ref[g] = jnp.broadcast_to(jnp.concatenate([pr[t_n:t_n + 1], pi[t_n:t_n + 1]], axis=1),
                                     (SUBLANES, 2 * LANES))


def _s5_weights(lam_re, lam_im, log_dt, b_re, b_im, c_re, c_im, d_skip, ng=4):
    depth, _, g_n, p_n = lam_re.shape
    h_n = S5_GROUP
    n = depth * g_n
    width = S5_CHUNK * h_n
    pair = lambda a: a.astype(F32).transpose(0, 2, 1, 3).reshape(n, 1, 2 * p_n)
    dt = jnp.broadcast_to(jnp.exp(log_dt.astype(F32))[..., None], lam_re.shape)
    lam = jnp.concatenate([pair(lam_re), pair(lam_im), pair(dt), jnp.zeros((n, SUBLANES - 3, 2 * p_n), F32)], axis=1)
    bt = lambda a: a.astype(F32).transpose(0, 2, 4, 1, 3).reshape(n, h_n, 2 * p_n)
    ct = lambda a: a.astype(F32).transpose(0, 2, 3, 1, 4).reshape(n, h_n, 2 * p_n)
    dd = jnp.eye(h_n, LANES, dtype=F32)[None] * d_skip.astype(F32).reshape(n, h_n, 1)
    small = pl.BlockSpec((ng, h_n, LANES), lambda i: (i, 0, 0))
    big = pl.BlockSpec((ng, width, width), lambda i: (i, 0, 0))
    return pl.pallas_call(
        functools.partial(_s5w_kernel, ng=ng),
        grid=(n // ng,),
        in_specs=[pl.BlockSpec((ng, SUBLANES, LANES), lambda i: (i, 0, 0)), small, small, small, small, small],
        out_specs=[big, big, big, pl.BlockSpec((ng, SUBLANES, width), lambda i: (i, 0, 0))],
        out_shape=[jax.ShapeDtypeStruct((n, width, width), BF16)] * 3
                  + [jax.ShapeDtypeStruct((n, SUBLANES, width), F32)],
        compiler_params=_cparams(("parallel",)),
        name="s5_weights",
    )(lam, bt(b_re), bt(b_im), ct(c_re), ct(c_im), dd)


def _s5_pitch(n):
    p = -(-n // SUBLANES)
    return (p | 1) * SUBLANES


def _s5_kernel(uc_ref, ul_ref, kin_ref, vin_ref, win_ref, a_ref, yc_ref, yl_ref,
               sc_scr, sl_scr, xac_scr, xal_scr, xbc_scr, xbl_scr, *, ng, ncc, ncl, bsz):
    half = 2 * S5_STATE
    pc, plat = _s5_pitch(ncc), _s5_pitch(ncl)
    segs = ((uc_ref, yc_ref, sc_scr, xac_scr, xbc_scr, ncc, pc), (ul_ref, yl_ref, sl_scr, xal_scr, xbl_scr, ncl, plat))
    for u_ref, _, s_scr, _, _, n, pitch in segs:
        for g in range(ng):
            s = jnp.dot(u_ref[g], vin_ref[g], preferred_element_type=F32)
            for b in range(bsz):
                for k in range(2):
                    s_scr[g, k, pl.ds(b * pitch, n), :] = s[b * n:(b + 1) * n, k * half:(k + 1) * half]
    lane = lax.broadcasted_iota(jnp.int32, (bsz, half), 1)
    fwd_lane = lane < S5_STATE
    a_re = [a_ref[g, :, :half] for g in range(ng)]
    a_im = [a_ref[g, :, half:] for g in range(ng)]

    def make_step(s_scr, xa_scr, xb_scr, pitch):
        def step(fc, bc, xs):
            rf = pl.ds(fc, bsz, stride=pitch)
            rb = pl.ds(bc, bsz, stride=pitch)
            out = []
            for g in range(ng):
                xr, xi = xs[2 * g], xs[2 * g + 1]
                xa_scr[g, 0, rf, :] = xr
                xa_scr[g, 1, rf, :] = xi
                xb_scr[g, 0, rb, :] = xr
                xb_scr[g, 1, rb, :] = xi
                sr = jnp.where(fwd_lane, s_scr[g, 0, rf, :], s_scr[g, 0, rb, :])
                si = jnp.where(fwd_lane, s_scr[g, 1, rf, :], s_scr[g, 1, rb, :])
                out.append(a_re[g] * xr - a_im[g] * xi + sr)
                out.append(a_re[g] * xi + a_im[g] * xr + si)
            return tuple(out)
        return step

    step_c = make_step(sc_scr, xac_scr, xbc_scr, pc)
    step_l = make_step(sl_scr, xal_scr, xbl_scr, plat)
    xs = tuple(jnp.zeros((bsz, half), F32) for _ in range(2 * ng))
    xs = lax.fori_loop(0, ncc, lambda i, c: step_c(i, ncc - 1 - i, c), xs, unroll=4)
    xs = lax.fori_loop(0, ncl, lambda i, c: step_l(i, ncl - 1 - i, c), xs, unroll=4)
    for u_ref, y_ref, _, xa_scr, xb_scr, n, pitch in segs:
        fwd_r = lax.broadcasted_iota(jnp.int32, (n, half), 1) < S5_STATE
        for g in range(ng):
            rows = []
            for b in range(bsz):
                r = pl.ds(b * pitch, n)
                rows.append(jnp.concatenate([jnp.where(fwd_r, xa_scr[g, k, r, :], xb_scr[g, k, r, :]) for k in range(2)],
                                            axis=1))
            x_in = jnp.concatenate(rows, axis=0).astype(BF16)
            y = (jnp.dot(u_ref[g], kin_ref[g], preferred_element_type=F32)
                 + jnp.dot(x_in, win_ref[g], preferred_element_type=F32))
            y_ref[g] = y.astype(y_ref.dtype)


def _s5(u_ctx, u_lat, kin, vin, win, a_t, l, bsz, ng=4):
    g_n, rc, w = u_ctx.shape
    rl = u_lat.shape[1]
    ncc, ncl = rc // bsz, rl // bsz
    nblk = g_n // ng
    wspec = pl.BlockSpec((ng, w, w), lambda i: (l * nblk + i, 0, 0))
    uspec = lambda r: pl.BlockSpec((ng, r, w), lambda i: (i, 0, 0))
    scr = lambda n: pltpu.VMEM((ng, 2, bsz * _s5_pitch(n), LANES), F32)
    return pl.pallas_call(
        functools.partial(_s5_kernel, ng=ng, ncc=ncc, ncl=ncl, bsz=bsz),
        grid=(nblk,),
        in_specs=[uspec(rc), uspec(rl), wspec, wspec, wspec,
                  pl.BlockSpec((ng, SUBLANES, w), lambda i: (l * nblk + i, 0, 0))],
        out_specs=[uspec(rc), uspec(rl)],
        out_shape=[jax.ShapeDtypeStruct(u_ctx.shape, BF16), jax.ShapeDtypeStruct(u_lat.shape, BF16)],
        scratch_shapes=[scr(ncc), scr(ncl)] * 3,
        compiler_params=_cparams(("parallel",)),
        name="s5_mixer",
    )(u_ctx, u_lat, kin, vin, win, a_t)


def _rotary_tables(seq):
    quarter = RET_DK // 4
    pos = jnp.arange(seq)
    inv_freq = ROPE_BASE ** (-jnp.arange(quarter, dtype=F32) / quarter)
    ang_r = (pos // GRID_W).astype(F32)[:, None] * inv_freq[None, :]
    ang_c = (pos % GRID_W).astype(F32)[:, None] * inv_freq[None, :]
    cos = jnp.concatenate([jnp.cos(ang_r)] * 2 + [jnp.cos(ang_c)] * 2, axis=-1)
    sin = jnp.concatenate([jnp.sin(ang_r)] * 2 + [jnp.sin(ang_c)] * 2, axis=-1)
    cos = jnp.tile(cos, (1, RET_HEADS))
    sin = jnp.tile(sin, (1, RET_HEADS))
    p = np.zeros((RET_QK_WIDTH, RET_QK_WIDTH), np.float32)
    for d in range(RET_QK_WIDTH):
        if d % (2 * quarter) < quarter:
            p[d + quarter, d] = -1.0
        else:
            p[d - quarter, d] = 1.0
    return cos, sin, jnp.asarray(p, BF16)


def _ret_kernel(lg_ref, q_ref, k_ref, v_ref, g_ref, cq_ref, ck_ref, cv_ref, cg_ref,
                cos_ref, sin_ref, p_ref, o_ref, co_ref, krot_scr, sin_scr, dm_scr, *, layer, need_ctx, ncl):
    t = RET_T
    qkw, vw = RET_QK_WIDTH, RET_WIDTH

    def per_head(shape, axis, width, d):
        head = lax.broadcasted_iota(jnp.int32, shape, axis) // width
        out = jnp.zeros(shape, F32)
        for h in range(RET_HEADS):
            out = jnp.where(head == h, lg_ref[layer, d * RET_HEADS + h], out)
        return out

    row = lax.broadcasted_iota(jnp.int32, (t, qkw), 0).astype(F32)
    lgf = per_head((t, qkw), 1, RET_DK, 0)
    lgb = per_head((t, qkw), 1, RET_DK, 1)
    qdec_f = jnp.exp((row + 1.0) * lgf)
    qdec_b = jnp.exp((t - row) * lgb)
    kdec_f = jnp.exp((t - 1.0 - row) * lgf)
    kdec_b = jnp.exp(row * lgb)
    cdec_f = jnp.exp(float(t) * per_head((qkw, vw), 0, RET_DK, 0))
    cdec_b = jnp.exp(float(t) * per_head((qkw, vw), 0, RET_DK, 1))
    blk = (lax.broadcasted_iota(jnp.int32, (qkw, vw), 0) // RET_DK
           == lax.broadcasted_iota(jnp.int32, (qkw, vw), 1) // RET_DV)
    head_lane = lax.broadcasted_iota(jnp.int32, (t, qkw), 1) // RET_DK

    ii = lax.broadcasted_iota(jnp.int32, (t, t), 0)
    jj = lax.broadcasted_iota(jnp.int32, (t, t), 1)
    dif = (ii - jj).astype(F32)
    for h in range(RET_HEADS):
        df = jnp.where(dif >= 0, jnp.exp(jnp.where(dif >= 0, dif, 0.0) * lg_ref[layer, h]), 0.0)
        db = jnp.where(dif < 0, jnp.exp(jnp.where(dif < 0, -dif, 0.0) * lg_ref[layer, RET_HEADS + h]), 0.0)
        dm_scr[h] = df + db

    def rotary(x_bf, c):
        rows = pl.ds(c * t, t)
        swapped = jnp.dot(x_bf, p_ref[...], preferred_element_type=F32)
        return x_bf.astype(F32) * cos_ref[rows, :] + swapped * sin_ref[rows, :]

    krot_scr[pl.ds(0, t), :] = ck_ref[...]
    for c in range(ncl):
        krot_scr[pl.ds((c + 1) * t, t), :] = rotary(k_ref[pl.ds(c * t, t), :], c).astype(BF16)

    def v_chunk(c):
        return cv_ref[...] if c == 0 else v_ref[pl.ds((c - 1) * t, t), :]

    def kv(c, kdec):
        kd = (krot_scr[pl.ds(c * t, t), :].astype(F32) * kdec).astype(BF16)
        return lax.dot_general(kd, v_chunk(c), (((0,), (0,)), ((), ())), preferred_element_type=F32)

    s = jnp.zeros((qkw, vw), F32)
    for c in range(ncl + 1):
        sin_scr[c, pl.ds(0, qkw), :] = jnp.where(blk, s, 0.0).astype(BF16)
        if c < ncl:
            s = cdec_f * s + kv(c, kdec_f)
    sin_scr[0, pl.ds(qkw, qkw), :] = jnp.zeros((qkw, vw), BF16)
    s = kv(0, kdec_b)
    for c in range(ncl, 0, -1):
        sin_scr[c, pl.ds(qkw, qkw), :] = jnp.where(blk, s, 0.0).astype(BF16)
        if c > 1:
            s = cdec_b * s + kv(c, kdec_b)

    for c in range(0 if need_ctx else 1, ncl + 1):
        if c == 0:
            q = cq_ref[...].astype(F32)
            gate = cg_ref[...].astype(F32)
        else:
            q = rotary(q_ref[pl.ds((c - 1) * t, t), :], c - 1)
            gate = g_ref[pl.ds((c - 1) * t, t), :].astype(F32)
        q_bf = q.astype(BF16)
        cross = (jnp.dot((q * qdec_f).astype(BF16), sin_scr[c, pl.ds(0, qkw), :], preferred_element_type=F32)
                 + jnp.dot((q * qdec_b).astype(BF16), sin_scr[c, pl.ds(qkw, qkw), :], preferred_element_type=F32))
        k_c = krot_scr[pl.ds(c * t, t), :]
        v_c = v_chunk(c)
        outs = []
        q_heads = jnp.concatenate([jnp.where(head_lane == h, q_bf, jnp.zeros_like(q_bf)) for h in range(RET_HEADS)],
                                  axis=0)
        sc_heads = lax.dot_general(q_heads, k_c, (((1,), (1,)), ((), ())), preferred_element_type=F32)
        for h in range(RET_HEADS):
            sc = (sc_heads[h * t:(h + 1) * t] * dm_scr[h]).astype(BF16)
            o = (jnp.dot(sc, v_c[:, h * RET_DV:(h + 1) * RET_DV], preferred_element_type=F32)
                 + cross[:, h * RET_DV:(h + 1) * RET_DV])
            mu = jnp.mean(o, axis=-1, keepdims=True)
            var = jnp.mean(jnp.square(o - mu), axis=-1, keepdims=True)
            outs.append((o - mu) * lax.rsqrt(var + GN_EPS))
        y = gate * _sigmoid(gate) * jnp.concatenate(outs, axis=1)
        if c == 0:
            co_ref[...] = y.astype(co_ref.dtype)
        else:
            o_ref[pl.ds((c - 1) * t, t), :] = y.astype(o_ref.dtype)
    if not need_ctx:
        co_ref[...] = jnp.zeros(co_ref.shape, co_ref.dtype)


def _retention(proj_lat, proj_ctx, lg, cos, sin, pmat, l, bsz, seq, n_ctx, need_ctx):
    t = RET_T
    ncl = seq // t
    assert n_ctx == t
    qb, kb = _MY_OFF['rq'] // RET_QK_WIDTH, _MY_OFF['rk'] // RET_QK_WIDTH
    vb, gb = _MY_OFF['rv'] // RET_WIDTH, _MY_OFF['rg'] // RET_WIDTH

    def col(n, w, j):
        return pl.BlockSpec((n, w), lambda b, j=j: (b, j))

    const = lambda shape: pl.BlockSpec(shape, lambda b: (0,) * len(shape))
    return pl.pallas_call(
        functools.partial(_ret_kernel, layer=l, need_ctx=need_ctx, ncl=ncl),
        grid=(bsz,),
        in_specs=[pl.BlockSpec(memory_space=pltpu.SMEM),
                  col(seq, RET_QK_WIDTH, qb), col(seq, RET_QK_WIDTH, kb), col(seq, RET_WIDTH, vb), col(seq, RET_WIDTH, gb),
                  col(n_ctx, RET_QK_WIDTH, qb), col(n_ctx, RET_QK_WIDTH, kb), col(n_ctx, RET_WIDTH, vb), col(n_ctx, RET_WIDTH, gb),
                  const((seq, RET_QK_WIDTH)), const((seq, RET_QK_WIDTH)), const((RET_QK_WIDTH, RET_QK_WIDTH))],
        out_specs=[pl.BlockSpec((seq, RET_WIDTH), lambda b: (b, 0)),
                   pl.BlockSpec((n_ctx, RET_WIDTH), lambda b: (b, 0))],
        out_shape=[jax.ShapeDtypeStruct((bsz * seq, RET_WIDTH), BF16),
                   jax.ShapeDtypeStruct((bsz * n_ctx, RET_WIDTH), BF16)],
        scratch_shapes=[pltpu.VMEM((seq + n_ctx, RET_QK_WIDTH), BF16),
                        pltpu.VMEM((ncl + 1, 2 * RET_QK_WIDTH, RET_WIDTH), BF16),
                        pltpu.VMEM((RET_HEADS, t, t), F32)],
        compiler_params=_cparams(("parallel",)),
        name="retention",
    )(lg, proj_lat, proj_lat, proj_lat, proj_lat, proj_ctx, proj_ctx, proj_ctx, proj_ctx, cos, sin, pmat)


def _na_block_start(kblk, rows):
    return jnp.clip(kblk * NA_QROWS - NA_ROWS // 2, 0, rows - NA_KROWS)


def _na_bias_table(rpb):
    depth, heads, nlag, ncol = rpb.shape
    cols = np.arange(GRID_W)
    cs = np.clip(cols - NA_COLS // 2, 0, GRID_W - NA_COLS)
    valid_c = (cols[None, :] >= cs[:, None]) & (cols[None, :] < cs[:, None] + NA_COLS)
    pad = GRID_W - 1
    padded = jnp.pad(rpb.astype(F32) * LOG2E, ((0, 0), (0, 0), (0, 0), (pad, pad)))
    toe = jnp.stack([padded[..., pad + NA_COLS - 1 - qc: pad + NA_COLS - 1 - qc + GRID_W] for qc in range(GRID_W)],
                    axis=-2)
    toe = jnp.where(jnp.asarray(valid_c), toe, NEG_INF)
    toe = jnp.pad(toe, ((0, 0), (0, 0), (1, NA_NLAG + 1 - nlag - 1), (0, 0), (0, 0)))
    table = jnp.concatenate([toe[:, :, :NA_NLAG], toe[:, :, 1:NA_NLAG + 1]], axis=-1)
    return table.reshape(depth, heads * NA_NLAG, GRID_W, 2 * GRID_W)


def _attend(q_pair, k_list, v_list, bias_list):
    nq = q_pair.shape[0]
    lane = lax.broadcasted_iota(jnp.int32, q_pair.shape, 1) // NA_HEAD_DIM
    zero = jnp.zeros_like(q_pair)
    q2 = jnp.concatenate([jnp.where(lane == 0, q_pair, zero), jnp.where(lane == 1, q_pair, zero)], axis=0)
    ss = []
    for k_i, b_i in zip(k_list, bias_list):
        s = lax.dot_general(q2, k_i, (((1,), (1,)), ((), ())), preferred_element_type=F32)
        if b_i is not None:
            s = s + jnp.concatenate([b_i(0), b_i(1)], axis=0)
        ss.append(s)
    m = ss[0].max(axis=-1, keepdims=True)
    for s in ss[1:]:
        m = jnp.maximum(m, s.max(axis=-1, keepdims=True))
    ps = [jnp.exp2(s - m) for s in ss]
    den = ps[0].sum(axis=-1, keepdims=True)
    for p in ps[1:]:
        den = den + p.sum(axis=-1, keepdims=True)
    acc = jnp.dot(ps[0].astype(BF16), v_list[0], preferred_element_type=F32)
    for p, v_i in zip(ps[1:], v_list[1:]):
        acc = acc + jnp.dot(p.astype(BF16), v_i, preferred_element_type=F32)
    out = acc / den
    return jnp.where(lane == 0, out[:nq], out[nq:])


def _attend_window(q_pair, k_win, k_ctx, v_win, v_ctx, tab_ref, hp, rel, starts):
    nq = q_pair.shape[0]
    npair = k_win.shape[0] // LANES
    lane = lax.broadcasted_iota(jnp.int32, q_pair.shape, 1) // NA_HEAD_DIM
    zero = jnp.zeros_like(q_pair)
    dn = (((1,), (1,)), ((), ()))
    q2 = jnp.concatenate([jnp.where(lane == 0, q_pair, zero), jnp.where(lane == 1, q_pair, zero)], axis=0)
    s_win = lax.dot_general(q2, k_win, dn, preferred_element_type=F32)
    s_ctx = lax.dot_general(q2, k_ctx, dn, preferred_element_type=F32)
    left = lax.broadcasted_iota(jnp.int32, (GRID_W, LANES), 1) < GRID_W
    neg_left = jnp.where(left, NEG_INF, 0.0)
    neg_right = jnp.where(left, 0.0, NEG_INF)
    p_win, p_ctx, inv = [], [], []
    for hh in range(2):
        h = 2 * hp + hh
        for qrl in range(NA_QROWS):
            r = slice(hh * nq + qrl * GRID_W, hh * nq + (qrl + 1) * GRID_W)
            st = starts[qrl]
            lo, hi = st // 2, (st + NA_ROWS + 1) // 2
            tiles = []
            for kp in range(lo, hi):
                lag = rel + 2 * kp - qrl + NA_ROWS - 1
                assert -1 <= lag <= NA_NLAG - 2
                t = tab_ref[0, h * NA_NLAG + lag + 1]
                if 2 * kp < st:
                    t = t + neg_left
                if 2 * kp + 1 >= st + NA_ROWS:
                    t = t + neg_right
                tiles.append(t)
            sw = s_win[r, lo * LANES:hi * LANES] + jnp.concatenate(tiles, axis=1)
            sc = s_ctx[r]
            m = jnp.maximum(sw.max(axis=-1, keepdims=True), sc.max(axis=-1, keepdims=True))
            pw, pc = jnp.exp2(sw - m), jnp.exp2(sc - m)
            inv.append(1.0 / (pw.sum(axis=-1, keepdims=True) + pc.sum(axis=-1, keepdims=True)))
            parts = [pw.astype(BF16)]
            if lo > 0:
                parts.insert(0, jnp.zeros((GRID_W, lo * LANES), BF16))
            if hi < npair:
                parts.append(jnp.zeros((GRID_W, (npair - hi) * LANES), BF16))
            p_win.append(jnp.concatenate(parts, axis=1) if len(parts) > 1 else parts[0])
            p_ctx.append(pc.astype(BF16))
    acc = (jnp.dot(jnp.concatenate(p_win, axis=0), v_win, preferred_element_type=F32)
           + jnp.dot(jnp.concatenate(p_ctx, axis=0), v_ctx, preferred_element_type=F32))
    out = jnp.concatenate([acc[i * GRID_W:(i + 1) * GRID_W] * inv[i] for i in range(len(inv))], axis=0)
    return jnp.where(lane == 0, out[:nq], out[nq:])


def _na_kernel(q_ref, k_ref, v_ref, ck_ref, cv_ref, tab_ref, o_ref, *, rows):
    nk = NA_KROWS * GRID_W
    nblk = rows // NA_QROWS
    kblk = pl.program_id(1)
    ks = pl.multiple_of(_na_block_start(kblk, rows) * GRID_W, NA_QROWS * GRID_W)

    def run(kb):
        r0 = kb * NA_QROWS
        ks_row = min(max(r0 - NA_ROWS // 2, 0), rows - NA_KROWS)
        starts = [min(max(r0 + qrl - NA_ROWS // 2, 0), rows - NA_ROWS) - ks_row for qrl in range(NA_QROWS)]
        for hp in range(NA_HEADS // 2):
            ln = pl.ds(hp * LANES, LANES)
            y = _attend_window(q_ref[:, ln], k_ref[pl.ds(ks, nk), ln], ck_ref[:, ln],
                               v_ref[pl.ds(ks, nk), ln], cv_ref[:, ln], tab_ref, hp, ks_row - r0, starts)
            o_ref[:, ln] = y.astype(o_ref.dtype)

    assert nblk >= 3 and all(_na_geometry(kb, rows) == _na_geometry(1, rows) for kb in range(1, nblk - 1))
    pl.when(kblk == 0)(lambda: run(0))
    pl.when((kblk > 0) & (kblk < nblk - 1))(lambda: run(1))
    pl.when(kblk == nblk - 1)(lambda: run(nblk - 1))


def _na_geometry(kb, rows):
    r0 = kb * NA_QROWS
    ks_row = min(max(r0 - NA_ROWS // 2, 0), rows - NA_KROWS)
    return (ks_row - r0,) + tuple(min(max(r0 + qrl - NA_ROWS // 2, 0), rows - NA_ROWS) - ks_row
                                  for qrl in range(NA_QROWS))


def _na(proj_lat, proj_ctx, table, l, bsz, seq, n_ctx):
    rows = seq // GRID_W
    nq = NA_QROWS * GRID_W
    nblk = seq // nq
    qb, kb, vb = (_MY_OFF[n] // NA_WIDTH for n in ('nq', 'nk', 'nv'))
    return pl.pallas_call(
        functools.partial(_na_kernel, rows=rows),
        grid=(bsz, nblk),
        in_specs=[pl.BlockSpec((nq, NA_WIDTH), lambda b, k: (b * nblk + k, qb)),
                  pl.BlockSpec((seq, NA_WIDTH), lambda b, k: (b, kb)),
                  pl.BlockSpec((seq, NA_WIDTH), lambda b, k: (b, vb)),
                  pl.BlockSpec((n_ctx, NA_WIDTH), lambda b, k: (b, kb)),
                  pl.BlockSpec((n_ctx, NA_WIDTH), lambda b, k: (b, vb)),
                  pl.BlockSpec((1,) + table.shape[1:], lambda b, k: (l, 0, 0, 0))],
        out_specs=pl.BlockSpec((nq, NA_WIDTH), lambda b, k: (b * nblk + k, 0)),
        out_shape=jax.ShapeDtypeStruct((bsz * seq, NA_WIDTH), BF16),
        compiler_params=_cparams(("parallel", "arbitrary")),
        name="neighborhood_attention",
    )(proj_lat, proj_lat, proj_lat, proj_ctx, proj_ctx, table)


def _ctx_attn_kernel(q_ref, k_ref, v_ref, o_ref):
    for hp in range(NA_HEADS // 2):
        ln = pl.ds(hp * LANES, LANES)
        y = _attend(q_ref[:, ln], [k_ref[:, ln]], [v_ref[:, ln]], [None])
        o_ref[:, ln] = y.astype(o_ref.dtype)


def _ctx_attn(proj_ctx, bsz, n_ctx):
    qb, kb, vb = (_MY_OFF[n] // NA_WIDTH for n in ('nq', 'nk', 'nv'))
    spec = lambda j: pl.BlockSpec((n_ctx, NA_WIDTH), lambda b: (b, j))
    return pl.pallas_call(
        _ctx_attn_kernel,
        grid=(bsz,),
        in_specs=[spec(qb), spec(kb), spec(vb)],
        out_specs=pl.BlockSpec((n_ctx, NA_WIDTH), lambda b: (b, 0)),
        out_shape=jax.ShapeDtypeStruct((bsz * n_ctx, NA_WIDTH), BF16),
        compiler_params=_cparams(("parallel",)),
        name="context_attention",
    )(proj_ctx, proj_ctx, proj_ctx)


def _gelu_tanh(x):
    return 0.5 * x * (1.0 + jnp.tanh(math.sqrt(2.0 / math.pi) * (x + 0.044715 * (x * x * x))))


def _merge_kernel(x_ref, g0_ref, g1_ref, g2_ref, yt_ref, yret_ref, yna_ref, wglu_ref, bglu_ref,
                  wbs5_ref, wbret_ref, wbna_ref, wout_ref, gate_ref, o_ref, w_scr):
    nrow = yt_ref.shape[1]
    for j in range(_NSLAB):
        for half in range(2):
            o = [yt_ref[j * _GPL + q, :, half * LANES:(half + 1) * LANES].astype(F32) for q in range(_GPL)]
            for tl, out in enumerate(_block_transpose(o)):
                w_scr[j, pl.ds(half * _GPL + tl, nrow, stride=S5_CHUNK), :] = out
    ge = _gelu_tanh(jnp.concatenate([w_scr[j] for j in range(_NSLAB)], axis=1))
    z = jnp.dot(ge.astype(BF16), wglu_ref[0], preferred_element_type=F32) + bglu_ref[0]
    s5 = (ge * _sigmoid(z)).astype(BF16)
    m = (_sigmoid(g0_ref[...].astype(F32)) * jnp.dot(s5, wbs5_ref[0], preferred_element_type=F32)
         + _sigmoid(g1_ref[...].astype(F32)) * jnp.dot(yret_ref[...], wbret_ref[0], preferred_element_type=F32)
         + _sigmoid(g2_ref[...].astype(F32)) * jnp.dot(yna_ref[...], wbna_ref[0], preferred_element_type=F32))
    o_ref[...] = x_ref[...] + gate_ref[0] * jnp.dot(m.astype(BF16), wout_ref[0], preferred_element_type=F32)


def _merge(x2, proj, y_t, yret, yna, mods, mod_row, l, wts, tm):
    m, d = x2.shape
    base = l * MOD_ROWS * 6
    rowblk = lambda w, j=0: pl.BlockSpec((tm, w), lambda i, j=j: (i, j))
    return pl.pallas_call(
        _merge_kernel,
        grid=(m // tm,),
        in_specs=[rowblk(d), rowblk(d, 0), rowblk(d, 1), rowblk(d, 2),
                  pl.BlockSpec((S5_GROUPS, tm // S5_CHUNK, S5_CHUNK * S5_GROUP), lambda i: (0, i, 0)),
                  rowblk(RET_WIDTH), rowblk(NA_WIDTH)]
                 + [_layer_spec(w, l) for w in wts]
                 + [pl.BlockSpec((1, 1, d), lambda i: (base + mod_row(i) * 6 + 2, 0, 0))],
        out_specs=rowblk(d),
        out_shape=jax.ShapeDtypeStruct((m, d), F32),
        scratch_shapes=[pltpu.VMEM((_NSLAB, tm, LANES), F32)],
        compiler_params=_cparams(("parallel",)),
        name="merge_residual",
    )(x2, proj, proj, proj, y_t, yret, yna, *wts, mods)


def _ffn_kernel(x_ref, sh_ref, sc_ref, gate_ref, wg_ref, wu_ref, wd_ref, fn_ref, o_ref, *, final, th):
    x = x_ref[...]
    h = (_rms(x) * (1.0 + sc_ref[0]) + sh_ref[0]).astype(BF16)
    hidden = wg_ref.shape[2]
    acc = jnp.zeros(x.shape, F32)
    for j in range(hidden // th):
        a = jnp.dot(h, wg_ref[0, :, j * th:(j + 1) * th], preferred_element_type=F32)
        b = jnp.dot(h, wu_ref[0, :, j * th:(j + 1) * th], preferred_element_type=F32)
        act = (a * _sigmoid(a) * b).astype(BF16)
        acc = acc + jnp.dot(act, wd_ref[0, j * th:(j + 1) * th, :], preferred_element_type=F32)
    y = x + gate_ref[0] * acc
    if final:
        y = _rms(y) * fn_ref[...]
    o_ref[...] = y


def _ffn(x2, mods, mod_row, l, wg, wu, wd, fn, tm, final):
    m, d = x2.shape
    base = l * MOD_ROWS * 6
    modspec = lambda k: pl.BlockSpec((1, 1, d), lambda i, k=k: (base + mod_row(i) * 6 + k, 0, 0))
    return pl.pallas_call(
        functools.partial(_ffn_kernel, final=final, th=256),
        grid=(m // tm,),
        in_specs=[pl.BlockSpec((tm, d), lambda i: (i, 0)), modspec(3), modspec(4), modspec(5),
                  _layer_spec(wg, l), _layer_spec(wu, l), _layer_spec(wd, l),
                  pl.BlockSpec(fn.shape, lambda i: (0, 0))],
        out_specs=pl.BlockSpec((tm, d), lambda i: (i, 0)),
        out_shape=jax.ShapeDtypeStruct((m, d), F32),
        compiler_params=_cparams(("parallel",)),
        name="swiglu_residual",
    )(x2, mods, mods, mods, wg, wu, wd, fn)


def kernel(x, c, ctx, c_ctx, w_ada, b_ada, w_in, s5_lam_re, s5_lam_im, s5_log_dt, s5_b_re, s5_b_im,
           s5_c_re, s5_c_im, s5_d, s5_w_glu, s5_b_glu, ret_theta, na_rpb, w_branch_s5, w_branch_ret,
           w_branch_na, w_out, w_ffn_gate, w_ffn_up, w_ffn_down, final_norm):
    bsz, seq, d = x.shape
    n_ctx = ctx.shape[1]
    depth = w_ada.shape[0]
    ctx_row = bsz
    assert bsz == SUBLANES and bsz + 1 <= MOD_ROWS

    cvec = jnp.zeros((MOD_ROWS, d), F32).at[:bsz].set(c).at[ctx_row].set(c_ctx)
    mods = _ada(cvec, w_ada, b_ada).reshape(depth * MOD_ROWS * 6, 1, d)

    w_in_k = jnp.concatenate(
        [(w_in[:, :, _REF_OFF[n]:_REF_OFF[n] + _REF_W[n]] * _COL_SCALE.get(n, 1.0)).astype(BF16) for n in _MY_ORDER],
        axis=2)

    cos, sin, pmat = _rotary_tables(seq)
    log_gamma = jax.nn.log_sigmoid(ret_theta.astype(F32)).reshape(depth, 2 * RET_HEADS)
    kin, vin, win, a_t = _s5_weights(s5_lam_re, s5_lam_im, s5_log_dt, s5_b_re, s5_b_im, s5_c_re, s5_c_im, s5_d)
    na_table = _na_bias_table(na_rpb)
    merge_w = (s5_w_glu.astype(BF16), s5_b_glu.reshape(depth, 1, -1).astype(F32), w_branch_s5.astype(BF16),
               w_branch_ret.astype(BF16), w_branch_na.astype(BF16), w_out.astype(BF16))
    ffn_w = (w_ffn_gate.astype(BF16), w_ffn_up.astype(BF16), w_ffn_down.astype(BF16),
             final_norm.reshape(1, d).astype(F32))

    tm_proj, tn_proj, tm = 1024, N_IN // 2, 512
    lat_row = lambda t: (lambda i: i // (seq // t))
    ctx_mod_row = lambda i: ctx_row

    x2 = x.reshape(bsz * seq, d)
    c2 = ctx.reshape(bsz * n_ctx, d)
    for l in range(depth):
        need_ctx = l < depth - 1
        proj_lat, u_lat = _inproj(x2, mods, lat_row(tm_proj), w_in_k, l, tm_proj, tn_proj)
        proj_ctx, u_ctx = _inproj(c2, mods, ctx_mod_row, w_in_k, l, tm_proj, tn_proj)

        ys5_ctx, ys5_lat = _s5(u_ctx, u_lat, kin, vin, win, a_t, l, bsz)
        yret_lat, yret_ctx = _retention(proj_lat, proj_ctx, log_gamma, cos, sin, pmat, l, bsz, seq, n_ctx, need_ctx)
        yna_lat = _na(proj_lat, proj_ctx, na_table, l, bsz, seq, n_ctx)

        x2 = _merge(x2, proj_lat, ys5_lat, yret_lat, yna_lat, mods, lat_row(tm), l, merge_w, tm)
        x2 = _ffn(x2, mods, lat_row(tm), l, *ffn_w, tm, final=not need_ctx)
        if need_ctx:
            yna_ctx = _ctx_attn(proj_ctx, bsz, n_ctx)
            c2 = _merge(c2, proj_ctx, ys5_ctx, yret_ctx, yna_ctx, mods, ctx_mod_row, l, merge_w, tm)
            c2 = _ffn(c2, mods, ctx_mod_row, l, *ffn_w, tm, final=False)
    return x2.reshape(bsz, seq, d)
```

```python
import functools
import math

import numpy as np
import jax
import jax.numpy as jnp
from jax import lax
from jax.experimental import pallas as pl
from jax.experimental.pallas import tpu as pltpu

F32 = jnp.float32
BF16 = jnp.bfloat16
HIGHEST = lax.Precision.HIGHEST

D_MODEL = 1024
GRID_W = 64
S5_WIDTH = 512
S5_GROUP = 16
S5_GROUPS = S5_WIDTH // S5_GROUP
S5_STATE = 64
S5_CHUNK = 16
RET_HEADS = 4
RET_DK = 64
RET_DV = 128
RET_QK_WIDTH = RET_HEADS * RET_DK
RET_WIDTH = RET_HEADS * RET_DV
RET_T = 256
NA_HEADS = 8
NA_HEAD_DIM = 64
NA_WIDTH = NA_HEADS * NA_HEAD_DIM
NA_ROWS = 8
NA_COLS = 16
NA_QROWS = 4
NA_KROWS = 12
NA_NLAG = 2 * NA_ROWS
N_BRANCH = 3
ROPE_BASE = 10000.0
RMS_EPS = 1e-6
GN_EPS = 1e-5
NEG_INF = -1e30
LANES = 128
SUBLANES = 8
MOD_ROWS = 16

_REF_SPLIT = (S5_WIDTH, RET_QK_WIDTH, RET_WIDTH, NA_WIDTH, NA_WIDTH,
              RET_QK_WIDTH, RET_WIDTH, NA_WIDTH, N_BRANCH * D_MODEL)
_REF_NAMES = ('u', 'rk', 'rv', 'nk', 'nv', 'rq', 'rg', 'nq', 'gates')
_REF_OFF = dict(zip(_REF_NAMES, np.concatenate([[0], np.cumsum(_REF_SPLIT)[:-1]]).tolist()))
_REF_W = dict(zip(_REF_NAMES, _REF_SPLIT))
_MY_ORDER = ('gates', 'rq', 'rk', 'u', 'rv', 'nk', 'nv', 'rg', 'nq')
_CTX_KV_BLOCK = 1
_MY_OFF = {}
_o = 0
for _n in _MY_ORDER:
    _MY_OFF[_n] = _o
    _o += _REF_W[_n]
N_IN = _o
LOG2E = math.log2(math.e)
_COL_SCALE = {'nq': NA_HEAD_DIM ** -0.5 * LOG2E, 'rk': RET_DK ** -0.5}

VMEM_LIMIT = 56 * 1024 * 1024


def _cparams(sem):
    return pltpu.CompilerParams(dimension_semantics=sem, vmem_limit_bytes=VMEM_LIMIT)


def _sigmoid(x):
    return 1.0 / (1.0 + jnp.exp(-x))


def _rms(x):
    return x * lax.rsqrt(jnp.mean(x * x, axis=-1, keepdims=True) + RMS_EPS)


def _layer_spec(arr, l):
    nd = arr.ndim
    return pl.BlockSpec((1,) + arr.shape[1:], lambda *_: (l,) + (0,) * (nd - 1))


def _ada_kernel(c_ref, w_ref, b_ref, o_ref):
    c = c_ref[...]
    s = c * _sigmoid(c)
    o_ref[0] = jnp.dot(s, w_ref[0], preferred_element_type=F32, precision=HIGHEST) + b_ref[0]


def _ada(cvec, w_ada, b_ada):
    depth, d, n = w_ada.shape
    tn = 1536
    rows = cvec.shape[0]
    return pl.pallas_call(
        _ada_kernel,
        grid=(depth, n // tn),
        in_specs=[pl.BlockSpec((rows, d), lambda l, j: (0, 0)),
                  pl.BlockSpec((1, d, tn), lambda l, j: (l, 0, j)),
                  pl.BlockSpec((1, 1, tn), lambda l, j: (l, 0, j))],
        out_specs=pl.BlockSpec((1, rows, tn), lambda l, j: (l, 0, j)),
        out_shape=jax.ShapeDtypeStruct((depth, rows, n), F32),
        compiler_params=_cparams(("parallel", "parallel")),
        name="ada_mod",
    )(cvec, w_ada, b_ada.reshape(depth, 1, n))


_GPL = LANES // S5_GROUP
_NSLAB = S5_WIDTH // LANES


def _block_transpose(tiles):
    blk = lax.broadcasted_iota(jnp.int32, tiles[0].shape, 1) // S5_GROUP
    tiles = list(tiles)
    s = _GPL // 2
    while s >= 1:
        hi = (blk & s) != 0
        for a in range(_GPL):
            if a & s:
                continue
            b = a + s
            ta, tb = tiles[a], tiles[b]
            tiles[a] = jnp.where(hi, pltpu.roll(tb, s * S5_GROUP, 1), ta)
            tiles[b] = jnp.where(hi, tb, pltpu.roll(ta, LANES - s * S5_GROUP, 1))
        s //= 2
    return tiles


def _inproj_kernel(x_ref, sh_ref, sc_ref, w_ref, o_ref, ut_ref, h_ref, u_scr, *, u_off, first_block):
    jcol = pl.program_id(1)

    @pl.when(jcol == first_block)
    def _():
        h = _rms(x_ref[...]) * (1.0 + sc_ref[0]) + sh_ref[0]
        h_ref[...] = h.astype(BF16)

    def project():
        res = jnp.dot(h_ref[...], w_ref[0], preferred_element_type=F32)
        o_ref[...] = res.astype(o_ref.dtype)
        for j in range(_NSLAB):
            u_scr[j] = res[:, u_off + j * LANES:u_off + (j + 1) * LANES]
        nrow = u_scr.shape[1] // S5_CHUNK
        for j in range(_NSLAB):
            for half in range(2):
                v = [u_scr[j, pl.ds(half * _GPL + tl, nrow, stride=S5_CHUNK), :] for tl in range(_GPL)]
                for q, out in enumerate(_block_transpose(v)):
                    ut_ref[j * _GPL + q, :, half * LANES:(half + 1) * LANES] = out.astype(ut_ref.dtype)

    if first_block == 0:
        project()
    else:
        @pl.when(jcol < first_block)
        def _():
            o_ref[...] = jnp.zeros(o_ref.shape, o_ref.dtype)

        pl.when(jcol >= first_block)(project)


def _inproj(x2, mods, mod_row, w, l, tm, tn, first_block=0):
    m, d = x2.shape
    n = w.shape[2]
    base = l * MOD_ROWS * 6
    nj = n // tn
    u_off = _MY_OFF['u'] - (nj - 1) * tn
    assert 0 <= u_off and u_off + S5_WIDTH <= tn
    return pl.pallas_call(
        functools.partial(_inproj_kernel, u_off=u_off, first_block=first_block),
        grid=(m // tm, nj),
        in_specs=[pl.BlockSpec((tm, d), lambda i, j: (i, 0)),
                  pl.BlockSpec((1, 1, d), lambda i, j: (base + mod_row(i) * 6 + 0, 0, 0)),
                  pl.BlockSpec((1, 1, d), lambda i, j: (base + mod_row(i) * 6 + 1, 0, 0)),
                  pl.BlockSpec((1, d, tn), lambda i, j: (l, 0, jnp.maximum(j, first_block)))],
        out_specs=[pl.BlockSpec((tm, tn), lambda i, j: (i, j)),
                   pl.BlockSpec((S5_GROUPS, tm // S5_CHUNK, S5_CHUNK * S5_GROUP), lambda i, j: (0, i, 0))],
        out_shape=[jax.ShapeDtypeStruct((m, n), BF16),
                   jax.ShapeDtypeStruct((S5_GROUPS, m // S5_CHUNK, S5_CHUNK * S5_GROUP), BF16)],
        scratch_shapes=[pltpu.VMEM((tm, d), BF16), pltpu.VMEM((_NSLAB, tm, LANES), F32)],
        compiler_params=_cparams(("parallel", "arbitrary")),
        name="in_proj",
    )(x2, mods, mods, w)


def _s5w_kernel(lam_ref, btr_ref, bti_ref, ctr_ref, cti_ref, dd_ref, kin_ref, vin_ref, win_ref, at_ref, *, ng):
    t_n, h_n = S5_CHUNK, S5_GROUP
    width = t_n * h_n
    lane = lax.broadcasted_iota(jnp.int32, (1, LANES), 1)
    f_lane = lane < S5_STATE
    tau = lax.broadcasted_iota(jnp.int32, (3 * SUBLANES, LANES), 0).astype(F32)
    lane_w = lax.broadcasted_iota(jnp.int32, (h_n, width), 1)
    for g in range(ng):
        lam_re, lam_im, dt = lam_ref[g, 0:1, :], lam_ref[g, 1:2, :], lam_ref[g, 2:3, :]
        mag = jnp.exp(tau * (lam_re * dt))
        ang = tau * (lam_im * dt)
        pr, pi = mag * jnp.cos(ang), mag * jnp.sin(ang)
        ab_re, ab_im = pr[1:2], pi[1:2]
        den = lam_re * lam_re + lam_im * lam_im
        f_re = ((ab_re - 1.0) * lam_re + ab_im * lam_im) / den
        f_im = (ab_im * lam_re - (ab_re - 1.0) * lam_im) / den
        btr, bti = btr_ref[g], bti_ref[g]
        bbr = f_re * btr - f_im * bti
        bbi = f_re * bti + f_im * btr
        ctr, cti = ctr_ref[g], cti_ref[g]

        def powers(pf, pb):
            rr = [jnp.broadcast_to(jnp.where(f_lane, pr[pf[t]:pf[t] + 1], pr[pb[t]:pb[t] + 1]), (h_n, LANES))
                  for t in range(t_n)]
            ri = [jnp.broadcast_to(jnp.where(f_lane, pi[pf[t]:pf[t] + 1], pi[pb[t]:pb[t] + 1]), (h_n, LANES))
                  for t in range(t_n)]
            return jnp.concatenate(rr, axis=0), jnp.concatenate(ri, axis=0)

        tile = lambda a: jnp.concatenate([a] * t_n, axis=0)
        bbr_t, bbi_t, ctr_t, cti_t = tile(bbr), tile(bbi), tile(ctr), tile(cti)

        xr, xi = powers([t_n - 1 - t for t in range(t_n)], list(range(t_n)))
        vin = jnp.concatenate([xr * bbr_t - xi * bbi_t, xr * bbi_t + xi * bbr_t], axis=1)
        vin_ref[g] = vin.astype(vin_ref.dtype)

        yr, yi = powers([t + 1 for t in range(t_n)], [t_n - t for t in range(t_n)])
        win_t = jnp.concatenate([ctr_t * yr - cti_t * yi, -(ctr_t * yi + cti_t * yr)], axis=1)
        win_ref[g] = win_t.T.astype(win_ref.dtype)

        zr, zi = powers(list(range(t_n)), [t_n - 1 - t for t in range(t_n)])
        fmat = jnp.concatenate([ctr_t * zr - cti_t * zi, ctr_t * zi + cti_t * zr], axis=1)
        dn = (((1,), (1,)), ((), ()))
        lhs_f = jnp.concatenate([jnp.where(f_lane, bbr, 0.0), jnp.where(f_lane, -bbi, 0.0)], axis=1)
        lhs_b = jnp.concatenate([jnp.where(f_lane, 0.0, bbr), jnp.where(f_lane, 0.0, -bbi)], axis=1)
        w_f = lax.dot_general(lhs_f, fmat, dn, preferred_element_type=F32, precision=HIGHEST)
        w_b = lax.dot_general(lhs_b, fmat, dn, preferred_element_type=F32, precision=HIGHEST)
        w_f = w_f + jnp.concatenate([dd_ref[g], jnp.zeros((h_n, width - LANES), F32)], axis=1)
        blocks = []
        for t in range(t_n):
            sh_f = t * h_n
            sh_b = (width - (t_n - 1 - t) * h_n) % width
            fw = w_f if sh_f == 0 else pltpu.roll(w_f, sh_f, 1)
            bw = w_b if sh_b == 0 else pltpu.roll(w_b, sh_b, 1)
            blocks.append(jnp.where(lane_w >= t * h_n, fw, 0.0) + jnp.where(lane_w < (t + 1) * h_n, bw, 0.0))
        kin_ref[g] = jnp.concatenate(blocks, axis=0).astype(kin_ref.dtype)
        at_ref[g] = jnp.broadcast_to(jnp.concatenate([pr[t_n:t_n + 1], pi[t_n:t_n + 1]], axis=1),
                                     (SUBLANES, 2 * LANES))


def _s5_weights(lam_re, lam_im, log_dt, b_re, b_im, c_re, c_im, d_skip, ng=4):
    depth, _, g_n, p_n = lam_re.shape
    h_n = S5_GROUP
    n = depth * g_n
    width = S5_CHUNK * h_n
    pair = lambda a: a.astype(F32).transpose(0, 2, 1, 3).reshape(n, 1, 2 * p_n)
    dt = jnp.broadcast_to(jnp.exp(log_dt.astype(F32))[..., None], lam_re.shape)
    lam = jnp.concatenate([pair(lam_re), pair(lam_im), pair(dt), jnp.zeros((n, SUBLANES - 3, 2 * p_n), F32)], axis=1)
    bt = lambda a: a.astype(F32).transpose(0, 2, 4, 1, 3).reshape(n, h_n, 2 * p_n)
    ct = lambda a: a.astype(F32).transpose(0, 2, 3, 1, 4).reshape(n, h_n, 2 * p_n)
    dd = jnp.eye(h_n, LANES, dtype=F32)[None] * d_skip.astype(F32).reshape(n, h_n, 1)
    small = pl.BlockSpec((ng, h_n, LANES), lambda i: (i, 0, 0))
    big = pl.BlockSpec((ng, width, width), lambda i: (i, 0, 0))
    return pl.pallas_call(
        functools.partial(_s5w_kernel, ng=ng),
        grid=(n // ng,),
        in_specs=[pl.BlockSpec((ng, SUBLANES, LANES), lambda i: (i, 0, 0)), small, small, small, small, small],
        out_specs=[big, big, big, pl.BlockSpec((ng, SUBLANES, width), lambda i: (i, 0, 0))],
        out_shape=[jax.ShapeDtypeStruct((n, width, width), BF16)] * 3
                  + [jax.ShapeDtypeStruct((n, SUBLANES, width), F32)],
        compiler_params=_cparams(("parallel",)),
        name="s5_weights",
    )(lam, bt(b_re), bt(b_im), ct(c_re), ct(c_im), dd)


def _s5_pitch(n):
    p = -(-n // SUBLANES)
    return (p | 1) * SUBLANES


def _s5_kernel(uc_ref, ul_ref, kin_ref, vin_ref, win_ref, a_ref, yc_ref, yl_ref,
               sc_scr, sl_scr, xac_scr, xal_scr, xbc_scr, xbl_scr, *, ng, ncc, ncl, bsz):
    half = 2 * S5_STATE
    pc, plat = _s5_pitch(ncc), _s5_pitch(ncl)
    segs = ((uc_ref, yc_ref, sc_scr, xac_scr, xbc_scr, ncc, pc), (ul_ref, yl_ref, sl_scr, xal_scr, xbl_scr, ncl, plat))
    for u_ref, _, s_scr, _, _, n, pitch in segs:
        for g in range(ng):
            s = jnp.dot(u_ref[g], vin_ref[g], preferred_element_type=F32)
            for b in range(bsz):
                for k in range(2):
                    s_scr[g, k, pl.ds(b * pitch, n), :] = s[b * n:(b + 1) * n, k * half:(k + 1) * half]
    lane = lax.broadcasted_iota(jnp.int32, (bsz, half), 1)
    fwd_lane = lane < S5_STATE
    a_re = [a_ref[g, :, :half] for g in range(ng)]
    a_im = [a_ref[g, :, half:] for g in range(ng)]

    def make_step(s_scr, xa_scr, xb_scr, pitch):
        def step(fc, bc, xs):
            rf = pl.ds(fc, bsz, stride=pitch)
            rb = pl.ds(bc, bsz, stride=pitch)
            out = []
            for g in range(ng):
                xr, xi = xs[2 * g], xs[2 * g + 1]
                xa_scr[g, 0, rf, :] = xr
                xa_scr[g, 1, rf, :] = xi
                xb_scr[g, 0, rb, :] = xr
                xb_scr[g, 1, rb, :] = xi
                sr = jnp.where(fwd_lane, s_scr[g, 0, rf, :], s_scr[g, 0, rb, :])
                si = jnp.where(fwd_lane, s_scr[g, 1, rf, :], s_scr[g, 1, rb, :])
                out.append(a_re[g] * xr - a_im[g] * xi + sr)
                out.append(a_re[g] * xi + a_im[g] * xr + si)
            return tuple(out)
        return step

    step_c = make_step(sc_scr, xac_scr, xbc_scr, pc)
    step_l = make_step(sl_scr, xal_scr, xbl_scr, plat)
    xs = tuple(jnp.zeros((bsz, half), F32) for _ in range(2 * ng))
    xs = lax.fori_loop(0, ncc, lambda i, c: step_c(i, ncc - 1 - i, c), xs, unroll=4)
    xs = lax.fori_loop(0, ncl, lambda i, c: step_l(i, ncl - 1 - i, c), xs, unroll=4)
    for u_ref, y_ref, _, xa_scr, xb_scr, n, pitch in segs:
        fwd_r = lax.broadcasted_iota(jnp.int32, (n, half), 1) < S5_STATE
        for g in range(ng):
            rows = []
            for b in range(bsz):
                r = pl.ds(b * pitch, n)
                rows.append(jnp.concatenate([jnp.where(fwd_r, xa_scr[g, k, r, :], xb_scr[g, k, r, :]) for k in range(2)],
                                            axis=1))
            x_in = jnp.concatenate(rows, axis=0).astype(BF16)
            y = (jnp.dot(u_ref[g], kin_ref[g], preferred_element_type=F32)
                 + jnp.dot(x_in, win_ref[g], preferred_element_type=F32))
            y_ref[g] = y.astype(y_ref.dtype)


def _s5(u_ctx, u_lat, kin, vin, win, a_t, l, bsz, ng=4):
    g_n, rc, w = u_ctx.shape
    rl = u_lat.shape[1]
    ncc, ncl = rc // bsz, rl // bsz
    nblk = g_n // ng
    wspec = pl.BlockSpec((ng, w, w), lambda i: (l * nblk + i, 0, 0))
    uspec = lambda r: pl.BlockSpec((ng, r, w), lambda i: (i, 0, 0))
    scr = lambda n: pltpu.VMEM((ng, 2, bsz * _s5_pitch(n), LANES), F32)
    return pl.pallas_call(
        functools.partial(_s5_kernel, ng=ng, ncc=ncc, ncl=ncl, bsz=bsz),
        grid=(nblk,),
        in_specs=[uspec(rc), uspec(rl), wspec, wspec, wspec,
                  pl.BlockSpec((ng, SUBLANES, w), lambda i: (l * nblk + i, 0, 0))],
        out_specs=[uspec(rc), uspec(rl)],
        out_shape=[jax.ShapeDtypeStruct(u_ctx.shape, BF16), jax.ShapeDtypeStruct(u_lat.shape, BF16)],
        scratch_shapes=[scr(ncc), scr(ncl)] * 3,
        compiler_params=_cparams(("parallel",)),
        name="s5_mixer",
    )(u_ctx, u_lat, kin, vin, win, a_t)


def _rotary_tables(seq):
    quarter = RET_DK // 4
    pos = jnp.arange(seq)
    inv_freq = ROPE_BASE ** (-jnp.arange(quarter, dtype=F32) / quarter)
    ang_r = (pos // GRID_W).astype(F32)[:, None] * inv_freq[None, :]
    ang_c = (pos % GRID_W).astype(F32)[:, None] * inv_freq[None, :]
    cos = jnp.concatenate([jnp.cos(ang_r)] * 2 + [jnp.cos(ang_c)] * 2, axis=-1)
    sin = jnp.concatenate([jnp.sin(ang_r)] * 2 + [jnp.sin(ang_c)] * 2, axis=-1)
    cos = jnp.tile(cos, (1, RET_HEADS))
    sin = jnp.tile(sin, (1, RET_HEADS))
    p = np.zeros((RET_QK_WIDTH, RET_QK_WIDTH), np.float32)
    for d in range(RET_QK_WIDTH):
        if d % (2 * quarter) < quarter:
            p[d + quarter, d] = -1.0
        else:
            p[d - quarter, d] = 1.0
    return cos, sin, jnp.asarray(p, BF16)


def _ret_kernel(lg_ref, q_ref, k_ref, v_ref, g_ref, cq_ref, ck_ref, cv_ref, cg_ref,
                cos_ref, sin_ref, p_ref, o_ref, co_ref, krot_scr, sin_scr, dm_scr, *, layer, need_ctx, ncl):
    t = RET_T
    qkw, vw = RET_QK_WIDTH, RET_WIDTH

    def per_head(shape, axis, width, d):
        head = lax.broadcasted_iota(jnp.int32, shape, axis) // width
        out = jnp.zeros(shape, F32)
        for h in range(RET_HEADS):
            out = jnp.where(head == h, lg_ref[layer, d * RET_HEADS + h], out)
        return out

    row = lax.broadcasted_iota(jnp.int32, (t, qkw), 0).astype(F32)
    lgf = per_head((t, qkw), 1, RET_DK, 0)
    lgb = per_head((t, qkw), 1, RET_DK, 1)
    qdec_f = jnp.exp((row + 1.0) * lgf)
    qdec_b = jnp.exp((t - row) * lgb)
    kdec_f = jnp.exp((t - 1.0 - row) * lgf)
    kdec_b = jnp.exp(row * lgb)
    cdec_f = jnp.exp(float(t) * per_head((qkw, vw), 0, RET_DK, 0))
    cdec_b = jnp.exp(float(t) * per_head((qkw, vw), 0, RET_DK, 1))
    blk = (lax.broadcasted_iota(jnp.int32, (qkw, vw), 0) // RET_DK
           == lax.broadcasted_iota(jnp.int32, (qkw, vw), 1) // RET_DV)
    head_lane = lax.broadcasted_iota(jnp.int32, (t, qkw), 1) // RET_DK

    ii = lax.broadcasted_iota(jnp.int32, (t, t), 0)
    jj = lax.broadcasted_iota(jnp.int32, (t, t), 1)
    dif = (ii - jj).astype(F32)
    for h in range(RET_HEADS):
        df = jnp.where(dif >= 0, jnp.exp(jnp.where(dif >= 0, dif, 0.0) * lg_ref[layer, h]), 0.0)
        db = jnp.where(dif < 0, jnp.exp(jnp.where(dif < 0, -dif, 0.0) * lg_ref[layer, RET_HEADS + h]), 0.0)
        dm_scr[h] = df + db

    def rotary(x_bf, c):
        rows = pl.ds(c * t, t)
        swapped = jnp.dot(x_bf, p_ref[...], preferred_element_type=F32)
        return x_bf.astype(F32) * cos_ref[rows, :] + swapped * sin_ref[rows, :]

    krot_scr[pl.ds(0, t), :] = ck_ref[...]
    for c in range(ncl):
        krot_scr[pl.ds((c + 1) * t, t), :] = rotary(k_ref[pl.ds(c * t, t), :], c).astype(BF16)

    def v_chunk(c):
        return cv_ref[...] if c == 0 else v_ref[pl.ds((c - 1) * t, t), :]

    def kv(c, kdec):
        kd = (krot_scr[pl.ds(c * t, t), :].astype(F32) * kdec).astype(BF16)
        return lax.dot_general(kd, v_chunk(c), (((0,), (0,)), ((), ())), preferred_element_type=F32)

    s = jnp.zeros((qkw, vw), F32)
    for c in range(ncl + 1):
        sin_scr[c, pl.ds(0, qkw), :] = jnp.where(blk, s, 0.0).astype(BF16)
        if c < ncl:
            s = cdec_f * s + kv(c, kdec_f)
    sin_scr[0, pl.ds(qkw, qkw), :] = jnp.zeros((qkw, vw), BF16)
    s = kv(0, kdec_b)
    for c in range(ncl, 0, -1):
        sin_scr[c, pl.ds(qkw, qkw), :] = jnp.where(blk, s, 0.0).astype(BF16)
        if c > 1:
            s = cdec_b * s + kv(c, kdec_b)

    for c in range(0 if need_ctx else 1, ncl + 1):
        if c == 0:
            q = cq_ref[...].astype(F32)
            gate = cg_ref[...].astype(F32)
        else:
            q = rotary(q_ref[pl.ds((c - 1) * t, t), :], c - 1)
            gate = g_ref[pl.ds((c - 1) * t, t), :].astype(F32)
        q_bf = q.astype(BF16)
        cross = (jnp.dot((q * qdec_f).astype(BF16), sin_scr[c, pl.ds(0, qkw), :], preferred_element_type=F32)
                 + jnp.dot((q * qdec_b).astype(BF16), sin_scr[c, pl.ds(qkw, qkw), :], preferred_element_type=F32))
        k_c = krot_scr[pl.ds(c * t, t), :]
        v_c = v_chunk(c)
        outs = []
        q_heads = jnp.concatenate([jnp.where(head_lane == h, q_bf, jnp.zeros_like(q_bf)) for h in range(RET_HEADS)],
                                  axis=0)
        sc_heads = lax.dot_general(q_heads, k_c, (((1,), (1,)), ((), ())), preferred_element_type=F32)
        for h in range(RET_HEADS):
            sc = (sc_heads[h * t:(h + 1) * t] * dm_scr[h]).astype(BF16)
            o = (jnp.dot(sc, v_c[:, h * RET_DV:(h + 1) * RET_DV], preferred_element_type=F32)
                 + cross[:, h * RET_DV:(h + 1) * RET_DV])
            mu = jnp.mean(o, axis=-1, keepdims=True)
            var = jnp.mean(jnp.square(o - mu), axis=-1, keepdims=True)
            outs.append((o - mu) * lax.rsqrt(var + GN_EPS))
        y = gate * _sigmoid(gate) * jnp.concatenate(outs, axis=1)
        if c == 0:
            co_ref[...] = y.astype(co_ref.dtype)
        else:
            o_ref[pl.ds((c - 1) * t, t), :] = y.astype(o_ref.dtype)
    if not need_ctx:
        co_ref[...] = jnp.zeros(co_ref.shape, co_ref.dtype)


def _retention(proj_lat, proj_ctx, lg, cos, sin, pmat, l, bsz, seq, n_ctx, need_ctx):
    t = RET_T
    ncl = seq // t
    assert n_ctx == t
    qb, kb = _MY_OFF['rq'] // RET_QK_WIDTH, _MY_OFF['rk'] // RET_QK_WIDTH
    vb, gb = _MY_OFF['rv'] // RET_WIDTH, _MY_OFF['rg'] // RET_WIDTH

    def col(n, w, j):
        return pl.BlockSpec((n, w), lambda b, j=j: (b, j))

    const = lambda shape: pl.BlockSpec(shape, lambda b: (0,) * len(shape))
    return pl.pallas_call(
        functools.partial(_ret_kernel, layer=l, need_ctx=need_ctx, ncl=ncl),
        grid=(bsz,),
        in_specs=[pl.BlockSpec(memory_space=pltpu.SMEM),
                  col(seq, RET_QK_WIDTH, qb), col(seq, RET_QK_WIDTH, kb), col(seq, RET_WIDTH, vb), col(seq, RET_WIDTH, gb),
                  col(n_ctx, RET_QK_WIDTH, qb), col(n_ctx, RET_QK_WIDTH, kb), col(n_ctx, RET_WIDTH, vb), col(n_ctx, RET_WIDTH, gb),
                  const((seq, RET_QK_WIDTH)), const((seq, RET_QK_WIDTH)), const((RET_QK_WIDTH, RET_QK_WIDTH))],
        out_specs=[pl.BlockSpec((seq, RET_WIDTH), lambda b: (b, 0)),
                   pl.BlockSpec((n_ctx, RET_WIDTH), lambda b: (b, 0))],
        out_shape=[jax.ShapeDtypeStruct((bsz * seq, RET_WIDTH), BF16),
                   jax.ShapeDtypeStruct((bsz * n_ctx, RET_WIDTH), BF16)],
        scratch_shapes=[pltpu.VMEM((seq + n_ctx, RET_QK_WIDTH), BF16),
                        pltpu.VMEM((ncl + 1, 2 * RET_QK_WIDTH, RET_WIDTH), BF16),
                        pltpu.VMEM((RET_HEADS, t, t), F32)],
        compiler_params=_cparams(("parallel",)),
        name="retention",
    )(lg, proj_lat, proj_lat, proj_lat, proj_lat, proj_ctx, proj_ctx, proj_ctx, proj_ctx, cos, sin, pmat)


def _na_block_start(kblk, rows):
    return jnp.clip(kblk * NA_QROWS - NA_ROWS // 2, 0, rows - NA_KROWS)


def _na_bias_table(rpb):
    depth, heads, nlag, ncol = rpb.shape
    cols = np.arange(GRID_W)
    cs = np.clip(cols - NA_COLS // 2, 0, GRID_W - NA_COLS)
    valid_c = (cols[None, :] >= cs[:, None]) & (cols[None, :] < cs[:, None] + NA_COLS)
    pad = GRID_W - 1
    padded = jnp.pad(rpb.astype(F32) * LOG2E, ((0, 0), (0, 0), (0, 0), (pad, pad)))
    toe = jnp.stack([padded[..., pad + NA_COLS - 1 - qc: pad + NA_COLS - 1 - qc + GRID_W] for qc in range(GRID_W)],
                    axis=-2)
    toe = jnp.where(jnp.asarray(valid_c), toe, NEG_INF)
    toe = jnp.pad(toe, ((0, 0), (0, 0), (1, NA_NLAG + 1 - nlag - 1), (0, 0), (0, 0)))
    table = jnp.concatenate([toe[:, :, :NA_NLAG], toe[:, :, 1:NA_NLAG + 1]], axis=-1)
    return table.reshape(depth, heads * NA_NLAG, GRID_W, 2 * GRID_W)


def _attend(q_pair, k_list, v_list, bias_list):
    nq = q_pair.shape[0]
    lane = lax.broadcasted_iota(jnp.int32, q_pair.shape, 1) // NA_HEAD_DIM
    zero = jnp.zeros_like(q_pair)
    q2 = jnp.concatenate([jnp.where(lane == 0, q_pair, zero), jnp.where(lane == 1, q_pair, zero)], axis=0)
    ss = []
    for k_i, b_i in zip(k_list, bias_list):
        s = lax.dot_general(q2, k_i, (((1,), (1,)), ((), ())), preferred_element_type=F32)
        if b_i is not None:
            s = s + jnp.concatenate([b_i(0), b_i(1)], axis=0)
        ss.append(s)
    m = ss[0].max(axis=-1, keepdims=True)
    for s in ss[1:]:
        m = jnp.maximum(m, s.max(axis=-1, keepdims=True))
    ps = [jnp.exp2(s - m) for s in ss]
    den = ps[0].sum(axis=-1, keepdims=True)
    for p in ps[1:]:
        den = den + p.sum(axis=-1, keepdims=True)
    acc = jnp.dot(ps[0].astype(BF16), v_list[0], preferred_element_type=F32)
    for p, v_i in zip(ps[1:], v_list[1:]):
        acc = acc + jnp.dot(p.astype(BF16), v_i, preferred_element_type=F32)
    out = acc / den
    return jnp.where(lane == 0, out[:nq], out[nq:])


def _attend_window(q_pair, k_win, k_ctx, v_win, v_ctx, tab_ref, hp, rel, starts):
    nq = q_pair.shape[0]
    npair = k_win.shape[0] // LANES
    lane = lax.broadcasted_iota(jnp.int32, q_pair.shape, 1) // NA_HEAD_DIM
    zero = jnp.zeros_like(q_pair)
    dn = (((1,), (1,)), ((), ()))
    q2 = jnp.concatenate([jnp.where(lane == 0, q_pair, zero), jnp.where(lane == 1, q_pair, zero)], axis=0)
    s_win = lax.dot_general(q2, k_win, dn, preferred_element_type=F32)
    s_ctx = lax.dot_general(q2, k_ctx, dn, preferred_element_type=F32)
    left = lax.broadcasted_iota(jnp.int32, (GRID_W, LANES), 1) < GRID_W
    neg_left = jnp.where(left, NEG_INF, 0.0)
    neg_right = jnp.where(left, 0.0, NEG_INF)
    p_win, p_ctx, inv = [], [], []
    for hh in range(2):
        h = 2 * hp + hh
        for qrl in range(NA_QROWS):
            r = slice(hh * nq + qrl * GRID_W, hh * nq + (qrl + 1) * GRID_W)
            st = starts[qrl]
            lo, hi = st // 2, (st + NA_ROWS + 1) // 2
            tiles = []
            for kp in range(lo, hi):
                lag = rel + 2 * kp - qrl + NA_ROWS - 1
                assert -1 <= lag <= NA_NLAG - 2
                t = tab_ref[0, h * NA_NLAG + lag + 1]
                if 2 * kp < st:
                    t = t + neg_left
                if 2 * kp + 1 >= st + NA_ROWS:
                    t = t + neg_right
                tiles.append(t)
            sw = s_win[r, lo * LANES:hi * LANES] + jnp.concatenate(tiles, axis=1)
            sc = s_ctx[r]
            m = jnp.maximum(sw.max(axis=-1, keepdims=True), sc.max(axis=-1, keepdims=True))
            pw, pc = jnp.exp2(sw - m), jnp.exp2(sc - m)
            inv.append(1.0 / (pw.sum(axis=-1, keepdims=True) + pc.sum(axis=-1, keepdims=True)))
            parts = [pw.astype(BF16)]
            if lo > 0:
                parts.insert(0, jnp.zeros((GRID_W, lo * LANES), BF16))
            if hi < npair:
                parts.append(jnp.zeros((GRID_W, (npair - hi) * LANES), BF16))
            p_win.append(jnp.concatenate(parts, axis=1) if len(parts) > 1 else parts[0])
            p_ctx.append(pc.astype(BF16))
    acc = (jnp.dot(jnp.concatenate(p_win, axis=0), v_win, preferred_element_type=F32)
           + jnp.dot(jnp.concatenate(p_ctx, axis=0), v_ctx, preferred_element_type=F32))
    out = jnp.concatenate([acc[i * GRID_W:(i + 1) * GRID_W] * inv[i] for i in range(len(inv))], axis=0)
    return jnp.where(lane == 0, out[:nq], out[nq:])


def _na_kernel(q_ref, k_ref, v_ref, ck_ref, cv_ref, tab_ref, o_ref, *, rows):
    nk = NA_KROWS * GRID_W
    nblk = rows // NA_QROWS
    kblk = pl.program_id(1)
    ks = pl.multiple_of(_na_block_start(kblk, rows) * GRID_W, NA_QROWS * GRID_W)

    def run(kb):
        r0 = kb * NA_QROWS
        ks_row = min(max(r0 - NA_ROWS // 2, 0), rows - NA_KROWS)
        starts = [min(max(r0 + qrl - NA_ROWS // 2, 0), rows - NA_ROWS) - ks_row for qrl in range(NA_QROWS)]
        for hp in range(NA_HEADS // 2):
            ln = pl.ds(hp * LANES, LANES)
            y = _attend_window(q_ref[:, ln], k_ref[pl.ds(ks, nk), ln], ck_ref[:, ln],
                               v_ref[pl.ds(ks, nk), ln], cv_ref[:, ln], tab_ref, hp, ks_row - r0, starts)
            o_ref[:, ln] = y.astype(o_ref.dtype)

    assert nblk >= 3 and all(_na_geometry(kb, rows) == _na_geometry(1, rows) for kb in range(1, nblk - 1))
    pl.when(kblk == 0)(lambda: run(0))
    pl.when((kblk > 0) & (kblk < nblk - 1))(lambda: run(1))
    pl.when(kblk == nblk - 1)(lambda: run(nblk - 1))


def _na_geometry(kb, rows):
    r0 = kb * NA_QROWS
    ks_row = min(max(r0 - NA_ROWS // 2, 0), rows - NA_KROWS)
    return (ks_row - r0,) + tuple(min(max(r0 + qrl - NA_ROWS // 2, 0), rows - NA_ROWS) - ks_row
                                  for qrl in range(NA_QROWS))


def _na(proj_lat, proj_ctx, table, l, bsz, seq, n_ctx):
    rows = seq // GRID_W
    nq = NA_QROWS * GRID_W
    nblk = seq // nq
    qb, kb, vb = (_MY_OFF[n] // NA_WIDTH for n in ('nq', 'nk', 'nv'))
    return pl.pallas_call(
        functools.partial(_na_kernel, rows=rows),
        grid=(bsz, nblk),
        in_specs=[pl.BlockSpec((nq, NA_WIDTH), lambda b, k: (b * nblk + k, qb)),
                  pl.BlockSpec((seq, NA_WIDTH), lambda b, k: (b, kb)),
                  pl.BlockSpec((seq, NA_WIDTH), lambda b, k: (b, vb)),
                  pl.BlockSpec((n_ctx, NA_WIDTH), lambda b, k: (b, kb)),
                  pl.BlockSpec((n_ctx, NA_WIDTH), lambda b, k: (b, vb)),
                  pl.BlockSpec((1,) + table.shape[1:], lambda b, k: (l, 0, 0, 0))],
        out_specs=pl.BlockSpec((nq, NA_WIDTH), lambda b, k: (b * nblk + k, 0)),
        out_shape=jax.ShapeDtypeStruct((bsz * seq, NA_WIDTH), BF16),
        compiler_params=_cparams(("parallel", "arbitrary")),
        name="neighborhood_attention",
    )(proj_lat, proj_lat, proj_lat, proj_ctx, proj_ctx, table)


def _ctx_attn_kernel(q_ref, k_ref, v_ref, o_ref):
    for hp in range(NA_HEADS // 2):
        ln = pl.ds(hp * LANES, LANES)
        y = _attend(q_ref[:, ln], [k_ref[:, ln]], [v_ref[:, ln]], [None])
        o_ref[:, ln] = y.astype(o_ref.dtype)


def _ctx_attn(proj_ctx, bsz, n_ctx):
    qb, kb, vb = (_MY_OFF[n] // NA_WIDTH for n in ('nq', 'nk', 'nv'))
    spec = lambda j: pl.BlockSpec((n_ctx, NA_WIDTH), lambda b: (b, j))
    return pl.pallas_call(
        _ctx_attn_kernel,
        grid=(bsz,),
        in_specs=[spec(qb), spec(kb), spec(vb)],
        out_specs=pl.BlockSpec((n_ctx, NA_WIDTH), lambda b: (b, 0)),
        out_shape=jax.ShapeDtypeStruct((bsz * n_ctx, NA_WIDTH), BF16),
        compiler_params=_cparams(("parallel",)),
        name="context_attention",
    )(proj_ctx, proj_ctx, proj_ctx)


def _gelu_tanh(x):
    return 0.5 * x * (1.0 + jnp.tanh(math.sqrt(2.0 / math.pi) * (x + 0.044715 * (x * x * x))))


def _merge_kernel(x_ref, g0_ref, g1_ref, g2_ref, yt_ref, yret_ref, yna_ref, wglu_ref, bglu_ref,
                  wbs5_ref, wbret_ref, wbna_ref, wout_ref, gate_ref, o_ref, w_scr):
    nrow = yt_ref.shape[1]
    for j in range(_NSLAB):
        for half in range(2):
            o = [yt_ref[j * _GPL + q, :, half * LANES:(half + 1) * LANES].astype(F32) for q in range(_GPL)]
            for tl, out in enumerate(_block_transpose(o)):
                w_scr[j, pl.ds(half * _GPL + tl, nrow, stride=S5_CHUNK), :] = out
    ge = _gelu_tanh(jnp.concatenate([w_scr[j] for j in range(_NSLAB)], axis=1))
    z = jnp.dot(ge.astype(BF16), wglu_ref[0], preferred_element_type=F32) + bglu_ref[0]
    s5 = (ge * _sigmoid(z)).astype(BF16)
    m = (_sigmoid(g0_ref[...].astype(F32)) * jnp.dot(s5, wbs5_ref[0], preferred_element_type=F32)
         + _sigmoid(g1_ref[...].astype(F32)) * jnp.dot(yret_ref[...], wbret_ref[0], preferred_element_type=F32)
         + _sigmoid(g2_ref[...].astype(F32)) * jnp.dot(yna_ref[...], wbna_ref[0], preferred_element_type=F32))
    o_ref[...] = x_ref[...] + gate_ref[0] * jnp.dot(m.astype(BF16), wout_ref[0], preferred_element_type=F32)


def _merge(x2, proj, y_t, yret, yna, mods, mod_row, l, wts, tm):
    m, d = x2.shape
    base = l * MOD_ROWS * 6
    rowblk = lambda w, j=0: pl.BlockSpec((tm, w), lambda i, j=j: (i, j))
    return pl.pallas_call(
        _merge_kernel,
        grid=(m // tm,),
        in_specs=[rowblk(d), rowblk(d, 0), rowblk(d, 1), rowblk(d, 2),
                  pl.BlockSpec((S5_GROUPS, tm // S5_CHUNK, S5_CHUNK * S5_GROUP), lambda i: (0, i, 0)),
                  rowblk(RET_WIDTH), rowblk(NA_WIDTH)]
                 + [_layer_spec(w, l) for w in wts]
                 + [pl.BlockSpec((1, 1, d), lambda i: (base + mod_row(i) * 6 + 2, 0, 0))],
        out_specs=rowblk(d),
        out_shape=jax.ShapeDtypeStruct((m, d), F32),
        scratch_shapes=[pltpu.VMEM((_NSLAB, tm, LANES), F32)],
        compiler_params=_cparams(("parallel",)),
        name="merge_residual",
    )(x2, proj, proj, proj, y_t, yret, yna, *wts, mods)


def _ffn_kernel(x_ref, sh_ref, sc_ref, gate_ref, wg_ref, wu_ref, wd_ref, fn_ref, o_ref, *, final, th):
    x = x_ref[...]
    h = (_rms(x) * (1.0 + sc_ref[0]) + sh_ref[0]).astype(BF16)
    hidden = wg_ref.shape[2]
    acc = jnp.zeros(x.shape, F32)
    for j in range(hidden // th):
        a = jnp.dot(h, wg_ref[0, :, j * th:(j + 1) * th], preferred_element_type=F32)
        b = jnp.dot(h, wu_ref[0, :, j * th:(j + 1) * th], preferred_element_type=F32)
        act = (a * _sigmoid(a) * b).astype(BF16)
        acc = acc + jnp.dot(act, wd_ref[0, j * th:(j + 1) * th, :], preferred_element_type=F32)
    y = x + gate_ref[0] * acc
    if final:
        y = _rms(y) * fn_ref[...]
    o_ref[...] = y


def _ffn(x2, mods, mod_row, l, wg, wu, wd, fn, tm, final):
    m, d = x2.shape
    base = l * MOD_ROWS * 6
    modspec = lambda k: pl.BlockSpec((1, 1, d), lambda i, k=k: (base + mod_row(i) * 6 + k, 0, 0))
    return pl.pallas_call(
        functools.partial(_ffn_kernel, final=final, th=256),
        grid=(m // tm,),
        in_specs=[pl.BlockSpec((tm, d), lambda i: (i, 0)), modspec(3), modspec(4), modspec(5),
                  _layer_spec(wg, l), _layer_spec(wu, l), _layer_spec(wd, l),
                  pl.BlockSpec(fn.shape, lambda i: (0, 0))],
        out_specs=pl.BlockSpec((tm, d), lambda i: (i, 0)),
        out_shape=jax.ShapeDtypeStruct((m, d), F32),
        compiler_params=_cparams(("parallel",)),
        name="swiglu_residual",
    )(x2, mods, mods, mods, wg, wu, wd, fn)


def kernel(x, c, ctx, c_ctx, w_ada, b_ada, w_in, s5_lam_re, s5_lam_im, s5_log_dt, s5_b_re, s5_b_im,
           s5_c_re, s5_c_im, s5_d, s5_w_glu, s5_b_glu, ret_theta, na_rpb, w_branch_s5, w_branch_ret,
           w_branch_na, w_out, w_ffn_gate, w_ffn_up, w_ffn_down, final_norm):
    bsz, seq, d = x.shape
    n_ctx = ctx.shape[1]
    depth = w_ada.shape[0]
    ctx_row = bsz
    assert bsz == SUBLANES and bsz + 1 <= MOD_ROWS

    cvec = jnp.zeros((MOD_ROWS, d), F32).at[:bsz].set(c).at[ctx_row].set(c_ctx)
    mods = _ada(cvec, w_ada, b_ada).reshape(depth * MOD_ROWS * 6, 1, d)

    w_in_k = jnp.concatenate(
        [w_in[:, :, _REF_OFF[n]:_REF_OFF[n] + _REF_W[n]] * _COL_SCALE.get(n, 1.0) for n in _MY_ORDER],
        axis=2).astype(BF16)

    cos, sin, pmat = _rotary_tables(seq)
    log_gamma = jax.nn.log_sigmoid(ret_theta.astype(F32)).reshape(depth, 2 * RET_HEADS)
    kin, vin, win, a_t = _s5_weights(s5_lam_re, s5_lam_im, s5_log_dt, s5_b_re, s5_b_im, s5_c_re, s5_c_im, s5_d)
    na_table = _na_bias_table(na_rpb)
    merge_w = (s5_w_glu.astype(BF16), s5_b_glu.reshape(depth, 1, -1).astype(F32), w_branch_s5.astype(BF16),
               w_branch_ret.astype(BF16), w_branch_na.astype(BF16), w_out.astype(BF16))
    ffn_w = (w_ffn_gate.astype(BF16), w_ffn_up.astype(BF16), w_ffn_down.astype(BF16),
             final_norm.reshape(1, d).astype(F32))

    tm_proj, tn_proj, tm = 1024, N_IN // 2, 512
    assert all(_MY_OFF[n] >= _CTX_KV_BLOCK * tn_proj for n in ('u', 'rk', 'rv', 'nk', 'nv'))
    lat_row = lambda t: (lambda i: i // (seq // t))
    ctx_mod_row = lambda i: ctx_row

    x2 = x.reshape(bsz * seq, d)
    c2 = ctx.reshape(bsz * n_ctx, d)
    for l in range(depth):
        need_ctx = l < depth - 1
        proj_lat, u_lat = _inproj(x2, mods, lat_row(tm_proj), w_in_k, l, tm_proj, tn_proj)
        proj_ctx, u_ctx = _inproj(c2, mods, ctx_mod_row, w_in_k, l, tm_proj, tn_proj,
                                  first_block=0 if need_ctx else _CTX_KV_BLOCK)

        ys5_ctx, ys5_lat = _s5(u_ctx, u_lat, kin, vin, win, a_t, l, bsz)
        yret_lat, yret_ctx = _retention(proj_lat, proj_ctx, log_gamma, cos, sin, pmat, l, bsz, seq, n_ctx, need_ctx)
        yna_lat = _na(proj_lat, proj_ctx, na_table, l, bsz, seq, n_ctx)

        x2 = _merge(x2, proj_lat, ys5_lat, yret_lat, yna_lat, mods, lat_row(tm), l, merge_w, tm)
        x2 = _ffn(x2, mods, lat_row(tm), l, *ffn_w, tm, final=not need_ctx)
        if need_ctx:
            yna_ctx = _ctx_attn(proj_ctx, bsz, n_ctx)
            c2 = _merge(c2, proj_ctx, ys5_ctx, yret_ctx, yna_ctx, mods, ctx_mod_row, l, merge_w, tm)
            c2 = _ffn(c2, mods, ctx_mod_row, l, *ffn_w, tm, final=False)
    return x2.reshape(bsz, seq, d)
```

```python
import functools
import math

import numpy as np
import jax
import jax.numpy as jnp
from jax import lax
from jax.experimental import pallas as pl
from jax.experimental.pallas import tpu as pltpu

F32 = jnp.float32
BF16 = jnp.bfloat16
HIGHEST = lax.Precision.HIGHEST

D_MODEL = 1024
GRID_W = 64
S5_WIDTH = 512
S5_GROUP = 16
S5_GROUPS = S5_WIDTH // S5_GROUP
S5_STATE = 64
S5_CHUNK = 16
RET_HEADS = 4
RET_DK = 64
RET_DV = 128
RET_QK_WIDTH = RET_HEADS * RET_DK
RET_WIDTH = RET_HEADS * RET_DV
RET_T = 256
NA_HEADS = 8
NA_HEAD_DIM = 64
NA_WIDTH = NA_HEADS * NA_HEAD_DIM
NA_ROWS = 8
NA_COLS = 16
NA_QROWS = 4
NA_KROWS = 12
NA_NLAG = 2 * NA_ROWS
N_BRANCH = 3
ROPE_BASE = 10000.0
RMS_EPS = 1e-6
GN_EPS = 1e-5
NEG_INF = -1e30
LANES = 128
SUBLANES = 8
MOD_ROWS = 16

_REF_SPLIT = (S5_WIDTH, RET_QK_WIDTH, RET_WIDTH, NA_WIDTH, NA_WIDTH,
              RET_QK_WIDTH, RET_WIDTH, NA_WIDTH, N_BRANCH * D_MODEL)
_REF_NAMES = ('u', 'rk', 'rv', 'nk', 'nv', 'rq', 'rg', 'nq', 'gates')
_REF_OFF = dict(zip(_REF_NAMES, np.concatenate([[0], np.cumsum(_REF_SPLIT)[:-1]]).tolist()))
_REF_W = dict(zip(_REF_NAMES, _REF_SPLIT))
_MY_ORDER = ('gates', 'rq', 'rk', 'u', 'rv', 'nk', 'nv', 'rg', 'nq')
_CTX_KV_BLOCK = 1
_MY_OFF = {}
_o = 0
for _n in _MY_ORDER:
    _MY_OFF[_n] = _o
    _o += _REF_W[_n]
N_IN = _o
LOG2E = math.log2(math.e)
_COL_SCALE = {'nq': NA_HEAD_DIM ** -0.5 * LOG2E, 'rk': RET_DK ** -0.5}

VMEM_LIMIT = 56 * 1024 * 1024


def _cparams(sem):
    return pltpu.CompilerParams(dimension_semantics=sem, vmem_limit_bytes=VMEM_LIMIT)


def _sigmoid(x):
    return 1.0 / (1.0 + jnp.exp(-x))


def _rms(x):
    return x * lax.rsqrt(jnp.mean(x * x, axis=-1, keepdims=True) + RMS_EPS)


def _layer_spec(arr, l):
    nd = arr.ndim
    return pl.BlockSpec((1,) + arr.shape[1:], lambda *_: (l,) + (0,) * (nd - 1))


def _ada_kernel(c_ref, w_ref, b_ref, o_ref):
    c = c_ref[...]
    s = c * _sigmoid(c)
    o_ref[0] = jnp.dot(s, w_ref[0], preferred_element_type=F32, precision=HIGHEST) + b_ref[0]


def _ada(cvec, w_ada, b_ada):
    depth, d, n = w_ada.shape
    tn = 1536
    rows = cvec.shape[0]
    return pl.pallas_call(
        _ada_kernel,
        grid=(depth, n // tn),
        in_specs=[pl.BlockSpec((rows, d), lambda l, j: (0, 0)),
                  pl.BlockSpec((1, d, tn), lambda l, j: (l, 0, j)),
                  pl.BlockSpec((1, 1, tn), lambda l, j: (l, 0, j))],
        out_specs=pl.BlockSpec((1, rows, tn), lambda l, j: (l, 0, j)),
        out_shape=jax.ShapeDtypeStruct((depth, rows, n), F32),
        compiler_params=_cparams(("parallel", "parallel")),
        name="ada_mod",
    )(cvec, w_ada, b_ada.reshape(depth, 1, n))


_GPL = LANES // S5_GROUP
_NSLAB = S5_WIDTH // LANES


def _block_transpose(tiles):
    blk = lax.broadcasted_iota(jnp.int32, tiles[0].shape, 1) // S5_GROUP
    tiles = list(tiles)
    s = _GPL // 2
    while s >= 1:
        hi = (blk & s) != 0
        for a in range(_GPL):
            if a & s:
                continue
            b = a + s
            ta, tb = tiles[a], tiles[b]
            tiles[a] = jnp.where(hi, pltpu.roll(tb, s * S5_GROUP, 1), ta)
            tiles[b] = jnp.where(hi, tb, pltpu.roll(ta, LANES - s * S5_GROUP, 1))
        s //= 2
    return tiles


def _inproj_kernel(x_ref, sh_ref, sc_ref, w_ref, o_ref, ut_ref, h_ref, u_scr, *, u_off, first_block, single):
    jcol = pl.program_id(1)

    def normed():
        return (_rms(x_ref[...]) * (1.0 + sc_ref[0]) + sh_ref[0]).astype(BF16)

    if not single:
        @pl.when(jcol == first_block)
        def _():
            h_ref[...] = normed()

    def project():
        res = jnp.dot(normed() if single else h_ref[...], w_ref[0], preferred_element_type=F32)
        o_ref[...] = res.astype(o_ref.dtype)
        for j in range(_NSLAB):
            u_scr[j] = res[:, u_off + j * LANES:u_off + (j + 1) * LANES]
        nrow = u_scr.shape[1] // S5_CHUNK
        for j in range(_NSLAB):
            for half in range(2):
                v = [u_scr[j, pl.ds(half * _GPL + tl, nrow, stride=S5_CHUNK), :] for tl in range(_GPL)]
                for q, out in enumerate(_block_transpose(v)):
                    ut_ref[j * _GPL + q, :, half * LANES:(half + 1) * LANES] = out.astype(ut_ref.dtype)

    if first_block == 0:
        project()
    else:
        @pl.when(jcol < first_block)
        def _():
            o_ref[...] = jnp.zeros(o_ref.shape, o_ref.dtype)

        pl.when(jcol >= first_block)(project)


def _inproj(x2, mods, mod_row, w, l, tm, tn, first_block=0):
    m, d = x2.shape
    n = w.shape[2]
    base = l * MOD_ROWS * 6
    nj = n // tn
    u_off = _MY_OFF['u'] - (nj - 1) * tn
    assert 0 <= u_off and u_off + S5_WIDTH <= tn
    return pl.pallas_call(
        functools.partial(_inproj_kernel, u_off=u_off, first_block=first_block, single=nj == 1),
        grid=(m // tm, nj),
        in_specs=[pl.BlockSpec((tm, d), lambda i, j: (i, 0)),
                  pl.BlockSpec((1, 1, d), lambda i, j: (base + mod_row(i) * 6 + 0, 0, 0)),
                  pl.BlockSpec((1, 1, d), lambda i, j: (base + mod_row(i) * 6 + 1, 0, 0)),
                  pl.BlockSpec((1, d, tn), lambda i, j: (l, 0, jnp.maximum(j, first_block)))],
        out_specs=[pl.BlockSpec((tm, tn), lambda i, j: (i, j)),
                   pl.BlockSpec((S5_GROUPS, tm // S5_CHUNK, S5_CHUNK * S5_GROUP), lambda i, j: (0, i, 0))],
        out_shape=[jax.ShapeDtypeStruct((m, n), BF16),
                   jax.ShapeDtypeStruct((S5_GROUPS, m // S5_CHUNK, S5_CHUNK * S5_GROUP), BF16)],
        scratch_shapes=[pltpu.VMEM((tm, d), BF16), pltpu.VMEM((_NSLAB, tm, LANES), F32)],
        compiler_params=_cparams(("parallel", "arbitrary")),
        name="in_proj",
    )(x2, mods, mods, w)


def _s5w_kernel(lam_ref, btr_ref, bti_ref, ctr_ref, cti_ref, dd_ref, kin_ref, vin_ref, win_ref, at_ref, *, ng):
    t_n, h_n = S5_CHUNK, S5_GROUP
    width = t_n * h_n
    lane = lax.broadcasted_iota(jnp.int32, (1, LANES), 1)
    f_lane = lane < S5_STATE
    tau = lax.broadcasted_iota(jnp.int32, (3 * SUBLANES, LANES), 0).astype(F32)
    lane_w = lax.broadcasted_iota(jnp.int32, (h_n, width), 1)
    for g in range(ng):
        lam_re, lam_im, dt = lam_ref[g, 0:1, :], lam_ref[g, 1:2, :], lam_ref[g, 2:3, :]
        mag = jnp.exp(tau * (lam_re * dt))
        ang = tau * (lam_im * dt)
        pr, pi = mag * jnp.cos(ang), mag * jnp.sin(ang)
        ab_re, ab_im = pr[1:2], pi[1:2]
        den = lam_re * lam_re + lam_im * lam_im
        f_re = ((ab_re - 1.0) * lam_re + ab_im * lam_im) / den
        f_im = (ab_im * lam_re - (ab_re - 1.0) * lam_im) / den
        btr, bti = btr_ref[g], bti_ref[g]
        bbr = f_re * btr - f_im * bti
        bbi = f_re * bti + f_im * btr
        ctr, cti = ctr_ref[g], cti_ref[g]

        def powers(pf, pb):
            rr = [jnp.broadcast_to(jnp.where(f_lane, pr[pf[t]:pf[t] + 1], pr[pb[t]:pb[t] + 1]), (h_n, LANES))
                  for t in range(t_n)]
            ri = [jnp.broadcast_to(jnp.where(f_lane, pi[pf[t]:pf[t] + 1], pi[pb[t]:pb[t] + 1]), (h_n, LANES))
                  for t in range(t_n)]
            return jnp.concatenate(rr, axis=0), jnp.concatenate(ri, axis=0)

        tile = lambda a: jnp.concatenate([a] * t_n, axis=0)
        bbr_t, bbi_t, ctr_t, cti_t = tile(bbr), tile(bbi), tile(ctr), tile(cti)

        xr, xi = powers([t_n - 1 - t for t in range(t_n)], list(range(t_n)))
        vin = jnp.concatenate([xr * bbr_t - xi * bbi_t, xr * bbi_t + xi * bbr_t], axis=1)
        vin_ref[g] = vin.astype(vin_ref.dtype)

        yr, yi = powers([t + 1 for t in range(t_n)], [t_n - t for t in range(t_n)])
        win_t = jnp.concatenate([ctr_t * yr - cti_t * yi, -(ctr_t * yi + cti_t * yr)], axis=1)
        win_ref[g] = win_t.T.astype(win_ref.dtype)

        zr, zi = powers(list(range(t_n)), [t_n - 1 - t for t in range(t_n)])
        fmat = jnp.concatenate([ctr_t * zr - cti_t * zi, ctr_t * zi + cti_t * zr], axis=1)
        dn = (((1,), (1,)), ((), ()))
        lhs_f = jnp.concatenate([jnp.where(f_lane, bbr, 0.0), jnp.where(f_lane, -bbi, 0.0)], axis=1)
        lhs_b = jnp.concatenate([jnp.where(f_lane, 0.0, bbr), jnp.where(f_lane, 0.0, -bbi)], axis=1)
        w_f = lax.dot_general(lhs_f, fmat, dn, preferred_element_type=F32, precision=HIGHEST)
        w_b = lax.dot_general(lhs_b, fmat, dn, preferred_element_type=F32, precision=HIGHEST)
        w_f = w_f + jnp.concatenate([dd_ref[g], jnp.zeros((h_n, width - LANES), F32)], axis=1)
        blocks = []
        for t in range(t_n):
            sh_f = t * h_n
            sh_b = (width - (t_n - 1 - t) * h_n) % width
            fw = w_f if sh_f == 0 else pltpu.roll(w_f, sh_f, 1)
            bw = w_b if sh_b == 0 else pltpu.roll(w_b, sh_b, 1)
            blocks.append(jnp.where(lane_w >= t * h_n, fw, 0.0) + jnp.where(lane_w < (t + 1) * h_n, bw, 0.0))
        kin_ref[g] = jnp.concatenate(blocks, axis=0).astype(kin_ref.dtype)
        at_ref[g] = jnp.broadcast_to(jnp.concatenate([pr[t_n:t_n + 1], pi[t_n:t_n + 1]], axis=1),
                                     (SUBLANES, 2 * LANES))


def _s5_weights(lam_re, lam_im, log_dt, b_re, b_im, c_re, c_im, d_skip, ng=4):
    depth, _, g_n, p_n = lam_re.shape
    h_n = S5_GROUP
    n = depth * g_n
    width = S5_CHUNK * h_n
    pair = lambda a: a.astype(F32).transpose(0, 2, 1, 3).reshape(n, 1, 2 * p_n)
    dt = jnp.broadcast_to(jnp.exp(log_dt.astype(F32))[..., None], lam_re.shape)
    lam = jnp.concatenate([pair(lam_re), pair(lam_im), pair(dt), jnp.zeros((n, SUBLANES - 3, 2 * p_n), F32)], axis=1)
    bt = lambda a: a.astype(F32).transpose(0, 2, 4, 1, 3).reshape(n, h_n, 2 * p_n)
    ct = lambda a: a.astype(F32).transpose(0, 2, 3, 1, 4).reshape(n, h_n, 2 * p_n)
    dd = jnp.eye(h_n, LANES, dtype=F32)[None] * d_skip.astype(F32).reshape(n, h_n, 1)
    small = pl.BlockSpec((ng, h_n, LANES), lambda i: (i, 0, 0))
    big = pl.BlockSpec((ng, width, width), lambda i: (i, 0, 0))
    return pl.pallas_call(
        functools.partial(_s5w_kernel, ng=ng),
        grid=(n // ng,),
        in_specs=[pl.BlockSpec((ng, SUBLANES, LANES), lambda i: (i, 0, 0)), small, small, small, small, small],
        out_specs=[big, big, big, pl.BlockSpec((ng, SUBLANES, width), lambda i: (i, 0, 0))],
        out_shape=[jax.ShapeDtypeStruct((n, width, width), BF16)] * 3
                  + [jax.ShapeDtypeStruct((n, SUBLANES, width), F32)],
        compiler_params=_cparams(("parallel",)),
        name="s5_weights",
    )(lam, bt(b_re), bt(b_im), ct(c_re), ct(c_im), dd)


def _s5_pitch(n):
    p = -(-n // SUBLANES)
    return (p | 1) * SUBLANES


def _s5_kernel(uc_ref, ul_ref, kin_ref, vin_ref, win_ref, a_ref, yc_ref, yl_ref,
               sc_scr, sl_scr, xac_scr, xal_scr, xbc_scr, xbl_scr, *, ng, ncc, ncl, bsz):
    half = 2 * S5_STATE
    pc, plat = _s5_pitch(ncc), _s5_pitch(ncl)
    segs = ((uc_ref, yc_ref, sc_scr, xac_scr, xbc_scr, ncc, pc), (ul_ref, yl_ref, sl_scr, xal_scr, xbl_scr, ncl, plat))
    for u_ref, _, s_scr, _, _, n, pitch in segs:
        for g in range(ng):
            s = jnp.dot(u_ref[g], vin_ref[g], preferred_element_type=F32)
            for b in range(bsz):
                for k in range(2):
                    s_scr[g, k, pl.ds(b * pitch, n), :] = s[b * n:(b + 1) * n, k * half:(k + 1) * half]
    lane = lax.broadcasted_iota(jnp.int32, (bsz, half), 1)
    fwd_lane = lane < S5_STATE
    a_re = [a_ref[g, :, :half] for g in range(ng)]
    a_im = [a_ref[g, :, half:] for g in range(ng)]

    def make_step(s_scr, xa_scr, xb_scr, pitch):
        def step(fc, bc, xs):
            rf = pl.ds(fc, bsz, stride=pitch)
            rb = pl.ds(bc, bsz, stride=pitch)
            out = []
            for g in range(ng):
                xr, xi = xs[2 * g], xs[2 * g + 1]
                xa_scr[g, 0, rf, :] = xr
                xa_scr[g, 1, rf, :] = xi
                xb_scr[g, 0, rb, :] = xr
                xb_scr[g, 1, rb, :] = xi
                sr = jnp.where(fwd_lane, s_scr[g, 0, rf, :], s_scr[g, 0, rb, :])
                si = jnp.where(fwd_lane, s_scr[g, 1, rf, :], s_scr[g, 1, rb, :])
                out.append(a_re[g] * xr - a_im[g] * xi + sr)
                out.append(a_re[g] * xi + a_im[g] * xr + si)
            return tuple(out)
        return step

    step_c = make_step(sc_scr, xac_scr, xbc_scr, pc)
    step_l = make_step(sl_scr, xal_scr, xbl_scr, plat)
    xs = tuple(jnp.zeros((bsz, half), F32) for _ in range(2 * ng))
    xs = lax.fori_loop(0, ncc, lambda i, c: step_c(i, ncc - 1 - i, c), xs, unroll=4)
    xs = lax.fori_loop(0, ncl, lambda i, c: step_l(i, ncl - 1 - i, c), xs, unroll=4)
    for u_ref, y_ref, _, xa_scr, xb_scr, n, pitch in segs:
        fwd_r = lax.broadcasted_iota(jnp.int32, (n, half), 1) < S5_STATE
        for g in range(ng):
            rows = []
            for b in range(bsz):
                r = pl.ds(b * pitch, n)
                rows.append(jnp.concatenate([jnp.where(fwd_r, xa_scr[g, k, r, :], xb_scr[g, k, r, :]) for k in range(2)],
                                            axis=1))
            x_in = jnp.concatenate(rows, axis=0).astype(BF16)
            y = (jnp.dot(u_ref[g], kin_ref[g], preferred_element_type=F32)
                 + jnp.dot(x_in, win_ref[g], preferred_element_type=F32))
            y_ref[g] = y.astype(y_ref.dtype)


def _s5(u_ctx, u_lat, kin, vin, win, a_t, l, bsz, ng=4):
    g_n, rc, w = u_ctx.shape
    rl = u_lat.shape[1]
    ncc, ncl = rc // bsz, rl // bsz
    nblk = g_n // ng
    wspec = pl.BlockSpec((ng, w, w), lambda i: (l * nblk + i, 0, 0))
    uspec = lambda r: pl.BlockSpec((ng, r, w), lambda i: (i, 0, 0))
    scr = lambda n: pltpu.VMEM((ng, 2, bsz * _s5_pitch(n), LANES), F32)
    return pl.pallas_call(
        functools.partial(_s5_kernel, ng=ng, ncc=ncc, ncl=ncl, bsz=bsz),
        grid=(nblk,),
        in_specs=[uspec(rc), uspec(rl), wspec, wspec, wspec,
                  pl.BlockSpec((ng, SUBLANES, w), lambda i: (l * nblk + i, 0, 0))],
        out_specs=[uspec(rc), uspec(rl)],
        out_shape=[jax.ShapeDtypeStruct(u_ctx.shape, BF16), jax.ShapeDtypeStruct(u_lat.shape, BF16)],
        scratch_shapes=[scr(ncc), scr(ncl)] * 3,
        compiler_params=_cparams(("parallel",)),
        name="s5_mixer",
    )(u_ctx, u_lat, kin, vin, win, a_t)


def _rotary_tables(seq):
    quarter = RET_DK // 4
    pos = jnp.arange(seq)
    inv_freq = ROPE_BASE ** (-jnp.arange(quarter, dtype=F32) / quarter)
    ang_r = (pos // GRID_W).astype(F32)[:, None] * inv_freq[None, :]
    ang_c = (pos % GRID_W).astype(F32)[:, None] * inv_freq[None, :]
    cos = jnp.concatenate([jnp.cos(ang_r)] * 2 + [jnp.cos(ang_c)] * 2, axis=-1)
    sin = jnp.concatenate([jnp.sin(ang_r)] * 2 + [jnp.sin(ang_c)] * 2, axis=-1)
    cos = jnp.tile(cos, (1, RET_HEADS))
    sin = jnp.tile(sin, (1, RET_HEADS))
    p = np.zeros((RET_QK_WIDTH, RET_QK_WIDTH), np.float32)
    for d in range(RET_QK_WIDTH):
        if d % (2 * quarter) < quarter:
            p[d + quarter, d] = -1.0
        else:
            p[d - quarter, d] = 1.0
    return cos, sin, jnp.asarray(p, BF16)


def _ret_kernel(lg_ref, q_ref, k_ref, v_ref, g_ref, cq_ref, ck_ref, cv_ref, cg_ref,
                cos_ref, sin_ref, p_ref, o_ref, co_ref, krot_scr, sin_scr, dm_scr, *, layer, need_ctx, ncl):
    t = RET_T
    qkw, vw = RET_QK_WIDTH, RET_WIDTH

    def per_head(shape, axis, width, d):
        head = lax.broadcasted_iota(jnp.int32, shape, axis) // width
        out = jnp.zeros(shape, F32)
        for h in range(RET_HEADS):
            out = jnp.where(head == h, lg_ref[layer, d * RET_HEADS + h], out)
        return out

    row = lax.broadcasted_iota(jnp.int32, (t, qkw), 0).astype(F32)
    lgf = per_head((t, qkw), 1, RET_DK, 0)
    lgb = per_head((t, qkw), 1, RET_DK, 1)
    qdec_f = jnp.exp((row + 1.0) * lgf)
    qdec_b = jnp.exp((t - row) * lgb)
    kdec_f = jnp.exp((t - 1.0 - row) * lgf)
    kdec_b = jnp.exp(row * lgb)
    cdec_f = jnp.exp(float(t) * per_head((qkw, vw), 0, RET_DK, 0))
    cdec_b = jnp.exp(float(t) * per_head((qkw, vw), 0, RET_DK, 1))
    blk = (lax.broadcasted_iota(jnp.int32, (qkw, vw), 0) // RET_DK
           == lax.broadcasted_iota(jnp.int32, (qkw, vw), 1) // RET_DV)
    head_lane = lax.broadcasted_iota(jnp.int32, (t, qkw), 1) // RET_DK

    ii = lax.broadcasted_iota(jnp.int32, (t, t), 0)
    jj = lax.broadcasted_iota(jnp.int32, (t, t), 1)
    dif = (ii - jj).astype(F32)
    for h in range(RET_HEADS):
        df = jnp.where(dif >= 0, jnp.exp(jnp.where(dif >= 0, dif, 0.0) * lg_ref[layer, h]), 0.0)
        db = jnp.where(dif < 0, jnp.exp(jnp.where(dif < 0, -dif, 0.0) * lg_ref[layer, RET_HEADS + h]), 0.0)
        dm_scr[h] = df + db

    def rotary(x_bf, c):
        rows = pl.ds(c * t, t)
        swapped = jnp.dot(x_bf, p_ref[...], preferred_element_type=F32)
        return x_bf.astype(F32) * cos_ref[rows, :] + swapped * sin_ref[rows, :]

    krot_scr[pl.ds(0, t), :] = ck_ref[...]
    for c in range(ncl):
        krot_scr[pl.ds((c + 1) * t, t), :] = rotary(k_ref[pl.ds(c * t, t), :], c).astype(BF16)

    def v_chunk(c):
        return cv_ref[...] if c == 0 else v_ref[pl.ds((c - 1) * t, t), :]

    def kv(c, kdec):
        kd = (krot_scr[pl.ds(c * t, t), :].astype(F32) * kdec).astype(BF16)
        return lax.dot_general(kd, v_chunk(c), (((0,), (0,)), ((), ())), preferred_element_type=F32)

    s = jnp.zeros((qkw, vw), F32)
    for c in range(ncl + 1):
        sin_scr[c, pl.ds(0, qkw), :] = jnp.where(blk, s, 0.0).astype(BF16)
        if c < ncl:
            s = cdec_f * s + kv(c, kdec_f)
    sin_scr[0, pl.ds(qkw, qkw), :] = jnp.zeros((qkw, vw), BF16)
    s = kv(0, kdec_b)
    for c in range(ncl, 0, -1):
        sin_scr[c, pl.ds(qkw, qkw), :] = jnp.where(blk, s, 0.0).astype(BF16)
        if c > 1:
            s = cdec_b * s + kv(c, kdec_b)

    for c in range(0 if need_ctx else 1, ncl + 1):
        if c == 0:
            q = cq_ref[...].astype(F32)
            gate = cg_ref[...].astype(F32)
        else:
            q = rotary(q_ref[pl.ds((c - 1) * t, t), :], c - 1)
            gate = g_ref[pl.ds((c - 1) * t, t), :].astype(F32)
        q_bf = q.astype(BF16)
        cross = (jnp.dot((q * qdec_f).astype(BF16), sin_scr[c, pl.ds(0, qkw), :], preferred_element_type=F32)
                 + jnp.dot((q * qdec_b).astype(BF16), sin_scr[c, pl.ds(qkw, qkw), :], preferred_element_type=F32))
        k_c = krot_scr[pl.ds(c * t, t), :]
        v_c = v_chunk(c)
        outs = []
        q_heads = jnp.concatenate([jnp.where(head_lane == h, q_bf, jnp.zeros_like(q_bf)) for h in range(RET_HEADS)],
                                  axis=0)
        sc_heads = lax.dot_general(q_heads, k_c, (((1,), (1,)), ((), ())), preferred_element_type=F32)
        for h in range(RET_HEADS):
            sc = (sc_heads[h * t:(h + 1) * t] * dm_scr[h]).astype(BF16)
            o = (jnp.dot(sc, v_c[:, h * RET_DV:(h + 1) * RET_DV], preferred_element_type=F32)
                 + cross[:, h * RET_DV:(h + 1) * RET_DV])
            mu = jnp.mean(o, axis=-1, keepdims=True)
            var = jnp.mean(jnp.square(o - mu), axis=-1, keepdims=True)
            outs.append((o - mu) * lax.rsqrt(var + GN_EPS))
        y = gate * _sigmoid(gate) * jnp.concatenate(outs, axis=1)
        if c == 0:
            co_ref[...] = y.astype(co_ref.dtype)
        else:
            o_ref[pl.ds((c - 1) * t, t), :] = y.astype(o_ref.dtype)
    if not need_ctx:
        co_ref[...] = jnp.zeros(co_ref.shape, co_ref.dtype)


def _retention(proj_lat, proj_ctx, lg, cos, sin, pmat, l, bsz, seq, n_ctx, need_ctx):
    t = RET_T
    ncl = seq // t
    assert n_ctx == t
    qb, kb = _MY_OFF['rq'] // RET_QK_WIDTH, _MY_OFF['rk'] // RET_QK_WIDTH
    vb, gb = _MY_OFF['rv'] // RET_WIDTH, _MY_OFF['rg'] // RET_WIDTH

    def col(n, w, j):
        return pl.BlockSpec((n, w), lambda b, j=j: (b, j))

    const = lambda shape: pl.BlockSpec(shape, lambda b: (0,) * len(shape))
    return pl.pallas_call(
        functools.partial(_ret_kernel, layer=l, need_ctx=need_ctx, ncl=ncl),
        grid=(bsz,),
        in_specs=[pl.BlockSpec(memory_space=pltpu.SMEM),
                  col(seq, RET_QK_WIDTH, qb), col(seq, RET_QK_WIDTH, kb), col(seq, RET_WIDTH, vb), col(seq, RET_WIDTH, gb),
                  col(n_ctx, RET_QK_WIDTH, qb), col(n_ctx, RET_QK_WIDTH, kb), col(n_ctx, RET_WIDTH, vb), col(n_ctx, RET_WIDTH, gb),
                  const((seq, RET_QK_WIDTH)), const((seq, RET_QK_WIDTH)), const((RET_QK_WIDTH, RET_QK_WIDTH))],
        out_specs=[pl.BlockSpec((seq, RET_WIDTH), lambda b: (b, 0)),
                   pl.BlockSpec((n_ctx, RET_WIDTH), lambda b: (b, 0))],
        out_shape=[jax.ShapeDtypeStruct((bsz * seq, RET_WIDTH), BF16),
                   jax.ShapeDtypeStruct((bsz * n_ctx, RET_WIDTH), BF16)],
        scratch_shapes=[pltpu.VMEM((seq + n_ctx, RET_QK_WIDTH), BF16),
                        pltpu.VMEM((ncl + 1, 2 * RET_QK_WIDTH, RET_WIDTH), BF16),
                        pltpu.VMEM((RET_HEADS, t, t), F32)],
        compiler_params=_cparams(("parallel",)),
        name="retention",
    )(lg, proj_lat, proj_lat, proj_lat, proj_lat, proj_ctx, proj_ctx, proj_ctx, proj_ctx, cos, sin, pmat)


def _na_block_start(kblk, rows):
    return jnp.clip(kblk * NA_QROWS - NA_ROWS // 2, 0, rows - NA_KROWS)


def _na_bias_table(rpb):
    depth, heads, nlag, ncol = rpb.shape
    cols = np.arange(GRID_W)
    cs = np.clip(cols - NA_COLS // 2, 0, GRID_W - NA_COLS)
    valid_c = (cols[None, :] >= cs[:, None]) & (cols[None, :] < cs[:, None] + NA_COLS)
    pad = GRID_W - 1
    padded = jnp.pad(rpb.astype(F32) * LOG2E, ((0, 0), (0, 0), (0, 0), (pad, pad)))
    toe = jnp.stack([padded[..., pad + NA_COLS - 1 - qc: pad + NA_COLS - 1 - qc + GRID_W] for qc in range(GRID_W)],
                    axis=-2)
    toe = jnp.where(jnp.asarray(valid_c), toe, NEG_INF)
    toe = jnp.pad(toe, ((0, 0), (0, 0), (1, NA_NLAG + 1 - nlag - 1), (0, 0), (0, 0)))
    table = jnp.concatenate([toe[:, :, :NA_NLAG], toe[:, :, 1:NA_NLAG + 1]], axis=-1)
    return table.reshape(depth, heads * NA_NLAG, GRID_W, 2 * GRID_W)


def _attend(q_pair, k_list, v_list, bias_list):
    nq = q_pair.shape[0]
    lane = lax.broadcasted_iota(jnp.int32, q_pair.shape, 1) // NA_HEAD_DIM
    zero = jnp.zeros_like(q_pair)
    q2 = jnp.concatenate([jnp.where(lane == 0, q_pair, zero), jnp.where(lane == 1, q_pair, zero)], axis=0)
    ss = []
    for k_i, b_i in zip(k_list, bias_list):
        s = lax.dot_general(q2, k_i, (((1,), (1,)), ((), ())), preferred_element_type=F32)
        if b_i is not None:
            s = s + jnp.concatenate([b_i(0), b_i(1)], axis=0)
        ss.append(s)
    m = ss[0].max(axis=-1, keepdims=True)
    for s in ss[1:]:
        m = jnp.maximum(m, s.max(axis=-1, keepdims=True))
    ps = [jnp.exp2(s - m) for s in ss]
    den = ps[0].sum(axis=-1, keepdims=True)
    for p in ps[1:]:
        den = den + p.sum(axis=-1, keepdims=True)
    acc = jnp.dot(ps[0].astype(BF16), v_list[0], preferred_element_type=F32)
    for p, v_i in zip(ps[1:], v_list[1:]):
        acc = acc + jnp.dot(p.astype(BF16), v_i, preferred_element_type=F32)
    out = acc / den
    return jnp.where(lane == 0, out[:nq], out[nq:])


def _attend_window(q_pair, k_win, k_ctx, v_win, v_ctx, tab_ref, hp, rel, starts):
    nq = q_pair.shape[0]
    npair = k_win.shape[0] // LANES
    lane = lax.broadcasted_iota(jnp.int32, q_pair.shape, 1) // NA_HEAD_DIM
    zero = jnp.zeros_like(q_pair)
    dn = (((1,), (1,)), ((), ()))
    q2 = jnp.concatenate([jnp.where(lane == 0, q_pair, zero), jnp.where(lane == 1, q_pair, zero)], axis=0)
    s_win = lax.dot_general(q2, k_win, dn, preferred_element_type=F32)
    s_ctx = lax.dot_general(q2, k_ctx, dn, preferred_element_type=F32)
    left = lax.broadcasted_iota(jnp.int32, (GRID_W, LANES), 1) < GRID_W
    neg_left = jnp.where(left, NEG_INF, 0.0)
    neg_right = jnp.where(left, 0.0, NEG_INF)
    p_win, p_ctx, inv = [], [], []
    for hh in range(2):
        h = 2 * hp + hh
        for qrl in range(NA_QROWS):
            r = slice(hh * nq + qrl * GRID_W, hh * nq + (qrl + 1) * GRID_W)
            st = starts[qrl]
            lo, hi = st // 2, (st + NA_ROWS + 1) // 2
            tiles = []
            for kp in range(lo, hi):
                lag = rel + 2 * kp - qrl + NA_ROWS - 1
                assert -1 <= lag <= NA_NLAG - 2
                t = tab_ref[0, h * NA_NLAG + lag + 1]
                if 2 * kp < st:
                    t = t + neg_left
                if 2 * kp + 1 >= st + NA_ROWS:
                    t = t + neg_right
                tiles.append(t)
            sw = s_win[r, lo * LANES:hi * LANES] + jnp.concatenate(tiles, axis=1)
            sc = s_ctx[r]
            m = jnp.maximum(sw.max(axis=-1, keepdims=True), sc.max(axis=-1, keepdims=True))
            pw, pc = jnp.exp2(sw - m), jnp.exp2(sc - m)
            inv.append(1.0 / (pw.sum(axis=-1, keepdims=True) + pc.sum(axis=-1, keepdims=True)))
            parts = [pw.astype(BF16)]
            if lo > 0:
                parts.insert(0, jnp.zeros((GRID_W, lo * LANES), BF16))
            if hi < npair:
                parts.append(jnp.zeros((GRID_W, (npair - hi) * LANES), BF16))
            p_win.append(jnp.concatenate(parts, axis=1) if len(parts) > 1 else parts[0])
            p_ctx.append(pc.astype(BF16))
    acc = (jnp.dot(jnp.concatenate(p_win, axis=0), v_win, preferred_element_type=F32)
           + jnp.dot(jnp.concatenate(p_ctx, axis=0), v_ctx, preferred_element_type=F32))
    out = jnp.concatenate([acc[i * GRID_W:(i + 1) * GRID_W] * inv[i] for i in range(len(inv))], axis=0)
    return jnp.where(lane == 0, out[:nq], out[nq:])


def _na_kernel(q_ref, k_ref, v_ref, ck_ref, cv_ref, tab_ref, o_ref, *, rows):
    nk = NA_KROWS * GRID_W
    nblk = rows // NA_QROWS
    kblk = pl.program_id(1)
    ks = pl.multiple_of(_na_block_start(kblk, rows) * GRID_W, NA_QROWS * GRID_W)

    def run(kb):
        r0 = kb * NA_QROWS
        ks_row = min(max(r0 - NA_ROWS // 2, 0), rows - NA_KROWS)
        starts = [min(max(r0 + qrl - NA_ROWS // 2, 0), rows - NA_ROWS) - ks_row for qrl in range(NA_QROWS)]
        for hp in range(NA_HEADS // 2):
            ln = pl.ds(hp * LANES, LANES)
            y = _attend_window(q_ref[:, ln], k_ref[pl.ds(ks, nk), ln], ck_ref[:, ln],
                               v_ref[pl.ds(ks, nk), ln], cv_ref[:, ln], tab_ref, hp, ks_row - r0, starts)
            o_ref[:, ln] = y.astype(o_ref.dtype)

    assert nblk >= 3 and all(_na_geometry(kb, rows) == _na_geometry(1, rows) for kb in range(1, nblk - 1))
    pl.when(kblk == 0)(lambda: run(0))
    pl.when((kblk > 0) & (kblk < nblk - 1))(lambda: run(1))
    pl.when(kblk == nblk - 1)(lambda: run(nblk - 1))


def _na_geometry(kb, rows):
    r0 = kb * NA_QROWS
    ks_row = min(max(r0 - NA_ROWS // 2, 0), rows - NA_KROWS)
    return (ks_row - r0,) + tuple(min(max(r0 + qrl - NA_ROWS // 2, 0), rows - NA_ROWS) - ks_row
                                  for qrl in range(NA_QROWS))


def _na(proj_lat, proj_ctx, table, l, bsz, seq, n_ctx):
    rows = seq // GRID_W
    nq = NA_QROWS * GRID_W
    nblk = seq // nq
    qb, kb, vb = (_MY_OFF[n] // NA_WIDTH for n in ('nq', 'nk', 'nv'))
    return pl.pallas_call(
        functools.partial(_na_kernel, rows=rows),
        grid=(bsz, nblk),
        in_specs=[pl.BlockSpec((nq, NA_WIDTH), lambda b, k: (b * nblk + k, qb)),
                  pl.BlockSpec((seq, NA_WIDTH), lambda b, k: (b, kb)),
                  pl.BlockSpec((seq, NA_WIDTH), lambda b, k: (b, vb)),
                  pl.BlockSpec((n_ctx, NA_WIDTH), lambda b, k: (b, kb)),
                  pl.BlockSpec((n_ctx, NA_WIDTH), lambda b, k: (b, vb)),
                  pl.BlockSpec((1,) + table.shape[1:], lambda b, k: (l, 0, 0, 0))],
        out_specs=pl.BlockSpec((nq, NA_WIDTH), lambda b, k: (b * nblk + k, 0)),
        out_shape=jax.ShapeDtypeStruct((bsz * seq, NA_WIDTH), BF16),
        compiler_params=_cparams(("parallel", "arbitrary")),
        name="neighborhood_attention",
    )(proj_lat, proj_lat, proj_lat, proj_ctx, proj_ctx, table)


def _ctx_attn_kernel(q_ref, k_ref, v_ref, o_ref):
    for hp in range(NA_HEADS // 2):
        ln = pl.ds(hp * LANES, LANES)
        y = _attend(q_ref[:, ln], [k_ref[:, ln]], [v_ref[:, ln]], [None])
        o_ref[:, ln] = y.astype(o_ref.dtype)


def _ctx_attn(proj_ctx, bsz, n_ctx):
    qb, kb, vb = (_MY_OFF[n] // NA_WIDTH for n in ('nq', 'nk', 'nv'))
    spec = lambda j: pl.BlockSpec((n_ctx, NA_WIDTH), lambda b: (b, j))
    return pl.pallas_call(
        _ctx_attn_kernel,
        grid=(bsz,),
        in_specs=[spec(qb), spec(kb), spec(vb)],
        out_specs=pl.BlockSpec((n_ctx, NA_WIDTH), lambda b: (b, 0)),
        out_shape=jax.ShapeDtypeStruct((bsz * n_ctx, NA_WIDTH), BF16),
        compiler_params=_cparams(("parallel",)),
        name="context_attention",
    )(proj_ctx, proj_ctx, proj_ctx)


def _gelu_tanh(x):
    return 0.5 * x * (1.0 + jnp.tanh(math.sqrt(2.0 / math.pi) * (x + 0.044715 * (x * x * x))))


def _merge_kernel(x_ref, g0_ref, g1_ref, g2_ref, yt_ref, yret_ref, yna_ref, wglu_ref, bglu_ref,
                  wbs5_ref, wbret_ref, wbna_ref, wout_ref, gate_ref, o_ref, w_scr):
    nrow = yt_ref.shape[1]
    for j in range(_NSLAB):
        for half in range(2):
            o = [yt_ref[j * _GPL + q, :, half * LANES:(half + 1) * LANES].astype(F32) for q in range(_GPL)]
            for tl, out in enumerate(_block_transpose(o)):
                w_scr[j, pl.ds(half * _GPL + tl, nrow, stride=S5_CHUNK), :] = out
    ge = _gelu_tanh(jnp.concatenate([w_scr[j] for j in range(_NSLAB)], axis=1))
    z = jnp.dot(ge.astype(BF16), wglu_ref[0], preferred_element_type=F32) + bglu_ref[0]
    s5 = (ge * _sigmoid(z)).astype(BF16)
    m = (_sigmoid(g0_ref[...].astype(F32)) * jnp.dot(s5, wbs5_ref[0], preferred_element_type=F32)
         + _sigmoid(g1_ref[...].astype(F32)) * jnp.dot(yret_ref[...], wbret_ref[0], preferred_element_type=F32)
         + _sigmoid(g2_ref[...].astype(F32)) * jnp.dot(yna_ref[...], wbna_ref[0], preferred_element_type=F32))
    o_ref[...] = x_ref[...] + gate_ref[0] * jnp.dot(m.astype(BF16), wout_ref[0], preferred_element_type=F32)


def _merge(x2, proj, y_t, yret, yna, mods, mod_row, l, wts, tm):
    m, d = x2.shape
    base = l * MOD_ROWS * 6
    rowblk = lambda w, j=0: pl.BlockSpec((tm, w), lambda i, j=j: (i, j))
    return pl.pallas_call(
        _merge_kernel,
        grid=(m // tm,),
        in_specs=[rowblk(d), rowblk(d, 0), rowblk(d, 1), rowblk(d, 2),
                  pl.BlockSpec((S5_GROUPS, tm // S5_CHUNK, S5_CHUNK * S5_GROUP), lambda i: (0, i, 0)),
                  rowblk(RET_WIDTH), rowblk(NA_WIDTH)]
                 + [_layer_spec(w, l) for w in wts]
                 + [pl.BlockSpec((1, 1, d), lambda i: (base + mod_row(i) * 6 + 2, 0, 0))],
        out_specs=rowblk(d),
        out_shape=jax.ShapeDtypeStruct((m, d), F32),
        scratch_shapes=[pltpu.VMEM((_NSLAB, tm, LANES), F32)],
        compiler_params=_cparams(("parallel",)),
        name="merge_residual",
    )(x2, proj, proj, proj, y_t, yret, yna, *wts, mods)


def _ffn_kernel(x_ref, sh_ref, sc_ref, gate_ref, wg_ref, wu_ref, wd_ref, fn_ref, o_ref, *, final, th):
    x = x_ref[...]
    h = (_rms(x) * (1.0 + sc_ref[0]) + sh_ref[0]).astype(BF16)
    hidden = wg_ref.shape[2]
    acc = jnp.zeros(x.shape, F32)
    for j in range(hidden // th):
        a = jnp.dot(h, wg_ref[0, :, j * th:(j + 1) * th], preferred_element_type=F32)
        b = jnp.dot(h, wu_ref[0, :, j * th:(j + 1) * th], preferred_element_type=F32)
        act = (a * _sigmoid(a) * b).astype(BF16)
        acc = acc + jnp.dot(act, wd_ref[0, j * th:(j + 1) * th, :], preferred_element_type=F32)
    y = x + gate_ref[0] * acc
    if final:
        y = _rms(y) * fn_ref[...]
    o_ref[...] = y


def _ffn(x2, mods, mod_row, l, wg, wu, wd, fn, tm, final):
    m, d = x2.shape
    base = l * MOD_ROWS * 6
    modspec = lambda k: pl.BlockSpec((1, 1, d), lambda i, k=k: (base + mod_row(i) * 6 + k, 0, 0))
    return pl.pallas_call(
        functools.partial(_ffn_kernel, final=final, th=256),
        grid=(m // tm,),
        in_specs=[pl.BlockSpec((tm, d), lambda i: (i, 0)), modspec(3), modspec(4), modspec(5),
                  _layer_spec(wg, l), _layer_spec(wu, l), _layer_spec(wd, l),
                  pl.BlockSpec(fn.shape, lambda i: (0, 0))],
        out_specs=pl.BlockSpec((tm, d), lambda i: (i, 0)),
        out_shape=jax.ShapeDtypeStruct((m, d), F32),
        compiler_params=_cparams(("parallel",)),
        name="swiglu_residual",
    )(x2, mods, mods, mods, wg, wu, wd, fn)


def kernel(x, c, ctx, c_ctx, w_ada, b_ada, w_in, s5_lam_re, s5_lam_im, s5_log_dt, s5_b_re, s5_b_im,
           s5_c_re, s5_c_im, s5_d, s5_w_glu, s5_b_glu, ret_theta, na_rpb, w_branch_s5, w_branch_ret,
           w_branch_na, w_out, w_ffn_gate, w_ffn_up, w_ffn_down, final_norm):
    bsz, seq, d = x.shape
    n_ctx = ctx.shape[1]
    depth = w_ada.shape[0]
    ctx_row = bsz
    assert bsz == SUBLANES and bsz + 1 <= MOD_ROWS

    cvec = jnp.zeros((MOD_ROWS, d), F32).at[:bsz].set(c).at[ctx_row].set(c_ctx)
    mods = _ada(cvec, w_ada, b_ada).reshape(depth * MOD_ROWS * 6, 1, d)

    w_in_k = jnp.concatenate(
        [w_in[:, :, _REF_OFF[n]:_REF_OFF[n] + _REF_W[n]] * _COL_SCALE.get(n, 1.0) for n in _MY_ORDER],
        axis=2).astype(BF16)

    cos, sin, pmat = _rotary_tables(seq)
    log_gamma = jax.nn.log_sigmoid(ret_theta.astype(F32)).reshape(depth, 2 * RET_HEADS)
    kin, vin, win, a_t = _s5_weights(s5_lam_re, s5_lam_im, s5_log_dt, s5_b_re, s5_b_im, s5_c_re, s5_c_im, s5_d)
    na_table = _na_bias_table(na_rpb)
    merge_w = (s5_w_glu.astype(BF16), s5_b_glu.reshape(depth, 1, -1).astype(F32), w_branch_s5.astype(BF16),
               w_branch_ret.astype(BF16), w_branch_na.astype(BF16), w_out.astype(BF16))
    ffn_w = (w_ffn_gate.astype(BF16), w_ffn_up.astype(BF16), w_ffn_down.astype(BF16),
             final_norm.reshape(1, d).astype(F32))

    tm = 512
    tm_ctx, tn_ctx = 1024, N_IN // 2
    assert all(_MY_OFF[n] >= _CTX_KV_BLOCK * tn_ctx for n in ('u', 'rk', 'rv', 'nk', 'nv'))
    lat_row = lambda t: (lambda i: i // (seq // t))
    ctx_mod_row = lambda i: ctx_row

    x2 = x.reshape(bsz * seq, d)
    c2 = ctx.reshape(bsz * n_ctx, d)
    for l in range(depth):
        need_ctx = l < depth - 1
        proj_lat, u_lat = _inproj(x2, mods, lat_row(tm), w_in_k, l, tm, N_IN)
        proj_ctx, u_ctx = _inproj(c2, mods, ctx_mod_row, w_in_k, l, tm_ctx, tn_ctx,
                                  first_block=0 if need_ctx else _CTX_KV_BLOCK)

        ys5_ctx, ys5_lat = _s5(u_ctx, u_lat, kin, vin, win, a_t, l, bsz)
        yret_lat, yret_ctx = _retention(proj_lat, proj_ctx, log_gamma, cos, sin, pmat, l, bsz, seq, n_ctx, need_ctx)
        yna_lat = _na(proj_lat, proj_ctx, na_table, l, bsz, seq, n_ctx)

        x2 = _merge(x2, proj_lat, ys5_lat, yret_lat, yna_lat, mods, lat_row(tm), l, merge_w, tm)
        x2 = _ffn(x2, mods, lat_row(tm), l, *ffn_w, tm, final=not need_ctx)
        if need_ctx:
            yna_ctx = _ctx_attn(proj_ctx, bsz, n_ctx)
            c2 = _merge(c2, proj_ctx, ys5_ctx, yret_ctx, yna_ctx, mods, ctx_mod_row, l, merge_w, tm)
            c2 = _ffn(c2, mods, ctx_mod_row, l, *ffn_w, tm, final=False)
    return x2.reshape(bsz, seq, d)
```

```python
import functools
import math

import numpy as np
import jax
import jax.numpy as jnp
from jax import lax
from jax.experimental import pallas as pl
from jax.experimental.pallas import tpu as pltpu

F32 = jnp.float32
BF16 = jnp.bfloat16
HIGHEST = lax.Precision.HIGHEST

D_MODEL = 1024
GRID_W = 64
S5_WIDTH = 512
S5_GROUP = 16
S5_GROUPS = S5_WIDTH // S5_GROUP
S5_STATE = 64
S5_CHUNK = 16
RET_HEADS = 4
RET_DK = 64
RET_DV = 128
RET_QK_WIDTH = RET_HEADS * RET_DK
RET_WIDTH = RET_HEADS * RET_DV
RET_T = 256
NA_HEADS = 8
NA_HEAD_DIM = 64
NA_WIDTH = NA_HEADS * NA_HEAD_DIM
NA_ROWS = 8
NA_COLS = 16
NA_QROWS = 4
NA_KROWS = 12
NA_NLAG = 2 * NA_ROWS
N_BRANCH = 3
ROPE_BASE = 10000.0
RMS_EPS = 1e-6
GN_EPS = 1e-5
NEG_INF = -1e30
LANES = 128
SUBLANES = 8
MOD_ROWS = 16

_REF_SPLIT = (S5_WIDTH, RET_QK_WIDTH, RET_WIDTH, NA_WIDTH, NA_WIDTH,
              RET_QK_WIDTH, RET_WIDTH, NA_WIDTH, N_BRANCH * D_MODEL)
_REF_NAMES = ('u', 'rk', 'rv', 'nk', 'nv', 'rq', 'rg', 'nq', 'gates')
_REF_OFF = dict(zip(_REF_NAMES, np.concatenate([[0], np.cumsum(_REF_SPLIT)[:-1]]).tolist()))
_REF_W = dict(zip(_REF_NAMES, _REF_SPLIT))
_MY_ORDER = ('gates', 'rq', 'rk', 'u', 'rv', 'nk', 'nv', 'rg', 'nq')
_CTX_KV_BLOCK = 1
_MY_OFF = {}
_o = 0
for _n in _MY_ORDER:
    _MY_OFF[_n] = _o
    _o += _REF_W[_n]
N_IN = _o
LOG2E = math.log2(math.e)
_COL_SCALE = {'nq': NA_HEAD_DIM ** -0.5 * LOG2E, 'rk': RET_DK ** -0.5}

VMEM_LIMIT = 56 * 1024 * 1024


def _cparams(sem):
    return pltpu.CompilerParams(dimension_semantics=sem, vmem_limit_bytes=VMEM_LIMIT)


def _sigmoid(x):
    return 1.0 / (1.0 + jnp.exp(-x))


def _rms(x):
    return x * lax.rsqrt(jnp.mean(x * x, axis=-1, keepdims=True) + RMS_EPS)


def _layer_spec(arr, l):
    nd = arr.ndim
    return pl.BlockSpec((1,) + arr.shape[1:], lambda *_: (l,) + (0,) * (nd - 1), pipeline_mode=pl.Buffered(1))


def _ada_kernel(c_ref, w_ref, b_ref, o_ref):
    c = c_ref[...]
    s = c * _sigmoid(c)
    o_ref[0] = jnp.dot(s, w_ref[0], preferred_element_type=F32, precision=HIGHEST) + b_ref[0]


def _ada(cvec, w_ada, b_ada):
    depth, d, n = w_ada.shape
    tn = 1536
    rows = cvec.shape[0]
    return pl.pallas_call(
        _ada_kernel,
        grid=(depth, n // tn),
        in_specs=[pl.BlockSpec((rows, d), lambda l, j: (0, 0)),
                  pl.BlockSpec((1, d, tn), lambda l, j: (l, 0, j)),
                  pl.BlockSpec((1, 1, tn), lambda l, j: (l, 0, j))],
        out_specs=pl.BlockSpec((1, rows, tn), lambda l, j: (l, 0, j)),
        out_shape=jax.ShapeDtypeStruct((depth, rows, n), F32),
        compiler_params=_cparams(("parallel", "parallel")),
        name="ada_mod",
    )(cvec, w_ada, b_ada.reshape(depth, 1, n))


_GPL = LANES // S5_GROUP
_NSLAB = S5_WIDTH // LANES


def _block_transpose(tiles):
    blk = lax.broadcasted_iota(jnp.int32, tiles[0].shape, 1) // S5_GROUP
    tiles = list(tiles)
    s = _GPL // 2
    while s >= 1:
        hi = (blk & s) != 0
        for a in range(_GPL):
            if a & s:
                continue
            b = a + s
            ta, tb = tiles[a], tiles[b]
            tiles[a] = jnp.where(hi, pltpu.roll(tb, s * S5_GROUP, 1), ta)
            tiles[b] = jnp.where(hi, tb, pltpu.roll(ta, LANES - s * S5_GROUP, 1))
        s //= 2
    return tiles


def _inproj_kernel(x_ref, sh_ref, sc_ref, w_ref, o_ref, ut_ref, h_ref, u_scr, *, u_off, first_block, single):
    jcol = pl.program_id(1)

    def normed():
        return (_rms(x_ref[...]) * (1.0 + sc_ref[0]) + sh_ref[0]).astype(BF16)

    if not single:
        @pl.when(jcol == first_block)
        def _():
            h_ref[...] = normed()

    def project():
        res = jnp.dot(normed() if single else h_ref[...], w_ref[0], preferred_element_type=F32)
        o_ref[...] = res.astype(o_ref.dtype)
        for j in range(_NSLAB):
            u_scr[j] = res[:, u_off + j * LANES:u_off + (j + 1) * LANES]
        nrow = u_scr.shape[1] // S5_CHUNK
        for j in range(_NSLAB):
            for half in range(2):
                v = [u_scr[j, pl.ds(half * _GPL + tl, nrow, stride=S5_CHUNK), :] for tl in range(_GPL)]
                for q, out in enumerate(_block_transpose(v)):
                    ut_ref[j * _GPL + q, :, half * LANES:(half + 1) * LANES] = out.astype(ut_ref.dtype)

    if first_block == 0:
        project()
    else:
        @pl.when(jcol < first_block)
        def _():
            o_ref[...] = jnp.zeros(o_ref.shape, o_ref.dtype)

        pl.when(jcol >= first_block)(project)


def _inproj(x2, mods, mod_row, w, l, tm, tn, first_block=0):
    m, d = x2.shape
    n = w.shape[2]
    base = l * MOD_ROWS * 6
    nj = n // tn
    u_off = _MY_OFF['u'] - (nj - 1) * tn
    assert 0 <= u_off and u_off + S5_WIDTH <= tn
    return pl.pallas_call(
        functools.partial(_inproj_kernel, u_off=u_off, first_block=first_block, single=nj == 1),
        grid=(m // tm, nj),
        in_specs=[pl.BlockSpec((tm, d), lambda i, j: (i, 0)),
                  pl.BlockSpec((1, 1, d), lambda i, j: (base + mod_row(i) * 6 + 0, 0, 0)),
                  pl.BlockSpec((1, 1, d), lambda i, j: (base + mod_row(i) * 6 + 1, 0, 0)),
                  pl.BlockSpec((1, d, tn), lambda i, j: (l, 0, jnp.maximum(j, first_block)))],
        out_specs=[pl.BlockSpec((tm, tn), lambda i, j: (i, j)),
                   pl.BlockSpec((S5_GROUPS, tm // S5_CHUNK, S5_CHUNK * S5_GROUP), lambda i, j: (0, i, 0))],
        out_shape=[jax.ShapeDtypeStruct((m, n), BF16),
                   jax.ShapeDtypeStruct((S5_GROUPS, m // S5_CHUNK, S5_CHUNK * S5_GROUP), BF16)],
        scratch_shapes=[pltpu.VMEM((tm, d), BF16), pltpu.VMEM((_NSLAB, tm, LANES), F32)],
        compiler_params=_cparams(("parallel", "arbitrary")),
        name="in_proj",
    )(x2, mods, mods, w)


def _s5w_kernel(lam_ref, btr_ref, bti_ref, ctr_ref, cti_ref, dd_ref, kin_ref, vin_ref, win_ref, at_ref, *, ng):
    t_n, h_n = S5_CHUNK, S5_GROUP
    width = t_n * h_n
    lane = lax.broadcasted_iota(jnp.int32, (1, LANES), 1)
    f_lane = lane < S5_STATE
    tau = lax.broadcasted_iota(jnp.int32, (3 * SUBLANES, LANES), 0).astype(F32)
    lane_w = lax.broadcasted_iota(jnp.int32, (h_n, width), 1)
    for g in range(ng):
        lam_re, lam_im, dt = lam_ref[g, 0:1, :], lam_ref[g, 1:2, :], lam_ref[g, 2:3, :]
        mag = jnp.exp(tau * (lam_re * dt))
        ang = tau * (lam_im * dt)
        pr, pi = mag * jnp.cos(ang), mag * jnp.sin(ang)
        ab_re, ab_im = pr[1:2], pi[1:2]
        den = lam_re * lam_re + lam_im * lam_im
        f_re = ((ab_re - 1.0) * lam_re + ab_im * lam_im) / den
        f_im = (ab_im * lam_re - (ab_re - 1.0) * lam_im) / den
        btr, bti = btr_ref[g], bti_ref[g]
        bbr = f_re * btr - f_im * bti
        bbi = f_re * bti + f_im * btr
        ctr, cti = ctr_ref[g], cti_ref[g]

        def powers(pf, pb):
            rr = [jnp.broadcast_to(jnp.where(f_lane, pr[pf[t]:pf[t] + 1], pr[pb[t]:pb[t] + 1]), (h_n, LANES))
                  for t in range(t_n)]
            ri = [jnp.broadcast_to(jnp.where(f_lane, pi[pf[t]:pf[t] + 1], pi[pb[t]:pb[t] + 1]), (h_n, LANES))
                  for t in range(t_n)]
            return jnp.concatenate(rr, axis=0), jnp.concatenate(ri, axis=0)

        tile = lambda a: jnp.concatenate([a] * t_n, axis=0)
        bbr_t, bbi_t, ctr_t, cti_t = tile(bbr), tile(bbi), tile(ctr), tile(cti)

        xr, xi = powers([t_n - 1 - t for t in range(t_n)], list(range(t_n)))
        vin = jnp.concatenate([xr * bbr_t - xi * bbi_t, xr * bbi_t + xi * bbr_t], axis=1)
        vin_ref[g] = vin.astype(vin_ref.dtype)

        yr, yi = powers([t + 1 for t in range(t_n)], [t_n - t for t in range(t_n)])
        win_t = jnp.concatenate([ctr_t * yr - cti_t * yi, -(ctr_t * yi + cti_t * yr)], axis=1)
        win_ref[g] = win_t.T.astype(win_ref.dtype)

        zr, zi = powers(list(range(t_n)), [t_n - 1 - t for t in range(t_n)])
        fmat = jnp.concatenate([ctr_t * zr - cti_t * zi, ctr_t * zi + cti_t * zr], axis=1)
        dn = (((1,), (1,)), ((), ()))
        lhs_f = jnp.concatenate([jnp.where(f_lane, bbr, 0.0), jnp.where(f_lane, -bbi, 0.0)], axis=1)
        lhs_b = jnp.concatenate([jnp.where(f_lane, 0.0, bbr), jnp.where(f_lane, 0.0, -bbi)], axis=1)
        w_f = lax.dot_general(lhs_f, fmat, dn, preferred_element_type=F32, precision=HIGHEST)
        w_b = lax.dot_general(lhs_b, fmat, dn, preferred_element_type=F32, precision=HIGHEST)
        w_f = w_f + jnp.concatenate([dd_ref[g], jnp.zeros((h_n, width - LANES), F32)], axis=1)
        blocks = []
        for t in range(t_n):
            sh_f = t * h_n
            sh_b = (width - (t_n - 1 - t) * h_n) % width
            fw = w_f if sh_f == 0 else pltpu.roll(w_f, sh_f, 1)
            bw = w_b if sh_b == 0 else pltpu.roll(w_b, sh_b, 1)
            blocks.append(jnp.where(lane_w >= t * h_n, fw, 0.0) + jnp.where(lane_w < (t + 1) * h_n, bw, 0.0))
        kin_ref[g] = jnp.concatenate(blocks, axis=0).astype(kin_ref.dtype)
        at_ref[g] = jnp.broadcast_to(jnp.concatenate([pr[t_n:t_n + 1], pi[t_n:t_n + 1]], axis=1),
                                     (SUBLANES, 2 * LANES))


def _s5_weights(lam_re, lam_im, log_dt, b_re, b_im, c_re, c_im, d_skip, ng=4):
    depth, _, g_n, p_n = lam_re.shape
    h_n = S5_GROUP
    n = depth * g_n
    width = S5_CHUNK * h_n
    pair = lambda a: a.astype(F32).transpose(0, 2, 1, 3).reshape(n, 1, 2 * p_n)
    dt = jnp.broadcast_to(jnp.exp(log_dt.astype(F32))[..., None], lam_re.shape)
    lam = jnp.concatenate([pair(lam_re), pair(lam_im), pair(dt), jnp.zeros((n, SUBLANES - 3, 2 * p_n), F32)], axis=1)
    bt = lambda a: a.astype(F32).transpose(0, 2, 4, 1, 3).reshape(n, h_n, 2 * p_n)
    ct = lambda a: a.astype(F32).transpose(0, 2, 3, 1, 4).reshape(n, h_n, 2 * p_n)
    dd = jnp.eye(h_n, LANES, dtype=F32)[None] * d_skip.astype(F32).reshape(n, h_n, 1)
    small = pl.BlockSpec((ng, h_n, LANES), lambda i: (i, 0, 0))
    big = pl.BlockSpec((ng, width, width), lambda i: (i, 0, 0))
    return pl.pallas_call(
        functools.partial(_s5w_kernel, ng=ng),
        grid=(n // ng,),
        in_specs=[pl.BlockSpec((ng, SUBLANES, LANES), lambda i: (i, 0, 0)), small, small, small, small, small],
        out_specs=[big, big, big, pl.BlockSpec((ng, SUBLANES, width), lambda i: (i, 0, 0))],
        out_shape=[jax.ShapeDtypeStruct((n, width, width), BF16)] * 3
                  + [jax.ShapeDtypeStruct((n, SUBLANES, width), F32)],
        compiler_params=_cparams(("parallel",)),
        name="s5_weights",
    )(lam, bt(b_re), bt(b_im), ct(c_re), ct(c_im), dd)


def _s5_pitch(n):
    p = -(-n // SUBLANES)
    return (p | 1) * SUBLANES


def _s5_kernel(uc_ref, ul_ref, kin_ref, vin_ref, win_ref, a_ref, yc_ref, yl_ref,
               sc_scr, sl_scr, xac_scr, xal_scr, xbc_scr, xbl_scr, *, ng, ncc, ncl, bsz):
    half = 2 * S5_STATE
    pc, plat = _s5_pitch(ncc), _s5_pitch(ncl)
    segs = ((uc_ref, yc_ref, sc_scr, xac_scr, xbc_scr, ncc, pc), (ul_ref, yl_ref, sl_scr, xal_scr, xbl_scr, ncl, plat))
    for u_ref, _, s_scr, _, _, n, pitch in segs:
        for g in range(ng):
            s = jnp.dot(u_ref[g], vin_ref[g], preferred_element_type=F32)
            for b in range(bsz):
                for k in range(2):
                    s_scr[g, k, pl.ds(b * pitch, n), :] = s[b * n:(b + 1) * n, k * half:(k + 1) * half]
    lane = lax.broadcasted_iota(jnp.int32, (bsz, half), 1)
    fwd_lane = lane < S5_STATE
    a_re = [a_ref[g, :, :half] for g in range(ng)]
    a_im = [a_ref[g, :, half:] for g in range(ng)]

    def make_step(s_scr, xa_scr, xb_scr, pitch):
        def step(fc, bc, xs):
            rf = pl.ds(fc, bsz, stride=pitch)
            rb = pl.ds(bc, bsz, stride=pitch)
            out = []
            for g in range(ng):
                xr, xi = xs[2 * g], xs[2 * g + 1]
                xa_scr[g, 0, rf, :] = xr
                xa_scr[g, 1, rf, :] = xi
                xb_scr[g, 0, rb, :] = xr
                xb_scr[g, 1, rb, :] = xi
                sr = jnp.where(fwd_lane, s_scr[g, 0, rf, :], s_scr[g, 0, rb, :])
                si = jnp.where(fwd_lane, s_scr[g, 1, rf, :], s_scr[g, 1, rb, :])
                out.append(a_re[g] * xr - a_im[g] * xi + sr)
                out.append(a_re[g] * xi + a_im[g] * xr + si)
            return tuple(out)
        return step

    step_c = make_step(sc_scr, xac_scr, xbc_scr, pc)
    step_l = make_step(sl_scr, xal_scr, xbl_scr, plat)
    xs = tuple(jnp.zeros((bsz, half), F32) for _ in range(2 * ng))
    xs = lax.fori_loop(0, ncc, lambda i, c: step_c(i, ncc - 1 - i, c), xs, unroll=4)
    xs = lax.fori_loop(0, ncl, lambda i, c: step_l(i, ncl - 1 - i, c), xs, unroll=4)
    for u_ref, y_ref, _, xa_scr, xb_scr, n, pitch in segs:
        fwd_r = lax.broadcasted_iota(jnp.int32, (n, half), 1) < S5_STATE
        for g in range(ng):
            rows = []
            for b in range(bsz):
                r = pl.ds(b * pitch, n)
                rows.append(jnp.concatenate([jnp.where(fwd_r, xa_scr[g, k, r, :], xb_scr[g, k, r, :]) for k in range(2)],
                                            axis=1))
            x_in = jnp.concatenate(rows, axis=0).astype(BF16)
            y = (jnp.dot(u_ref[g], kin_ref[g], preferred_element_type=F32)
                 + jnp.dot(x_in, win_ref[g], preferred_element_type=F32))
            y_ref[g] = y.astype(y_ref.dtype)


def _s5(u_ctx, u_lat, kin, vin, win, a_t, l, bsz, ng=4):
    g_n, rc, w = u_ctx.shape
    rl = u_lat.shape[1]
    ncc, ncl = rc // bsz, rl // bsz
    nblk = g_n // ng
    wspec = pl.BlockSpec((ng, w, w), lambda i: (l * nblk + i, 0, 0))
    uspec = lambda r: pl.BlockSpec((ng, r, w), lambda i: (i, 0, 0))
    scr = lambda n: pltpu.VMEM((ng, 2, bsz * _s5_pitch(n), LANES), F32)
    return pl.pallas_call(
        functools.partial(_s5_kernel, ng=ng, ncc=ncc, ncl=ncl, bsz=bsz),
        grid=(nblk,),
        in_specs=[uspec(rc), uspec(rl), wspec, wspec, wspec,
                  pl.BlockSpec((ng, SUBLANES, w), lambda i: (l * nblk + i, 0, 0))],
        out_specs=[uspec(rc), uspec(rl)],
        out_shape=[jax.ShapeDtypeStruct(u_ctx.shape, BF16), jax.ShapeDtypeStruct(u_lat.shape, BF16)],
        scratch_shapes=[scr(ncc), scr(ncl)] * 3,
        compiler_params=_cparams(("parallel",)),
        name="s5_mixer",
    )(u_ctx, u_lat, kin, vin, win, a_t)


def _rotary_tables(seq):
    quarter = RET_DK // 4
    pos = jnp.arange(seq)
    inv_freq = ROPE_BASE ** (-jnp.arange(quarter, dtype=F32) / quarter)
    ang_r = (pos // GRID_W).astype(F32)[:, None] * inv_freq[None, :]
    ang_c = (pos % GRID_W).astype(F32)[:, None] * inv_freq[None, :]
    cos = jnp.concatenate([jnp.cos(ang_r)] * 2 + [jnp.cos(ang_c)] * 2, axis=-1)
    sin = jnp.concatenate([jnp.sin(ang_r)] * 2 + [jnp.sin(ang_c)] * 2, axis=-1)
    cos = jnp.tile(cos, (1, RET_HEADS))
    sin = jnp.tile(sin, (1, RET_HEADS))
    p = np.zeros((RET_QK_WIDTH, RET_QK_WIDTH), np.float32)
    for d in range(RET_QK_WIDTH):
        if d % (2 * quarter) < quarter:
            p[d + quarter, d] = -1.0
        else:
            p[d - quarter, d] = 1.0
    return cos, sin, jnp.asarray(p, BF16)


def _ret_kernel(lg_ref, q_ref, k_ref, v_ref, g_ref, cq_ref, ck_ref, cv_ref, cg_ref,
                cos_ref, sin_ref, p_ref, o_ref, co_ref, krot_scr, sin_scr, dm_scr, *, layer, need_ctx, ncl):
    t = RET_T
    qkw, vw = RET_QK_WIDTH, RET_WIDTH

    def per_head(shape, axis, width, d):
        head = lax.broadcasted_iota(jnp.int32, shape, axis) // width
        out = jnp.zeros(shape, F32)
        for h in range(RET_HEADS):
            out = jnp.where(head == h, lg_ref[layer, d * RET_HEADS + h], out)
        return out

    row = lax.broadcasted_iota(jnp.int32, (t, qkw), 0).astype(F32)
    lgf = per_head((t, qkw), 1, RET_DK, 0)
    lgb = per_head((t, qkw), 1, RET_DK, 1)
    qdec_f = jnp.exp((row + 1.0) * lgf)
    qdec_b = jnp.exp((t - row) * lgb)
    kdec_f = jnp.exp((t - 1.0 - row) * lgf)
    kdec_b = jnp.exp(row * lgb)
    cdec_f = jnp.exp(float(t) * per_head((qkw, vw), 0, RET_DK, 0))
    cdec_b = jnp.exp(float(t) * per_head((qkw, vw), 0, RET_DK, 1))
    blk = (lax.broadcasted_iota(jnp.int32, (qkw, vw), 0) // RET_DK
           == lax.broadcasted_iota(jnp.int32, (qkw, vw), 1) // RET_DV)
    head_lane = lax.broadcasted_iota(jnp.int32, (t, qkw), 1) // RET_DK

    ii = lax.broadcasted_iota(jnp.int32, (t, t), 0)
    jj = lax.broadcasted_iota(jnp.int32, (t, t), 1)
    dif = (ii - jj).astype(F32)
    for h in range(RET_HEADS):
        df = jnp.where(dif >= 0, jnp.exp(jnp.where(dif >= 0, dif, 0.0) * lg_ref[layer, h]), 0.0)
        db = jnp.where(dif < 0, jnp.exp(jnp.where(dif < 0, -dif, 0.0) * lg_ref[layer, RET_HEADS + h]), 0.0)
        dm_scr[h] = df + db

    def rotary(x_bf, c):
        rows = pl.ds(c * t, t)
        swapped = jnp.dot(x_bf, p_ref[...], preferred_element_type=F32)
        return x_bf.astype(F32) * cos_ref[rows, :] + swapped * sin_ref[rows, :]

    krot_scr[pl.ds(0, t), :] = ck_ref[...]
    for c in range(ncl):
        krot_scr[pl.ds((c + 1) * t, t), :] = rotary(k_ref[pl.ds(c * t, t), :], c).astype(BF16)

    def v_chunk(c):
        return cv_ref[...] if c == 0 else v_ref[pl.ds((c - 1) * t, t), :]

    def kv(c, kdec):
        kd = (krot_scr[pl.ds(c * t, t), :].astype(F32) * kdec).astype(BF16)
        return lax.dot_general(kd, v_chunk(c), (((0,), (0,)), ((), ())), preferred_element_type=F32)

    s = jnp.zeros((qkw, vw), F32)
    for c in range(ncl + 1):
        sin_scr[c, pl.ds(0, qkw), :] = jnp.where(blk, s, 0.0).astype(BF16)
        if c < ncl:
            s = cdec_f * s + kv(c, kdec_f)
    sin_scr[0, pl.ds(qkw, qkw), :] = jnp.zeros((qkw, vw), BF16)
    s = kv(0, kdec_b)
    for c in range(ncl, 0, -1):
        sin_scr[c, pl.ds(qkw, qkw), :] = jnp.where(blk, s, 0.0).astype(BF16)
        if c > 1:
            s = cdec_b * s + kv(c, kdec_b)

    for c in range(0 if need_ctx else 1, ncl + 1):
        if c == 0:
            q = cq_ref[...].astype(F32)
            gate = cg_ref[...].astype(F32)
        else:
            q = rotary(q_ref[pl.ds((c - 1) * t, t), :], c - 1)
            gate = g_ref[pl.ds((c - 1) * t, t), :].astype(F32)
        q_bf = q.astype(BF16)
        cross = (jnp.dot((q * qdec_f).astype(BF16), sin_scr[c, pl.ds(0, qkw), :], preferred_element_type=F32)
                 + jnp.dot((q * qdec_b).astype(BF16), sin_scr[c, pl.ds(qkw, qkw), :], preferred_element_type=F32))
        k_c = krot_scr[pl.ds(c * t, t), :]
        v_c = v_chunk(c)
        outs = []
        q_heads = jnp.concatenate([jnp.where(head_lane == h, q_bf, jnp.zeros_like(q_bf)) for h in range(RET_HEADS)],
                                  axis=0)
        sc_heads = lax.dot_general(q_heads, k_c, (((1,), (1,)), ((), ())), preferred_element_type=F32)
        for h in range(RET_HEADS):
            sc = (sc_heads[h * t:(h + 1) * t] * dm_scr[h]).astype(BF16)
            o = (jnp.dot(sc, v_c[:, h * RET_DV:(h + 1) * RET_DV], preferred_element_type=F32)
                 + cross[:, h * RET_DV:(h + 1) * RET_DV])
            mu = jnp.mean(o, axis=-1, keepdims=True)
            var = jnp.mean(jnp.square(o - mu), axis=-1, keepdims=True)
            outs.append((o - mu) * lax.rsqrt(var + GN_EPS))
        y = gate * _sigmoid(gate) * jnp.concatenate(outs, axis=1)
        if c == 0:
            co_ref[...] = y.astype(co_ref.dtype)
        else:
            o_ref[pl.ds((c - 1) * t, t), :] = y.astype(o_ref.dtype)
    if not need_ctx:
        co_ref[...] = jnp.zeros(co_ref.shape, co_ref.dtype)


def _retention(proj_lat, proj_ctx, lg, cos, sin, pmat, l, bsz, seq, n_ctx, need_ctx):
    t = RET_T
    ncl = seq // t
    assert n_ctx == t
    qb, kb = _MY_OFF['rq'] // RET_QK_WIDTH, _MY_OFF['rk'] // RET_QK_WIDTH
    vb, gb = _MY_OFF['rv'] // RET_WIDTH, _MY_OFF['rg'] // RET_WIDTH

    def col(n, w, j):
        return pl.BlockSpec((n, w), lambda b, j=j: (b, j))

    const = lambda shape: pl.BlockSpec(shape, lambda b: (0,) * len(shape))
    return pl.pallas_call(
        functools.partial(_ret_kernel, layer=l, need_ctx=need_ctx, ncl=ncl),
        grid=(bsz,),
        in_specs=[pl.BlockSpec(memory_space=pltpu.SMEM),
                  col(seq, RET_QK_WIDTH, qb), col(seq, RET_QK_WIDTH, kb), col(seq, RET_WIDTH, vb), col(seq, RET_WIDTH, gb),
                  col(n_ctx, RET_QK_WIDTH, qb), col(n_ctx, RET_QK_WIDTH, kb), col(n_ctx, RET_WIDTH, vb), col(n_ctx, RET_WIDTH, gb),
                  const((seq, RET_QK_WIDTH)), const((seq, RET_QK_WIDTH)), const((RET_QK_WIDTH, RET_QK_WIDTH))],
        out_specs=[pl.BlockSpec((seq, RET_WIDTH), lambda b: (b, 0)),
                   pl.BlockSpec((n_ctx, RET_WIDTH), lambda b: (b, 0))],
        out_shape=[jax.ShapeDtypeStruct((bsz * seq, RET_WIDTH), BF16),
                   jax.ShapeDtypeStruct((bsz * n_ctx, RET_WIDTH), BF16)],
        scratch_shapes=[pltpu.VMEM((seq + n_ctx, RET_QK_WIDTH), BF16),
                        pltpu.VMEM((ncl + 1, 2 * RET_QK_WIDTH, RET_WIDTH), BF16),
                        pltpu.VMEM((RET_HEADS, t, t), F32)],
        compiler_params=_cparams(("parallel",)),
        name="retention",
    )(lg, proj_lat, proj_lat, proj_lat, proj_lat, proj_ctx, proj_ctx, proj_ctx, proj_ctx, cos, sin, pmat)


def _na_block_start(kblk, rows):
    return jnp.clip(kblk * NA_QROWS - NA_ROWS // 2, 0, rows - NA_KROWS)


def _na_bias_table(rpb):
    depth, heads, nlag, ncol = rpb.shape
    cols = np.arange(GRID_W)
    cs = np.clip(cols - NA_COLS // 2, 0, GRID_W - NA_COLS)
    valid_c = (cols[None, :] >= cs[:, None]) & (cols[None, :] < cs[:, None] + NA_COLS)
    pad = GRID_W - 1
    padded = jnp.pad(rpb.astype(F32) * LOG2E, ((0, 0), (0, 0), (0, 0), (pad, pad)))
    toe = jnp.stack([padded[..., pad + NA_COLS - 1 - qc: pad + NA_COLS - 1 - qc + GRID_W] for qc in range(GRID_W)],
                    axis=-2)
    toe = jnp.where(jnp.asarray(valid_c), toe, NEG_INF)
    toe = jnp.pad(toe, ((0, 0), (0, 0), (1, NA_NLAG + 1 - nlag - 1), (0, 0), (0, 0)))
    table = jnp.concatenate([toe[:, :, :NA_NLAG], toe[:, :, 1:NA_NLAG + 1]], axis=-1)
    return table.reshape(depth, heads * NA_NLAG, GRID_W, 2 * GRID_W)


def _attend(q_pair, k_list, v_list, bias_list):
    nq = q_pair.shape[0]
    lane = lax.broadcasted_iota(jnp.int32, q_pair.shape, 1) // NA_HEAD_DIM
    zero = jnp.zeros_like(q_pair)
    q2 = jnp.concatenate([jnp.where(lane == 0, q_pair, zero), jnp.where(lane == 1, q_pair, zero)], axis=0)
    ss = []
    for k_i, b_i in zip(k_list, bias_list):
        s = lax.dot_general(q2, k_i, (((1,), (1,)), ((), ())), preferred_element_type=F32)
        if b_i is not None:
            s = s + jnp.concatenate([b_i(0), b_i(1)], axis=0)
        ss.append(s)
    m = ss[0].max(axis=-1, keepdims=True)
    for s in ss[1:]:
        m = jnp.maximum(m, s.max(axis=-1, keepdims=True))
    ps = [jnp.exp2(s - m) for s in ss]
    den = ps[0].sum(axis=-1, keepdims=True)
    for p in ps[1:]:
        den = den + p.sum(axis=-1, keepdims=True)
    acc = jnp.dot(ps[0].astype(BF16), v_list[0], preferred_element_type=F32)
    for p, v_i in zip(ps[1:], v_list[1:]):
        acc = acc + jnp.dot(p.astype(BF16), v_i, preferred_element_type=F32)
    out = acc / den
    return jnp.where(lane == 0, out[:nq], out[nq:])


def _attend_window(q_pair, k_win, k_ctx, v_win, v_ctx, tab_ref, hp, rel, starts):
    nq = q_pair.shape[0]
    npair = k_win.shape[0] // LANES
    lane = lax.broadcasted_iota(jnp.int32, q_pair.shape, 1) // NA_HEAD_DIM
    zero = jnp.zeros_like(q_pair)
    dn = (((1,), (1,)), ((), ()))
    q2 = jnp.concatenate([jnp.where(lane == 0, q_pair, zero), jnp.where(lane == 1, q_pair, zero)], axis=0)
    s_win = lax.dot_general(q2, k_win, dn, preferred_element_type=F32)
    s_ctx = lax.dot_general(q2, k_ctx, dn, preferred_element_type=F32)
    left = lax.broadcasted_iota(jnp.int32, (GRID_W, LANES), 1) < GRID_W
    neg_left = jnp.where(left, NEG_INF, 0.0)
    neg_right = jnp.where(left, 0.0, NEG_INF)
    p_win, p_ctx, inv = [], [], []
    for hh in range(2):
        h = 2 * hp + hh
        for qrl in range(NA_QROWS):
            r = slice(hh * nq + qrl * GRID_W, hh * nq + (qrl + 1) * GRID_W)
            st = starts[qrl]
            lo, hi = st // 2, (st + NA_ROWS + 1) // 2
            tiles = []
            for kp in range(lo, hi):
                lag = rel + 2 * kp - qrl + NA_ROWS - 1
                assert -1 <= lag <= NA_NLAG - 2
                t = tab_ref[0, h * NA_NLAG + lag + 1]
                if 2 * kp < st:
                    t = t + neg_left
                if 2 * kp + 1 >= st + NA_ROWS:
                    t = t + neg_right
                tiles.append(t)
            sw = s_win[r, lo * LANES:hi * LANES] + jnp.concatenate(tiles, axis=1)
            sc = s_ctx[r]
            m = jnp.maximum(sw.max(axis=-1, keepdims=True), sc.max(axis=-1, keepdims=True))
            pw, pc = jnp.exp2(sw - m), jnp.exp2(sc - m)
            inv.append(1.0 / (pw.sum(axis=-1, keepdims=True) + pc.sum(axis=-1, keepdims=True)))
            parts = [pw.astype(BF16)]
            if lo > 0:
                parts.insert(0, jnp.zeros((GRID_W, lo * LANES), BF16))
            if hi < npair:
                parts.append(jnp.zeros((GRID_W, (npair - hi) * LANES), BF16))
            p_win.append(jnp.concatenate(parts, axis=1) if len(parts) > 1 else parts[0])
            p_ctx.append(pc.astype(BF16))
    acc = (jnp.dot(jnp.concatenate(p_win, axis=0), v_win, preferred_element_type=F32)
           + jnp.dot(jnp.concatenate(p_ctx, axis=0), v_ctx, preferred_element_type=F32))
    out = jnp.concatenate([acc[i * GRID_W:(i + 1) * GRID_W] * inv[i] for i in range(len(inv))], axis=0)
    return jnp.where(lane == 0, out[:nq], out[nq:])


def _na_kernel(q_ref, k_ref, v_ref, ck_ref, cv_ref, tab_ref, o_ref, *, rows):
    nk = NA_KROWS * GRID_W
    nblk = rows // NA_QROWS
    kblk = pl.program_id(1)
    ks = pl.multiple_of(_na_block_start(kblk, rows) * GRID_W, NA_QROWS * GRID_W)

    def run(kb):
        r0 = kb * NA_QROWS
        ks_row = min(max(r0 - NA_ROWS // 2, 0), rows - NA_KROWS)
        starts = [min(max(r0 + qrl - NA_ROWS // 2, 0), rows - NA_ROWS) - ks_row for qrl in range(NA_QROWS)]
        for hp in range(NA_HEADS // 2):
            ln = pl.ds(hp * LANES, LANES)
            y = _attend_window(q_ref[:, ln], k_ref[pl.ds(ks, nk), ln], ck_ref[:, ln],
                               v_ref[pl.ds(ks, nk), ln], cv_ref[:, ln], tab_ref, hp, ks_row - r0, starts)
            o_ref[:, ln] = y.astype(o_ref.dtype)

    assert nblk >= 3 and all(_na_geometry(kb, rows) == _na_geometry(1, rows) for kb in range(1, nblk - 1))
    pl.when(kblk == 0)(lambda: run(0))
    pl.when((kblk > 0) & (kblk < nblk - 1))(lambda: run(1))
    pl.when(kblk == nblk - 1)(lambda: run(nblk - 1))


def _na_geometry(kb, rows):
    r0 = kb * NA_QROWS
    ks_row = min(max(r0 - NA_ROWS // 2, 0), rows - NA_KROWS)
    return (ks_row - r0,) + tuple(min(max(r0 + qrl - NA_ROWS // 2, 0), rows - NA_ROWS) - ks_row
                                  for qrl in range(NA_QROWS))


def _na(proj_lat, proj_ctx, table, l, bsz, seq, n_ctx):
    rows = seq // GRID_W
    nq = NA_QROWS * GRID_W
    nblk = seq // nq
    qb, kb, vb = (_MY_OFF[n] // NA_WIDTH for n in ('nq', 'nk', 'nv'))
    return pl.pallas_call(
        functools.partial(_na_kernel, rows=rows),
        grid=(bsz, nblk),
        in_specs=[pl.BlockSpec((nq, NA_WIDTH), lambda b, k: (b * nblk + k, qb)),
                  pl.BlockSpec((seq, NA_WIDTH), lambda b, k: (b, kb)),
                  pl.BlockSpec((seq, NA_WIDTH), lambda b, k: (b, vb)),
                  pl.BlockSpec((n_ctx, NA_WIDTH), lambda b, k: (b, kb)),
                  pl.BlockSpec((n_ctx, NA_WIDTH), lambda b, k: (b, vb)),
                  pl.BlockSpec((1,) + table.shape[1:], lambda b, k: (l, 0, 0, 0))],
        out_specs=pl.BlockSpec((nq, NA_WIDTH), lambda b, k: (b * nblk + k, 0)),
        out_shape=jax.ShapeDtypeStruct((bsz * seq, NA_WIDTH), BF16),
        compiler_params=_cparams(("parallel", "arbitrary")),
        name="neighborhood_attention",
    )(proj_lat, proj_lat, proj_lat, proj_ctx, proj_ctx, table)


def _ctx_attn_kernel(q_ref, k_ref, v_ref, o_ref):
    for hp in range(NA_HEADS // 2):
        ln = pl.ds(hp * LANES, LANES)
        y = _attend(q_ref[:, ln], [k_ref[:, ln]], [v_ref[:, ln]], [None])
        o_ref[:, ln] = y.astype(o_ref.dtype)


def _ctx_attn(proj_ctx, bsz, n_ctx):
    qb, kb, vb = (_MY_OFF[n] // NA_WIDTH for n in ('nq', 'nk', 'nv'))
    spec = lambda j: pl.BlockSpec((n_ctx, NA_WIDTH), lambda b: (b, j))
    return pl.pallas_call(
        _ctx_attn_kernel,
        grid=(bsz,),
        in_specs=[spec(qb), spec(kb), spec(vb)],
        out_specs=pl.BlockSpec((n_ctx, NA_WIDTH), lambda b: (b, 0)),
        out_shape=jax.ShapeDtypeStruct((bsz * n_ctx, NA_WIDTH), BF16),
        compiler_params=_cparams(("parallel",)),
        name="context_attention",
    )(proj_ctx, proj_ctx, proj_ctx)


def _gelu_tanh(x):
    return 0.5 * x * (1.0 + jnp.tanh(math.sqrt(2.0 / math.pi) * (x + 0.044715 * (x * x * x))))


def _merge_kernel(x_ref, g0_ref, g1_ref, g2_ref, yt_ref, yret_ref, yna_ref, wglu_ref, bglu_ref,
                  wbs5_ref, wbret_ref, wbna_ref, wout_ref, gate_ref, o_ref, w_scr):
    nrow = yt_ref.shape[1]
    for j in range(_NSLAB):
        for half in range(2):
            o = [yt_ref[j * _GPL + q, :, half * LANES:(half + 1) * LANES].astype(F32) for q in range(_GPL)]
            for tl, out in enumerate(_block_transpose(o)):
                w_scr[j, pl.ds(half * _GPL + tl, nrow, stride=S5_CHUNK), :] = out
    ge = _gelu_tanh(jnp.concatenate([w_scr[j] for j in range(_NSLAB)], axis=1))
    z = jnp.dot(ge.astype(BF16), wglu_ref[0], preferred_element_type=F32) + bglu_ref[0]
    s5 = (ge * _sigmoid(z)).astype(BF16)
    m = (_sigmoid(g0_ref[...].astype(F32)) * jnp.dot(s5, wbs5_ref[0], preferred_element_type=F32)
         + _sigmoid(g1_ref[...].astype(F32)) * jnp.dot(yret_ref[...], wbret_ref[0], preferred_element_type=F32)
         + _sigmoid(g2_ref[...].astype(F32)) * jnp.dot(yna_ref[...], wbna_ref[0], preferred_element_type=F32))
    o_ref[...] = x_ref[...] + gate_ref[0] * jnp.dot(m.astype(BF16), wout_ref[0], preferred_element_type=F32)


def _merge(x2, proj, y_t, yret, yna, mods, mod_row, l, wts, tm):
    m, d = x2.shape
    base = l * MOD_ROWS * 6
    rowblk = lambda w, j=0: pl.BlockSpec((tm, w), lambda i, j=j: (i, j))
    return pl.pallas_call(
        _merge_kernel,
        grid=(m // tm,),
        in_specs=[rowblk(d), rowblk(d, 0), rowblk(d, 1), rowblk(d, 2),
                  pl.BlockSpec((S5_GROUPS, tm // S5_CHUNK, S5_CHUNK * S5_GROUP), lambda i: (0, i, 0)),
                  rowblk(RET_WIDTH), rowblk(NA_WIDTH)]
                 + [_layer_spec(w, l) for w in wts]
                 + [pl.BlockSpec((1, 1, d), lambda i: (base + mod_row(i) * 6 + 2, 0, 0))],
        out_specs=rowblk(d),
        out_shape=jax.ShapeDtypeStruct((m, d), F32),
        scratch_shapes=[pltpu.VMEM((_NSLAB, tm, LANES), F32)],
        compiler_params=_cparams(("parallel",)),
        name="merge_residual",
    )(x2, proj, proj, proj, y_t, yret, yna, *wts, mods)


def _ffn_kernel(x_ref, sh_ref, sc_ref, gate_ref, wg_ref, wu_ref, wd_ref, fn_ref, o_ref, *, final, th):
    x = x_ref[...]
    h = (_rms(x) * (1.0 + sc_ref[0]) + sh_ref[0]).astype(BF16)
    hidden = wg_ref.shape[2]
    acc = jnp.zeros(x.shape, F32)
    for j in range(hidden // th):
        a = jnp.dot(h, wg_ref[0, :, j * th:(j + 1) * th], preferred_element_type=F32)
        b = jnp.dot(h, wu_ref[0, :, j * th:(j + 1) * th], preferred_element_type=F32)
        act = (a * _sigmoid(a) * b).astype(BF16)
        acc = acc + jnp.dot(act, wd_ref[0, j * th:(j + 1) * th, :], preferred_element_type=F32)
    y = x + gate_ref[0] * acc
    if final:
        y = _rms(y) * fn_ref[...]
    o_ref[...] = y


def _ffn(x2, mods, mod_row, l, wg, wu, wd, fn, tm, final):
    m, d = x2.shape
    base = l * MOD_ROWS * 6
    modspec = lambda k: pl.BlockSpec((1, 1, d), lambda i, k=k: (base + mod_row(i) * 6 + k, 0, 0))
    return pl.pallas_call(
        functools.partial(_ffn_kernel, final=final, th=256),
        grid=(m // tm,),
        in_specs=[pl.BlockSpec((tm, d), lambda i: (i, 0)), modspec(3), modspec(4), modspec(5),
                  _layer_spec(wg, l), _layer_spec(wu, l), _layer_spec(wd, l),
                  pl.BlockSpec(fn.shape, lambda i: (0, 0))],
        out_specs=pl.BlockSpec((tm, d), lambda i: (i, 0)),
        out_shape=jax.ShapeDtypeStruct((m, d), F32),
        compiler_params=_cparams(("parallel",)),
        name="swiglu_residual",
    )(x2, mods, mods, mods, wg, wu, wd, fn)


def kernel(x, c, ctx, c_ctx, w_ada, b_ada, w_in, s5_lam_re, s5_lam_im, s5_log_dt, s5_b_re, s5_b_im,
           s5_c_re, s5_c_im, s5_d, s5_w_glu, s5_b_glu, ret_theta, na_rpb, w_branch_s5, w_branch_ret,
           w_branch_na, w_out, w_ffn_gate, w_ffn_up, w_ffn_down, final_norm):
    bsz, seq, d = x.shape
    n_ctx = ctx.shape[1]
    depth = w_ada.shape[0]
    ctx_row = bsz
    assert bsz == SUBLANES and bsz + 1 <= MOD_ROWS

    cvec = jnp.zeros((MOD_ROWS, d), F32).at[:bsz].set(c).at[ctx_row].set(c_ctx)
    mods = _ada(cvec, w_ada, b_ada).reshape(depth * MOD_ROWS * 6, 1, d)

    w_in_k = jnp.concatenate(
        [w_in[:, :, _REF_OFF[n]:_REF_OFF[n] + _REF_W[n]] * _COL_SCALE.get(n, 1.0) for n in _MY_ORDER],
        axis=2).astype(BF16)

    cos, sin, pmat = _rotary_tables(seq)
    log_gamma = jax.nn.log_sigmoid(ret_theta.astype(F32)).reshape(depth, 2 * RET_HEADS)
    kin, vin, win, a_t = _s5_weights(s5_lam_re, s5_lam_im, s5_log_dt, s5_b_re, s5_b_im, s5_c_re, s5_c_im, s5_d)
    na_table = _na_bias_table(na_rpb)
    merge_w = (s5_w_glu.astype(BF16), s5_b_glu.reshape(depth, 1, -1).astype(F32), w_branch_s5.astype(BF16),
               w_branch_ret.astype(BF16), w_branch_na.astype(BF16), w_out.astype(BF16))
    ffn_w = (w_ffn_gate.astype(BF16), w_ffn_up.astype(BF16), w_ffn_down.astype(BF16),
             final_norm.reshape(1, d).astype(F32))

    tm = 512
    tm_ctx, tn_ctx = 1024, N_IN // 2
    assert all(_MY_OFF[n] >= _CTX_KV_BLOCK * tn_ctx for n in ('u', 'rk', 'rv', 'nk', 'nv'))
    lat_row = lambda t: (lambda i: i // (seq // t))
    ctx_mod_row = lambda i: ctx_row

    x2 = x.reshape(bsz * seq, d)
    c2 = ctx.reshape(bsz * n_ctx, d)
    for l in range(depth):
        need_ctx = l < depth - 1
        proj_lat, u_lat = _inproj(x2, mods, lat_row(tm), w_in_k, l, tm, N_IN)
        proj_ctx, u_ctx = _inproj(c2, mods, ctx_mod_row, w_in_k, l, tm_ctx, tn_ctx,
                                  first_block=0 if need_ctx else _CTX_KV_BLOCK)

        ys5_ctx, ys5_lat = _s5(u_ctx, u_lat, kin, vin, win, a_t, l, bsz)
        yret_lat, yret_ctx = _retention(proj_lat, proj_ctx, log_gamma, cos, sin, pmat, l, bsz, seq, n_ctx, need_ctx)
        yna_lat = _na(proj_lat, proj_ctx, na_table, l, bsz, seq, n_ctx)

        x2 = _merge(x2, proj_lat, ys5_lat, yret_lat, yna_lat, mods, lat_row(tm), l, merge_w, tm)
        x2 = _ffn(x2, mods, lat_row(2 * tm), l, *ffn_w, 2 * tm, final=not need_ctx)
        if need_ctx:
            yna_ctx = _ctx_attn(proj_ctx, bsz, n_ctx)
            c2 = _merge(c2, proj_ctx, ys5_ctx, yret_ctx, yna_ctx, mods, ctx_mod_row, l, merge_w, tm)
            c2 = _ffn(c2, mods, ctx_mod_row, l, *ffn_w, tm, final=False)
    return x2.reshape(bsz, seq, d)
```
